```python
import math
import jax, jax.numpy as jnp
from jax import lax
import numpy as np

D_MODEL = 2048
BATCH = 8
SEQ = 2048
DEPTH = 2

N_MIXERS = 2
N_RWKV_LAYERS = (DEPTH + 1) // 2
N_DIFF_LAYERS = DEPTH // 2
NORM_EPS = 1e-6

RWKV_HEAD = 64
RWKV_HEADS = D_MODEL // RWKV_HEAD
RWKV_WIDTH = RWKV_HEADS * RWKV_HEAD
DECAY_LORA = 96
ICLR_LORA = 96
GN_EPS = 64e-5
N_SHIFT_PATHS = 6

DIFF_HEAD = 64
DIFF_HEADS = D_MODEL // (2 * DIFF_HEAD)
DIFF_WIDTH = DIFF_HEADS * 2 * DIFF_HEAD
SUBLN_EPS = 1e-5
Q_BLOCK = 128

kernel_name = "rwkv7_diffattn_interleaved_trunk"


def rms_norm(x, w, eps):
    xf = x.astype(jnp.float32)
    y = xf * lax.rsqrt(jnp.mean(xf * xf, axis=-1, keepdims=True) + eps)
    return (y * w.astype(jnp.float32)).astype(x.dtype)


def alibi_slopes(n_heads):
    return 2.0 ** (-8.0 * jnp.arange(1, n_heads + 1, dtype=jnp.float32) / n_heads)


def wkv7_step(S, inp):
    r_t, w_t, k_t, v_t, a_t, b_t = inp
    sa = jnp.einsum('bhvk,bhk->bhv', S, a_t)
    S = S * w_t[:, :, None, :] + sa[..., None] * b_t[:, :, None, :] + v_t[..., None] * k_t[:, :, None, :]
    y = jnp.einsum('bhvk,bhk->bhv', S, r_t)
    return S, y


def rwkv7_mixer(h, mu, w_in, w0, w1, w2, a0, a1, a2, k_k, k_a, r_k, ln_w, ln_b, w_out):
    B, T, D = h.shape
    H, N = RWKV_HEADS, RWKV_HEAD
    f32 = jnp.float32
    h_prev = jnp.pad(h, ((0, 0), (1, 0), (0, 0)))[:, :T]
    xs = h[None] + (h_prev - h)[None] * mu[:, None, None, :]
    x_r, x_w, x_k, x_v, x_a, x_g = xs[0], xs[1], xs[2], xs[3], xs[4], xs[5]
    proj = jnp.einsum('pbtd,dpe->pbte', jnp.stack([x_r, x_k, x_v, x_g]),
                      w_in.reshape(D, 4, RWKV_WIDTH))
    r, k, v, g_pre = proj[0], proj[1], proj[2], proj[3]
    w_log = -jax.nn.softplus(-(w0 + jnp.tanh(x_w @ w1) @ w2).astype(f32)) - 0.5
    decay = jnp.exp(-jnp.exp(w_log))
    iclr = jax.nn.sigmoid((a0 + (x_a @ a1) @ a2).astype(f32))

    def heads(z):
        return z.astype(f32).reshape(B, T, H, N)

    kk = heads(k * k_k)
    kk = kk / jnp.maximum(jnp.sqrt(jnp.sum(kk * kk, axis=-1, keepdims=True)), 1e-12)
    iclr_h = heads(iclr)
    k_h = heads(k) * (1.0 + (iclr_h - 1.0) * k_a.astype(f32).reshape(H, N))
    r_h, v_h, w_h = heads(r), heads(v), heads(decay)
    a_h = -kk
    b_h = kk * iclr_h
    seq = tuple(z.transpose(1, 0, 2, 3) for z in (r_h, w_h, k_h, v_h, a_h, b_h))
    S0 = jnp.zeros((B, H, N, N), f32)
    _, y = lax.scan(wkv7_step, S0, seq)
    y = y.transpose(1, 0, 2, 3)
    mean = jnp.mean(y, axis=-1, keepdims=True)
    var = jnp.mean(jnp.square(y - mean), axis=-1, keepdims=True)
    y = ((y - mean) * lax.rsqrt(var + GN_EPS)).reshape(B, T, D) * ln_w + ln_b
    bonus = jnp.sum(r_h * k_h * r_k.astype(f32), axis=-1, keepdims=True) * v_h
    y = y + bonus.reshape(B, T, D)
    y = (y * jax.nn.silu(g_pre.astype(f32))).astype(h.dtype)
    return y @ w_out


def diff_attn_mixer(h, w_in, lam, subln_w, w_out, lambda_init):
    B, T, D = h.shape
    H, d = DIFF_HEADS, DIFF_HEAD
    f32 = jnp.float32
    proj = h @ w_in
    q, k, v, g_pre = jnp.split(proj, 4, axis=-1)
    lam = lam.astype(f32)
    lam_full = (jnp.exp(jnp.sum(lam[0] * lam[1])) - jnp.exp(jnp.sum(lam[2] * lam[3]))
                + lambda_init)
    nb = T // Q_BLOCK
    qb = q.reshape(B, nb, Q_BLOCK, H, 2, d).transpose(1, 0, 3, 4, 2, 5)
    kt = k.reshape(B, T, H, 2, d).transpose(0, 2, 3, 1, 4)
    vt = v.reshape(B, T, H, 2 * d).transpose(0, 2, 1, 3)
    slopes = alibi_slopes(H)
    kpos = jnp.arange(T)
    scale = 1.0 / math.sqrt(d)

    def block(args):
        q_blk, start = args
        qpos = start + jnp.arange(Q_BLOCK)
        s = jnp.einsum('bhcqd,bhckd->bhcqk', q_blk, kt).astype(f32) * scale
        dist = (qpos[:, None] - kpos[None, :]).astype(f32)
        bias = jnp.where(dist >= 0, -slopes[:, None, None] * dist, -jnp.inf)
        p = jax.nn.softmax(s + bias[None, :, None], axis=-1)
        attn = p[:, :, 0] - lam_full * p[:, :, 1]
        return jnp.einsum('bhqk,bhke->bhqe', attn.astype(vt.dtype), vt)

    out = lax.map(block, (qb, jnp.arange(nb) * Q_BLOCK))
    out = out.transpose(1, 0, 3, 2, 4).reshape(B, T, H, 2 * d)
    out = rms_norm(out, subln_w, SUBLN_EPS) * (1.0 - lambda_init)
    out = out.reshape(B, T, DIFF_WIDTH) * jax.nn.silu(g_pre)
    return out @ w_out


def setup_inputs(seed: int = 0) -> dict:
    key = jax.random.key(seed)
    ks = jax.random.split(key, 24)
    nr, nd = N_RWKV_LAYERS, N_DIFF_LAYERS
    D = D_MODEL
    n = jax.random.normal
    return {
        "x": n(ks[0], (BATCH, SEQ, D), jnp.float32),
        "norm_w": 1.0 + 0.02 * n(ks[1], (DEPTH, D), jnp.float32),
        "rwkv_mu": jax.random.uniform(ks[2], (nr, N_SHIFT_PATHS, D), jnp.float32),
        "rwkv_w_in": n(ks[3], (nr, D, 4 * RWKV_WIDTH), jnp.float32) * D ** -0.5,
        "rwkv_w0": n(ks[4], (nr, D), jnp.float32) - 1.0,
        "rwkv_w1": n(ks[5], (nr, D, DECAY_LORA), jnp.float32) * D ** -0.5,
        "rwkv_w2": n(ks[6], (nr, DECAY_LORA, D), jnp.float32) * (0.1 * DECAY_LORA ** -0.5),
        "rwkv_a0": 0.1 * n(ks[7], (nr, D), jnp.float32),
        "rwkv_a1": n(ks[8], (nr, D, ICLR_LORA), jnp.float32) * D ** -0.5,
        "rwkv_a2": n(ks[9], (nr, ICLR_LORA, D), jnp.float32) * (0.1 * ICLR_LORA ** -0.5),
        "rwkv_k_k": 0.85 + 0.05 * n(ks[10], (nr, D), jnp.float32),
        "rwkv_k_a": 1.0 + 0.05 * n(ks[11], (nr, D), jnp.float32),
        "rwkv_r_k": 0.1 * n(ks[12], (nr, RWKV_HEADS, RWKV_HEAD), jnp.float32),
        "rwkv_ln_w": 1.0 + 0.02 * n(ks[13], (nr, D), jnp.float32),
        "rwkv_ln_b": 0.01 * n(ks[14], (nr, D), jnp.float32),
        "rwkv_w_out": n(ks[15], (nr, RWKV_WIDTH, D), jnp.float32) * RWKV_WIDTH ** -0.5,
        "diff_w_in": n(ks[16], (nd, D, 4 * DIFF_WIDTH), jnp.float32) * D ** -0.5,
        "diff_lambda": 0.1 * n(ks[17], (nd, 4, DIFF_HEAD), jnp.float32),
        "diff_subln_w": 1.0 + 0.02 * n(ks[18], (nd, 2 * DIFF_HEAD), jnp.float32),
        "diff_w_out": n(ks[19], (nd, DIFF_WIDTH, D), jnp.float32) * DIFF_WIDTH ** -0.5,
        "final_norm_w": 1.0 + 0.02 * n(ks[20], (D,), jnp.float32),
    }


def reference(x, norm_w, rwkv_mu, rwkv_w_in, rwkv_w0, rwkv_w1, rwkv_w2, rwkv_a0, rwkv_a1, rwkv_a2,
              rwkv_k_k, rwkv_k_a, rwkv_r_k, rwkv_ln_w, rwkv_ln_b, rwkv_w_out,
              diff_w_in, diff_lambda, diff_subln_w, diff_w_out, final_norm_w):
    for i in range(DEPTH):
        h = rms_norm(x, norm_w[i], NORM_EPS)
        j = i // N_MIXERS
        if i % N_MIXERS == 0:
            y = rwkv7_mixer(h, rwkv_mu[j], rwkv_w_in[j], rwkv_w0[j], rwkv_w1[j], rwkv_w2[j],
                            rwkv_a0[j], rwkv_a1[j], rwkv_a2[j], rwkv_k_k[j], rwkv_k_a[j], rwkv_r_k[j],
                            rwkv_ln_w[j], rwkv_ln_b[j], rwkv_w_out[j])
        else:
            lambda_init = 0.8 - 0.6 * math.exp(-0.3 * i)
            y = diff_attn_mixer(h, diff_w_in[j], diff_lambda[j], diff_subln_w[j], diff_w_out[j],
                                lambda_init)
        x = x + y
    return rms_norm(x, final_norm_w, NORM_EPS)
```

```python
import functools
import math

import jax
import jax.numpy as jnp
from jax import lax
from jax.experimental import pallas as pl
from jax.experimental.pallas import tpu as pltpu

F32 = jnp.float32
BF16 = jnp.bfloat16

NORM_EPS = 1e-6
GN_EPS = 64e-5
SUBLN_EPS = 1e-5
HEAD = 64
LANES = 128
CHUNK = 64
LORA_PAD = 128
VMEM_LIMIT = 48 * 1024 * 1024
NEG_BIG = -1e30


def _cparams(sem):
    return pltpu.CompilerParams(dimension_semantics=sem, vmem_limit_bytes=VMEM_LIMIT)


def _dot(a, b):
    return jnp.dot(a, b, preferred_element_type=F32)


def _dot_nt(a, b):
    return lax.dot_general(a, b, (((1,), (1,)), ((), ())), preferred_element_type=F32)


def _split2(x):
    hi = x.astype(BF16)
    lo = (x - hi.astype(F32)).astype(BF16)
    return hi, lo


def _mm(a, b, passes, nt=False):
    d = _dot_nt if nt else _dot
    if passes == 1:
        return d(a.astype(BF16), b.astype(BF16))
    ah, al = _split2(a)
    bh, bl = _split2(b)
    return d(ah, bh) + (d(al, bh) + d(ah, bl))


def _mm_exact_rhs(a, e_bf16, parts):
    acc = None
    rem = a
    for _ in range(parts):
        piece = rem.astype(BF16)
        term = _dot(piece, e_bf16)
        acc = term if acc is None else acc + term
        rem = rem - piece.astype(F32)
    return acc


def _mm_exact_lhs(e_bf16, a, parts):
    acc = None
    rem = a
    for _ in range(parts):
        piece = rem.astype(BF16)
        term = _dot(e_bf16, piece)
        acc = term if acc is None else acc + term
        rem = rem - piece.astype(F32)
    return acc


def _rwkv_prep_kernel(x_ref, nw_ref, mu_ref, w0_ref, w1_ref, w2_ref, a0_ref, a1_ref, a2_ref,
                      xs_ref, lw_ref, ic_ref, carry_ref):
    t = pl.program_id(1)
    x = x_ref[...]
    tm = x.shape[0]
    h = x * lax.rsqrt(jnp.mean(x * x, axis=-1, keepdims=True) + NORM_EPS) * nw_ref[...]

    @pl.when(t == 0)
    def _():
        carry_ref[...] = jnp.zeros_like(carry_ref)

    prev_last = carry_ref[0:1, :]
    rolled = pltpu.roll(h, 1, axis=0)
    row = lax.broadcasted_iota(jnp.int32, h.shape, 0)
    h_prev = jnp.where(row == 0, prev_last, rolled)
    carry_ref[0:1, :] = h[tm - 1:tm, :]
    d = h_prev - h
    mu = mu_ref[...]
    xs_ref[0] = (h + d * mu[0:1]).astype(BF16)
    xs_ref[1] = (h + d * mu[2:3]).astype(BF16)
    xs_ref[2] = (h + d * mu[3:4]).astype(BF16)
    xs_ref[3] = (h + d * mu[5:6]).astype(BF16)
    xw = (h + d * mu[1:2]).astype(BF16)
    xa = (h + d * mu[4:5]).astype(BF16)
    zw = w0_ref[...] + _dot(jnp.tanh(_dot(xw, w1_ref[...])).astype(BF16), w2_ref[...])
    lw_ref[...] = (-math.exp(-0.5)) / (1.0 + jnp.exp(-zw))
    za = a0_ref[...] + _dot(_dot(xa, a1_ref[...]).astype(BF16), a2_ref[...])
    ic_ref[...] = 1.0 / (1.0 + jnp.exp(-za))


def _rwkv_prep(x2, nw, mu, w0, w1p, w2p, a0, a1p, a2p, *, batch, seq, tm):
    m, d = x2.shape
    nt = seq // tm
    row_spec = pl.BlockSpec((tm, d), lambda b, t: (b * nt + t, 0))
    vec_spec = pl.BlockSpec((1, d), lambda b, t: (0, 0))

    def full(a):
        return pl.BlockSpec(a.shape, lambda b, t: (0,) * a.ndim)

    return pl.pallas_call(
        _rwkv_prep_kernel,
        grid=(batch, nt),
        in_specs=[row_spec, vec_spec, full(mu), vec_spec, full(w1p), full(w2p), vec_spec, full(a1p), full(a2p)],
        out_specs=[pl.BlockSpec((4, tm, d), lambda b, t: (0, b * nt + t, 0)), row_spec, row_spec],
        out_shape=[jax.ShapeDtypeStruct((4, m, d), BF16),
                   jax.ShapeDtypeStruct((m, d), F32),
                   jax.ShapeDtypeStruct((m, d), F32)],
        scratch_shapes=[pltpu.VMEM((8, d), F32)],
        compiler_params=_cparams(("parallel", "arbitrary")),
        name="rwkv_prep",
    )(x2, nw, mu, w0, w1p, w2p, a0, a1p, a2p)


def _mm_kernel(a_ref, b_ref, o_ref):
    o_ref[...] = _dot(a_ref[...], b_ref[...]).astype(o_ref.dtype)


def _proj4(xs, w, *, tm, tn, out_dtype):
    _, m, k = xs.shape
    n = w.shape[1]
    per_path = (n // 4) // tn
    return pl.pallas_call(
        _mm_kernel,
        grid=(m // tm, n // tn),
        in_specs=[pl.BlockSpec((None, tm, k), lambda i, j: (j // per_path, i, 0)),
                  pl.BlockSpec((k, tn), lambda i, j: (0, j))],
        out_specs=pl.BlockSpec((tm, tn), lambda i, j: (i, j)),
        out_shape=jax.ShapeDtypeStruct((m, n), out_dtype),
        compiler_params=_cparams(("parallel", "arbitrary")),
        name="proj4",
    )(xs, w)


def _matmul(a, w, *, tm, tn, out_dtype):
    m, k = a.shape
    n = w.shape[1]
    return pl.pallas_call(
        _mm_kernel,
        grid=(m // tm, n // tn),
        in_specs=[pl.BlockSpec((tm, k), lambda i, j: (i, 0)),
                  pl.BlockSpec((k, tn), lambda i, j: (0, j))],
        out_specs=pl.BlockSpec((tm, tn), lambda i, j: (i, j)),
        out_shape=jax.ShapeDtypeStruct((m, n), out_dtype),
        compiler_params=_cparams(("parallel", "arbitrary")),
        name="matmul",
    )(a, w)


def _mm_res_kernel(a_ref, b_ref, r_ref, o_ref):
    o_ref[...] = r_ref[...] + _dot(a_ref[...], b_ref[...])


def _matmul_residual(a, w, res, *, tm, tn):
    m, k = a.shape
    n = w.shape[1]
    return pl.pallas_call(
        _mm_res_kernel,
        grid=(m // tm, n // tn),
        in_specs=[pl.BlockSpec((tm, k), lambda i, j: (i, 0)),
                  pl.BlockSpec((k, tn), lambda i, j: (0, j)),
                  pl.BlockSpec((tm, tn), lambda i, j: (i, j))],
        out_specs=pl.BlockSpec((tm, tn), lambda i, j: (i, j)),
        out_shape=jax.ShapeDtypeStruct((m, n), F32),
        compiler_params=_cparams(("parallel", "arbitrary")),
        name="matmul_residual",
    )(a, w, res)


def _mm_res_norm_kernel(a_ref, b_ref, r_ref, nw_ref, o_ref):
    y = r_ref[...] + _dot(a_ref[...], b_ref[...])
    o_ref[...] = y * lax.rsqrt(jnp.mean(y * y, axis=-1, keepdims=True) + NORM_EPS) * nw_ref[...]


def _matmul_residual_norm(a, w, res, nw, *, tm):
    m, k = a.shape
    n = w.shape[1]
    return pl.pallas_call(
        _mm_res_norm_kernel,
        grid=(m // tm,),
        in_specs=[pl.BlockSpec((tm, k), lambda i: (i, 0)),
                  pl.BlockSpec((k, n), lambda i: (0, 0)),
                  pl.BlockSpec((tm, n), lambda i: (i, 0)),
                  pl.BlockSpec((1, n), lambda i: (0, 0))],
        out_specs=pl.BlockSpec((tm, n), lambda i: (i, 0)),
        out_shape=jax.ShapeDtypeStruct((m, n), F32),
        compiler_params=_cparams(("parallel",)),
        name="matmul_residual_norm",
    )(a, w, res, nw)


def _norm_kernel(x_ref, nw_ref, o_ref):
    x = x_ref[...]
    o_ref[...] = (x * lax.rsqrt(jnp.mean(x * x, axis=-1, keepdims=True) + NORM_EPS) * nw_ref[...]).astype(o_ref.dtype)


def _rms_norm(x2, nw, *, tm, out_dtype):
    m, d = x2.shape
    return pl.pallas_call(
        _norm_kernel,
        grid=(m // tm,),
        in_specs=[pl.BlockSpec((tm, d), lambda i: (i, 0)), pl.BlockSpec((1, d), lambda i: (0, 0))],
        out_specs=pl.BlockSpec((tm, d), lambda i: (i, 0)),
        out_shape=jax.ShapeDtypeStruct((m, d), out_dtype),
        compiler_params=_cparams(("parallel",)),
        name="rms_norm",
    )(x2, nw)


P_CHUNK = 3
P_STATE = 3


def _wkv_kernel(r_ref, k_ref, v_ref, g_ref, lw_ref, ic_ref, kk_ref, ka_ref, rk_ref, lnw_ref, lnb_ref,
                o_ref,
                pq_sc, rs_sc, arb_sc, arkv_sc, bkt_sc, vs_sc, wc_sc, y_sc, h_sc, *, post_rows):
    tb = r_ref.shape[0]
    nc = tb // CHUNK
    two = 2 * CHUNK

    lane1 = lax.broadcasted_iota(jnp.int32, (1, LANES), 1)
    m0 = (lane1 < HEAD).astype(F32)
    m1 = 1.0 - m0
    row = lax.broadcasted_iota(jnp.int32, (two, two), 0)
    col = lax.broadcasted_iota(jnp.int32, (two, two), 1)
    strict = row > col
    incl = row >= col
    eye = (row == col).astype(F32)
    same_head = ((row < HEAD) == (col < HEAD))
    ones_bd = same_head.astype(BF16)
    tril_c = incl[:CHUNK, :CHUNK].astype(BF16)

    kk_p = kk_ref[...]
    ka_p = ka_ref[...]

    def segsum(z):
        return _mm_exact_rhs(z, ones_bd, 2)

    def stack(z):
        return jnp.concatenate([z * m0, z * m1], axis=0)

    @pl.when(pl.program_id(2) == 0)
    def _():
        h_sc[...] = jnp.zeros_like(h_sc)

    def phase_a(c, carry):
        rows = pl.ds(pl.multiple_of(c * CHUNK, CHUNK), CHUNK)
        r = r_ref[rows, :]
        k = k_ref[rows, :]
        v = v_ref[rows, :]
        lw = lw_ref[rows, :]
        ic = ic_ref[rows, :]
        kkv = k * kk_p
        kkn = kkv / jnp.maximum(jnp.sqrt(segsum(kkv * kkv)), 1e-12)
        kh = k * (1.0 + (ic - 1.0) * ka_p)
        lin = _mm_exact_lhs(tril_c, lw, 3)
        e_in = jnp.exp(lin)
        e_ex = jnp.exp(lin - lw)
        e_neg = jnp.exp(-lin)
        a_s = stack(-kkn * e_ex)
        r_s = stack(r * e_in)
        b_s = stack(kkn * ic * e_neg)
        k_s = stack(kh * e_neg)
        v_s = stack(v)
        a_all = _mm(jnp.concatenate([a_s, r_s], axis=0), jnp.concatenate([b_s, k_s], axis=0), P_CHUNK, nt=True)
        a_ab = jnp.where(strict, a_all[:two, :two], 0.0)
        a_ak = jnp.where(strict, a_all[:two, two:], 0.0)
        a_rb = jnp.where(incl, a_all[two:, :two], 0.0)
        a_rk = jnp.where(incl, a_all[two:, two:], 0.0)
        t_inv = eye + a_ab
        pw = a_ab
        for _ in range(5):
            pw = _mm(pw, pw, P_CHUNK)
            t_inv = t_inv + _mm(t_inv, pw, P_CHUNK)
        av = _mm(jnp.concatenate([a_ak, a_rk], axis=0), v_s, P_CHUNK)
        pq_sc[c] = _mm(t_inv, jnp.concatenate([a_s, av[:two]], axis=1), P_CHUNK)
        rs_sc[c] = r_s
        arb_sc[c] = a_rb
        arkv_sc[c] = av[two:]
        vs_sc[c] = v_s
        bkt_sc[c] = jnp.concatenate([b_s.T, k_s.T], axis=1)
        wcol = jnp.sum(eye * e_in[CHUNK - 1:CHUNK, :], axis=1, keepdims=True)
        wc_sc[c] = jnp.broadcast_to(wcol, (two, LANES))
        return carry

    lax.fori_loop(0, nc, phase_a, 0)

    def phase_b(c, h):
        pq = pq_sc[c]
        u_s = _mm(pq[:, :LANES], h, P_STATE) + pq[:, LANES:]
        y_s = _mm(rs_sc[c], h, P_STATE) + _mm(arb_sc[c], u_s, P_STATE) + arkv_sc[c]
        uv = jnp.concatenate([u_s, vs_sc[c]], axis=0)
        h = wc_sc[c] * (h + _mm(bkt_sc[c], uv, P_STATE))
        rows = pl.ds(pl.multiple_of(c * CHUNK, CHUNK), CHUNK)
        y_sc[rows, :] = y_s[:CHUNK] + y_s[CHUNK:]
        return h

    h_sc[...] = lax.fori_loop(0, nc, phase_b, h_sc[...])

    rk_p = rk_ref[...]
    lnw_p = lnw_ref[...]
    lnb_p = lnb_ref[...]
    inv_n = 1.0 / HEAD

    def phase_c(i, carry):
        rows = pl.ds(pl.multiple_of(i * post_rows, post_rows), post_rows)
        y = y_sc[rows, :]
        mean = segsum(y) * inv_n
        yc = y - mean
        var = segsum(yc * yc) * inv_n
        gn = yc * lax.rsqrt(var + GN_EPS) * lnw_p + lnb_p
        r = r_ref[rows, :]
        kh = k_ref[rows, :] * (1.0 + (ic_ref[rows, :] - 1.0) * ka_p)
        bonus = segsum(r * kh * rk_p) * v_ref[rows, :]
        g = g_ref[rows, :]
        o_ref[rows, :] = ((gn + bonus) * (g / (1.0 + jnp.exp(-g)))).astype(o_ref.dtype)
        return carry

    lax.fori_loop(0, tb // post_rows, phase_c, 0)


def _wkv(proj, lw, ic, kk, ka, rk, lnw, lnb, *, batch, seq, tb):
    m, n4 = proj.shape
    w = n4 // 4
    nhp = w // LANES
    ntb = seq // tb
    nc = tb // CHUNK
    two = 2 * CHUNK

    def col_spec(p):
        return pl.BlockSpec((tb, LANES), lambda b, hp, t: (b * ntb + t, p * nhp + hp))

    par_spec = pl.BlockSpec((1, LANES), lambda b, hp, t: (0, hp))
    act_spec = pl.BlockSpec((tb, LANES), lambda b, hp, t: (b * ntb + t, hp))
    return pl.pallas_call(
        functools.partial(_wkv_kernel, post_rows=min(tb, 256)),
        grid=(batch, nhp, ntb),
        in_specs=[col_spec(0), col_spec(1), col_spec(2), col_spec(3), act_spec, act_spec,
                  par_spec, par_spec, par_spec, par_spec, par_spec],
        out_specs=act_spec,
        out_shape=jax.ShapeDtypeStruct((m, w), BF16),
        scratch_shapes=[pltpu.VMEM((nc, two, 2 * LANES), F32),
                        pltpu.VMEM((nc, two, LANES), F32),
                        pltpu.VMEM((nc, two, two), F32),
                        pltpu.VMEM((nc, two, LANES), F32),
                        pltpu.VMEM((nc, LANES, 2 * two), F32),
                        pltpu.VMEM((nc, two, LANES), F32),
                        pltpu.VMEM((nc, two, LANES), F32),
                        pltpu.VMEM((tb, LANES), F32),
                        pltpu.VMEM((two, LANES), F32)],
        compiler_params=_cparams(("parallel", "parallel", "arbitrary")),
        name="wkv7",
    )(proj, proj, proj, proj, lw, ic, kk, ka, rk, lnw, lnb)


def _attn_kernel(q_ref, k_ref, v_ref, g_ref, lam_ref, sw_ref, o_ref, *, n_heads, lambda_init):
    tq = q_ref.shape[0]
    h = pl.program_id(1)
    qi = pl.program_id(2)
    lane = lax.broadcasted_iota(jnp.int32, (1, LANES), 1)
    qf = q_ref[...].astype(F32) * (1.0 / math.sqrt(HEAD))
    q1 = jnp.where(lane < HEAD, qf, 0.0).astype(BF16)
    q2 = jnp.where(lane >= HEAD, qf, 0.0).astype(BF16)
    hf = (h + 1).astype(F32)
    slope = jnp.exp2(jnp.full((1, tq), -8.0 / n_heads, F32) * hf)
    colf = lax.broadcasted_iota(jnp.int32, (1, tq), 1).astype(F32)

    def update(s, m, l, acc, vs):
        mn = jnp.maximum(m, jnp.max(s, axis=1, keepdims=True))
        alpha = jnp.exp(m - mn)
        p = jnp.exp(s - mn)
        l = alpha * l + jnp.sum(p, axis=1, keepdims=True)
        acc = alpha * acc + _dot(p.astype(BF16), vs)
        return mn, l, acc

    def block(j, carry, masked):
        m1, l1, a1, m2, l2, a2 = carry
        rows = pl.ds(pl.multiple_of(j * tq, tq), tq)
        ks = k_ref[rows, :]
        vs = v_ref[rows, :]
        bias = slope * (colf + ((j - qi) * tq).astype(F32))
        s1 = _dot_nt(q1, ks) + bias
        s2 = _dot_nt(q2, ks) + bias
        if masked:
            ri = lax.broadcasted_iota(jnp.int32, (tq, tq), 0)
            ci = lax.broadcasted_iota(jnp.int32, (tq, tq), 1)
            keep = ci <= ri
            s1 = jnp.where(keep, s1, NEG_BIG)
            s2 = jnp.where(keep, s2, NEG_BIG)
        m1, l1, a1 = update(s1, m1, l1, a1, vs)
        m2, l2, a2 = update(s2, m2, l2, a2, vs)
        return m1, l1, a1, m2, l2, a2

    neg = jnp.full((tq, 1), NEG_BIG, F32)
    zl = jnp.zeros((tq, 1), F32)
    za = jnp.zeros((tq, LANES), F32)
    carry = lax.fori_loop(0, qi, lambda j, c: block(j, c, False), (neg, zl, za, neg, zl, za))
    m1, l1, a1, m2, l2, a2 = block(qi, carry, True)

    lam = lam_ref[...]
    lam_full = (jnp.exp(jnp.sum(lam[0:1] * lam[1:2], axis=1, keepdims=True))
                - jnp.exp(jnp.sum(lam[2:3] * lam[3:4], axis=1, keepdims=True)) + lambda_init)
    o = a1 / l1 - lam_full * (a2 / l2)
    o = o * lax.rsqrt(jnp.mean(o * o, axis=-1, keepdims=True) + SUBLN_EPS) * sw_ref[...] * (1.0 - lambda_init)
    g = g_ref[...].astype(F32)
    o_ref[...] = (o * (g / (1.0 + jnp.exp(-g)))).astype(o_ref.dtype)


def _diff_attn(qkvg, lam, sw, *, batch, seq, tq, lambda_init):
    m, n4 = qkvg.shape
    w = n4 // 4
    nh = w // LANES
    nq = seq // tq
    q_spec = pl.BlockSpec((tq, LANES), lambda b, h, i: (b * nq + i, h))
    k_spec = pl.BlockSpec((seq, LANES), lambda b, h, i: (b, nh + h))
    v_spec = pl.BlockSpec((seq, LANES), lambda b, h, i: (b, 2 * nh + h))
    g_spec = pl.BlockSpec((tq, LANES), lambda b, h, i: (b * nq + i, 3 * nh + h))
    return pl.pallas_call(
        functools.partial(_attn_kernel, n_heads=nh, lambda_init=lambda_init),
        grid=(batch, nh, nq),
        in_specs=[q_spec, k_spec, v_spec, g_spec,
                  pl.BlockSpec(lam.shape, lambda b, h, i: (0, 0)),
                  pl.BlockSpec((1, LANES), lambda b, h, i: (0, 0))],
        out_specs=pl.BlockSpec((tq, LANES), lambda b, h, i: (b * nq + i, h)),
        out_shape=jax.ShapeDtypeStruct((m, w), BF16),
        compiler_params=_cparams(("parallel", "parallel", "arbitrary")),
        name="diff_attn",
    )(qkvg, qkvg, qkvg, qkvg, lam, sw)


def kernel(x, norm_w, rwkv_mu, rwkv_w_in, rwkv_w0, rwkv_w1, rwkv_w2, rwkv_a0, rwkv_a1, rwkv_a2, rwkv_k_k, rwkv_k_a, rwkv_r_k, rwkv_ln_w, rwkv_ln_b, rwkv_w_out, diff_w_in, diff_lambda, diff_subln_w, diff_w_out, final_norm_w):
    batch, seq, d = x.shape
    m = batch * seq
    assert d % LANES == 0 and seq % CHUNK == 0
    assert norm_w.shape[0] == 2 and rwkv_mu.shape[0] == 1 and diff_w_in.shape[0] == 1
    tm_big = min(1024, m)
    tn_big = min(1024, d)
    tm_prep = min(256, seq)
    tm_last = min(512, m)
    tb = min(1024, seq)
    tq = min(256, seq)

    x2 = x.reshape(m, d)
    row = lambda a: a.reshape(1, -1)
    lora = rwkv_w1.shape[-1]
    pad_c = lambda a: jnp.pad(a, ((0, 0), (0, LORA_PAD - lora))).astype(BF16)
    pad_r = lambda a: jnp.pad(a, ((0, LORA_PAD - lora), (0, 0))).astype(BF16)

    xs, lw, ic = _rwkv_prep(x2, row(norm_w[0]), rwkv_mu[0], row(rwkv_w0[0]), pad_c(rwkv_w1[0]), pad_r(rwkv_w2[0]),
                            row(rwkv_a0[0]), pad_c(rwkv_a1[0]), pad_r(rwkv_a2[0]),
                            batch=batch, seq=seq, tm=tm_prep)
    proj = _proj4(xs, rwkv_w_in[0].astype(BF16), tm=tm_big, tn=tn_big, out_dtype=F32)
    yg = _wkv(proj, lw, ic, row(rwkv_k_k[0]), row(rwkv_k_a[0]), row(rwkv_r_k[0]), row(rwkv_ln_w[0]),
              row(rwkv_ln_b[0]), batch=batch, seq=seq, tb=tb)
    x1 = _matmul_residual(yg, rwkv_w_out[0].astype(BF16), x2, tm=tm_big, tn=tn_big)

    lambda_init = 0.8 - 0.6 * math.exp(-0.3 * 1)
    h1 = _rms_norm(x1, row(norm_w[1]), tm=tm_last, out_dtype=BF16)
    qkvg = _matmul(h1, diff_w_in[0].astype(BF16), tm=tm_big, tn=tn_big, out_dtype=BF16)
    att = _diff_attn(qkvg, diff_lambda[0], row(diff_subln_w[0]), batch=batch, seq=seq, tq=tq,
                     lambda_init=lambda_init)
    out = _matmul_residual_norm(att, diff_w_out[0].astype(BF16), x1, row(final_norm_w), tm=tm_last)
    return out.reshape(batch, seq, d)
```

```python
import functools
import math

import jax
import jax.numpy as jnp
from jax import lax
from jax.experimental import pallas as pl
from jax.experimental.pallas import tpu as pltpu

F32 = jnp.float32
BF16 = jnp.bfloat16

NORM_EPS = 1e-6
GN_EPS = 64e-5
SUBLN_EPS = 1e-5
HEAD = 64
LANES = 128
CHUNK = 64
LORA_PAD = 128
VMEM_LIMIT = 48 * 1024 * 1024
NEG_BIG = -1e30


def _cparams(sem):
    return pltpu.CompilerParams(dimension_semantics=sem, vmem_limit_bytes=VMEM_LIMIT)


def _dot(a, b):
    return jnp.dot(a, b, preferred_element_type=F32)


def _dot_nt(a, b):
    return lax.dot_general(a, b, (((1,), (1,)), ((), ())), preferred_element_type=F32)


def _split2(x):
    hi = x.astype(BF16)
    lo = (x - hi.astype(F32)).astype(BF16)
    return hi, lo


def _mm(a, b, passes, nt=False):
    d = _dot_nt if nt else _dot
    if passes == 1:
        return d(a.astype(BF16), b.astype(BF16))
    ah, al = _split2(a)
    bh, bl = _split2(b)
    return d(ah, bh) + (d(al, bh) + d(ah, bl))


def _mm_exact_rhs(a, e_bf16, parts):
    acc = None
    rem = a
    for _ in range(parts):
        piece = rem.astype(BF16)
        term = _dot(piece, e_bf16)
        acc = term if acc is None else acc + term
        rem = rem - piece.astype(F32)
    return acc


def _mm_exact_lhs(e_bf16, a, parts):
    acc = None
    rem = a
    for _ in range(parts):
        piece = rem.astype(BF16)
        term = _dot(e_bf16, piece)
        acc = term if acc is None else acc + term
        rem = rem - piece.astype(F32)
    return acc


def _rwkv_prep_kernel(x_ref, nw_ref, mu_ref, w0_ref, w1_ref, w2_ref, a0_ref, a1_ref, a2_ref,
                      xs_ref, lw_ref, ic_ref, carry_ref):
    t = pl.program_id(1)
    x = x_ref[...]
    tm = x.shape[0]
    h = x * lax.rsqrt(jnp.mean(x * x, axis=-1, keepdims=True) + NORM_EPS) * nw_ref[...]

    @pl.when(t == 0)
    def _():
        carry_ref[...] = jnp.zeros_like(carry_ref)

    prev_last = carry_ref[0:1, :]
    rolled = pltpu.roll(h, 1, axis=0)
    row = lax.broadcasted_iota(jnp.int32, h.shape, 0)
    h_prev = jnp.where(row == 0, prev_last, rolled)
    carry_ref[0:1, :] = h[tm - 1:tm, :]
    d = h_prev - h
    mu = mu_ref[...]
    xs_ref[0] = (h + d * mu[0:1]).astype(BF16)
    xs_ref[1] = (h + d * mu[2:3]).astype(BF16)
    xs_ref[2] = (h + d * mu[3:4]).astype(BF16)
    xs_ref[3] = (h + d * mu[5:6]).astype(BF16)
    xw = (h + d * mu[1:2]).astype(BF16)
    xa = (h + d * mu[4:5]).astype(BF16)
    zw = w0_ref[...] + _dot(jnp.tanh(_dot(xw, w1_ref[...])).astype(BF16), w2_ref[...])
    lw_ref[...] = (-math.exp(-0.5)) / (1.0 + jnp.exp(-zw))
    za = a0_ref[...] + _dot(_dot(xa, a1_ref[...]).astype(BF16), a2_ref[...])
    ic_ref[...] = 1.0 / (1.0 + jnp.exp(-za))


def _rwkv_prep(x2, nw, mu, w0, w1p, w2p, a0, a1p, a2p, *, batch, seq, tm):
    m, d = x2.shape
    nt = seq // tm
    row_spec = pl.BlockSpec((tm, d), lambda b, t: (b * nt + t, 0))
    vec_spec = pl.BlockSpec((1, d), lambda b, t: (0, 0))

    def full(a):
        return pl.BlockSpec(a.shape, lambda b, t: (0,) * a.ndim)

    return pl.pallas_call(
        _rwkv_prep_kernel,
        grid=(batch, nt),
        in_specs=[row_spec, vec_spec, full(mu), vec_spec, full(w1p), full(w2p), vec_spec, full(a1p), full(a2p)],
        out_specs=[pl.BlockSpec((4, tm, d), lambda b, t: (0, b * nt + t, 0)), row_spec, row_spec],
        out_shape=[jax.ShapeDtypeStruct((4, m, d), BF16),
                   jax.ShapeDtypeStruct((m, d), F32),
                   jax.ShapeDtypeStruct((m, d), F32)],
        scratch_shapes=[pltpu.VMEM((8, d), F32)],
        compiler_params=_cparams(("parallel", "arbitrary")),
        name="rwkv_prep",
    )(x2, nw, mu, w0, w1p, w2p, a0, a1p, a2p)


def _mm_kernel(a_ref, b_ref, o_ref):
    o_ref[...] = _dot(a_ref[...], b_ref[...]).astype(o_ref.dtype)


def _proj4(xs, w, *, tm, tn, out_dtype):
    _, m, k = xs.shape
    n = w.shape[1]
    per_path = (n // 4) // tn
    return pl.pallas_call(
        _mm_kernel,
        grid=(m // tm, n // tn),
        in_specs=[pl.BlockSpec((None, tm, k), lambda i, j: (j // per_path, i, 0)),
                  pl.BlockSpec((k, tn), lambda i, j: (0, j))],
        out_specs=pl.BlockSpec((tm, tn), lambda i, j: (i, j)),
        out_shape=jax.ShapeDtypeStruct((m, n), out_dtype),
        compiler_params=_cparams(("parallel", "arbitrary")),
        name="proj4",
    )(xs, w)


def _matmul(a, w, *, tm, tn, out_dtype):
    m, k = a.shape
    n = w.shape[1]
    return pl.pallas_call(
        _mm_kernel,
        grid=(m // tm, n // tn),
        in_specs=[pl.BlockSpec((tm, k), lambda i, j: (i, 0)),
                  pl.BlockSpec((k, tn), lambda i, j: (0, j))],
        out_specs=pl.BlockSpec((tm, tn), lambda i, j: (i, j)),
        out_shape=jax.ShapeDtypeStruct((m, n), out_dtype),
        compiler_params=_cparams(("parallel", "arbitrary")),
        name="matmul",
    )(a, w)


def _mm_res_kernel(a_ref, b_ref, r_ref, o_ref):
    o_ref[...] = r_ref[...] + _dot(a_ref[...], b_ref[...])


def _matmul_residual(a, w, res, *, tm, tn):
    m, k = a.shape
    n = w.shape[1]
    return pl.pallas_call(
        _mm_res_kernel,
        grid=(m // tm, n // tn),
        in_specs=[pl.BlockSpec((tm, k), lambda i, j: (i, 0)),
                  pl.BlockSpec((k, tn), lambda i, j: (0, j)),
                  pl.BlockSpec((tm, tn), lambda i, j: (i, j))],
        out_specs=pl.BlockSpec((tm, tn), lambda i, j: (i, j)),
        out_shape=jax.ShapeDtypeStruct((m, n), F32),
        compiler_params=_cparams(("parallel", "arbitrary")),
        name="matmul_residual",
    )(a, w, res)


def _mm_res_norm_kernel(a_ref, b_ref, r_ref, nw_ref, o_ref):
    y = r_ref[...] + _dot(a_ref[...], b_ref[...])
    o_ref[...] = y * lax.rsqrt(jnp.mean(y * y, axis=-1, keepdims=True) + NORM_EPS) * nw_ref[...]


def _matmul_residual_norm(a, w, res, nw, *, tm):
    m, k = a.shape
    n = w.shape[1]
    return pl.pallas_call(
        _mm_res_norm_kernel,
        grid=(m // tm,),
        in_specs=[pl.BlockSpec((tm, k), lambda i: (i, 0)),
                  pl.BlockSpec((k, n), lambda i: (0, 0)),
                  pl.BlockSpec((tm, n), lambda i: (i, 0)),
                  pl.BlockSpec((1, n), lambda i: (0, 0))],
        out_specs=pl.BlockSpec((tm, n), lambda i: (i, 0)),
        out_shape=jax.ShapeDtypeStruct((m, n), F32),
        compiler_params=_cparams(("parallel",)),
        name="matmul_residual_norm",
    )(a, w, res, nw)


def _norm_kernel(x_ref, nw_ref, o_ref):
    x = x_ref[...]
    o_ref[...] = (x * lax.rsqrt(jnp.mean(x * x, axis=-1, keepdims=True) + NORM_EPS) * nw_ref[...]).astype(o_ref.dtype)


def _rms_norm(x2, nw, *, tm, out_dtype):
    m, d = x2.shape
    return pl.pallas_call(
        _norm_kernel,
        grid=(m // tm,),
        in_specs=[pl.BlockSpec((tm, d), lambda i: (i, 0)), pl.BlockSpec((1, d), lambda i: (0, 0))],
        out_specs=pl.BlockSpec((tm, d), lambda i: (i, 0)),
        out_shape=jax.ShapeDtypeStruct((m, d), out_dtype),
        compiler_params=_cparams(("parallel",)),
        name="rms_norm",
    )(x2, nw)


WKV_GROUP = 4


def _wkv_kernel(r_ref, k_ref, v_ref, g_ref, lw_ref, ic_ref, kk_ref, ka_ref, rk_ref, lnw_ref, lnb_ref,
                o_ref, rm_sc, y0_sc, g_sc, wc_sc, y_sc, *, group, post_rows):
    seq = r_ref.shape[0]
    ngroups = seq // (CHUNK * group)
    two = 2 * CHUNK

    lane1 = lax.broadcasted_iota(jnp.int32, (1, LANES), 1)
    m0 = (lane1 < HEAD).astype(F32)
    m1 = 1.0 - m0
    row = lax.broadcasted_iota(jnp.int32, (two, two), 0)
    col = lax.broadcasted_iota(jnp.int32, (two, two), 1)
    strict = row > col
    incl = row >= col
    eye = (row == col).astype(F32)
    ones_bd = ((row < HEAD) == (col < HEAD)).astype(BF16)
    tril_c = incl[:CHUNK, :CHUNK].astype(BF16)
    zeros_b = jnp.zeros((two, LANES), BF16)

    kk_p = kk_ref[...]
    ka_p = ka_ref[...]

    def segsum(z):
        return _mm_exact_rhs(z, ones_bd, 2)

    def stack(z):
        return jnp.concatenate([z * m0, z * m1], axis=0)

    def chunk_rows(c):
        return pl.ds(pl.multiple_of(c * CHUNK, CHUNK), CHUNK)

    def phase_a(gi):
        us = range(group)
        rows = [chunk_rows(gi * group + u) for u in us]
        k = [k_ref[rw, :] for rw in rows]
        lw = [lw_ref[rw, :] for rw in rows]
        kkv = [k[u] * kk_p for u in us]
        n2 = [segsum(kkv[u] * kkv[u]) for u in us]
        lin = [_mm_exact_lhs(tril_c, lw[u], 3) for u in us]
        yield
        a_s, r_s, v_s, bkt, wcb, a_all = [], [], [], [], [], []
        for u in us:
            ic = ic_ref[rows[u], :]
            kkn = kkv[u] / jnp.maximum(jnp.sqrt(n2[u]), 1e-12)
            kh = k[u] * (1.0 + (ic - 1.0) * ka_p)
            e_in = jnp.exp(lin[u])
            e_ex = jnp.exp(lin[u] - lw[u])
            e_neg = jnp.exp(-lin[u])
            b_f = stack(kkn * ic * e_neg)
            k_f = stack(kh * e_neg)
            a_s.append(stack(-kkn * e_ex).astype(BF16))
            r_s.append(stack(r_ref[rows[u], :] * e_in).astype(BF16))
            v_s.append(stack(v_ref[rows[u], :]).astype(BF16))
            a_all.append(_dot_nt(jnp.concatenate([a_s[u], r_s[u]], axis=0),
                                 jnp.concatenate([b_f, k_f], axis=0).astype(BF16)))
            bkt.append(jnp.concatenate([b_f.T, k_f.T], axis=1).astype(BF16))
            wcol = jnp.sum(eye * e_in[CHUNK - 1:CHUNK, :], axis=1, keepdims=True)
            wcb.append(jnp.broadcast_to(wcol, (two, LANES)))
        yield
        t_inv, pwb, a_rb, av = [], [], [], []
        for u in us:
            a_ab = jnp.where(strict, a_all[u][:two, :two], 0.0)
            a_ak = jnp.where(strict, a_all[u][:two, two:], 0.0).astype(BF16)
            a_rk = jnp.where(incl, a_all[u][two:, two:], 0.0).astype(BF16)
            a_rb.append(jnp.where(incl, a_all[u][two:, :two], 0.0).astype(BF16))
            av.append(_dot(jnp.concatenate([a_ak, a_rk], axis=0), v_s[u]))
            t_inv.append(eye + a_ab)
            pwb.append(a_ab.astype(BF16))
        yield
        pwb = [_dot(pwb[u], pwb[u]).astype(BF16) for u in us]
        yield
        for _ in range(4):
            t_inv = [t_inv[u] + _dot(t_inv[u].astype(BF16), pwb[u]) for u in us]
            pwb = [_dot(pwb[u], pwb[u]).astype(BF16) for u in us]
            yield
        t_inv = [t_inv[u] + _dot(t_inv[u].astype(BF16), pwb[u]) for u in us]
        yield
        pqb = [_dot(t_inv[u].astype(BF16),
                    jnp.concatenate([a_s[u], av[u][:two].astype(BF16)], axis=1)).astype(BF16) for u in us]
        yield
        rq = [_dot(a_rb[u], pqb[u]) for u in us]
        mg = [_dot(bkt[u], jnp.concatenate([pqb[u], jnp.concatenate([zeros_b, v_s[u]], axis=1)], axis=0)) for u in us]
        yield
        for u in us:
            y0 = rq[u][:, LANES:] + av[u][two:]
            rm_sc[u] = jnp.concatenate([r_s[u].astype(F32) + rq[u][:, :LANES], wcb[u] * mg[u][:, :LANES]],
                                       axis=0).astype(BF16)
            g_sc[u] = wcb[u] * mg[u][:, LANES:]
            y0_sc[u] = y0[:CHUNK] + y0[CHUNK:]
            wc_sc[u] = wcb[u]

    def state_step(gi, u, h):
        res = _dot(rm_sc[u], h.astype(BF16))
        y_sc[chunk_rows(gi * group + u), :] = res[:CHUNK] + res[CHUNK:two] + y0_sc[u]
        return wc_sc[u] * h + res[two:] + g_sc[u]

    def run(a_gi, b_gi, h):
        done = 0
        if a_gi is None:
            for u in range(group):
                h = state_step(b_gi, u, h)
            return h
        for i, _ in enumerate(phase_a(a_gi)):
            if b_gi is not None and done < group and i % 2 == 1:
                h = state_step(b_gi, done, h)
                done += 1
        assert b_gi is None or done == group
        return h

    run(0, None, None)
    h = lax.fori_loop(0, ngroups - 1, lambda gi, h: run(gi + 1, gi, h), jnp.zeros((two, LANES), F32))
    run(None, ngroups - 1, h)

    rk_p = rk_ref[...]
    lnw_p = lnw_ref[...]
    lnb_p = lnb_ref[...]
    inv_n = 1.0 / HEAD

    def phase_c(i, carry):
        rows = pl.ds(pl.multiple_of(i * post_rows, post_rows), post_rows)
        y = y_sc[rows, :]
        mean = segsum(y) * inv_n
        yc = y - mean
        var = segsum(yc * yc) * inv_n
        gn = yc * lax.rsqrt(var + GN_EPS) * lnw_p + lnb_p
        r = r_ref[rows, :]
        kh = k_ref[rows, :] * (1.0 + (ic_ref[rows, :] - 1.0) * ka_p)
        bonus = segsum(r * kh * rk_p) * v_ref[rows, :]
        g = g_ref[rows, :]
        o_ref[rows, :] = ((gn + bonus) * (g / (1.0 + jnp.exp(-g)))).astype(o_ref.dtype)
        return carry

    lax.fori_loop(0, seq // post_rows, phase_c, 0)


def _wkv(proj, lw, ic, kk, ka, rk, lnw, lnb, *, batch, seq):
    m, n4 = proj.shape
    w = n4 // 4
    nhp = w // LANES
    two = 2 * CHUNK
    group = WKV_GROUP
    assert seq % (CHUNK * group) == 0

    def col_spec(p):
        return pl.BlockSpec((seq, LANES), lambda b, hp: (b, p * nhp + hp))

    par_spec = pl.BlockSpec((1, LANES), lambda b, hp: (0, hp))
    act_spec = pl.BlockSpec((seq, LANES), lambda b, hp: (b, hp))
    return pl.pallas_call(
        functools.partial(_wkv_kernel, group=group, post_rows=min(seq, 256)),
        grid=(batch, nhp),
        in_specs=[col_spec(0), col_spec(1), col_spec(2), col_spec(3), act_spec, act_spec,
                  par_spec, par_spec, par_spec, par_spec, par_spec],
        out_specs=act_spec,
        out_shape=jax.ShapeDtypeStruct((m, w), BF16),
        scratch_shapes=[pltpu.VMEM((group, 2 * two, LANES), BF16),
                        pltpu.VMEM((group, CHUNK, LANES), F32),
                        pltpu.VMEM((group, two, LANES), F32),
                        pltpu.VMEM((group, two, LANES), F32),
                        pltpu.VMEM((seq, LANES), F32)],
        compiler_params=_cparams(("parallel", "parallel")),
        name="wkv7",
    )(proj, proj, proj, proj, lw, ic, kk, ka, rk, lnw, lnb)


def _attn_kernel(q_ref, k_ref, v_ref, g_ref, lam_ref, sw_ref, o_ref, *, n_heads, lambda_init):
    tq = q_ref.shape[0]
    h = pl.program_id(1)
    qi = pl.program_id(2)
    lane = lax.broadcasted_iota(jnp.int32, (1, LANES), 1)
    qf = q_ref[...].astype(F32) * (1.0 / math.sqrt(HEAD))
    q1 = jnp.where(lane < HEAD, qf, 0.0).astype(BF16)
    q2 = jnp.where(lane >= HEAD, qf, 0.0).astype(BF16)
    hf = (h + 1).astype(F32)
    slope = jnp.exp2(jnp.full((1, tq), -8.0 / n_heads, F32) * hf)
    colf = lax.broadcasted_iota(jnp.int32, (1, tq), 1).astype(F32)

    def update(s, m, l, acc, vs):
        mn = jnp.maximum(m, jnp.max(s, axis=1, keepdims=True))
        alpha = jnp.exp(m - mn)
        p = jnp.exp(s - mn)
        l = alpha * l + jnp.sum(p, axis=1, keepdims=True)
        acc = alpha * acc + _dot(p.astype(BF16), vs)
        return mn, l, acc

    def block(j, carry, masked):
        m1, l1, a1, m2, l2, a2 = carry
        rows = pl.ds(pl.multiple_of(j * tq, tq), tq)
        ks = k_ref[rows, :]
        vs = v_ref[rows, :]
        bias = slope * (colf + ((j - qi) * tq).astype(F32))
        s1 = _dot_nt(q1, ks) + bias
        s2 = _dot_nt(q2, ks) + bias
        if masked:
            ri = lax.broadcasted_iota(jnp.int32, (tq, tq), 0)
            ci = lax.broadcasted_iota(jnp.int32, (tq, tq), 1)
            keep = ci <= ri
            s1 = jnp.where(keep, s1, NEG_BIG)
            s2 = jnp.where(keep, s2, NEG_BIG)
        m1, l1, a1 = update(s1, m1, l1, a1, vs)
        m2, l2, a2 = update(s2, m2, l2, a2, vs)
        return m1, l1, a1, m2, l2, a2

    neg = jnp.full((tq, 1), NEG_BIG, F32)
    zl = jnp.zeros((tq, 1), F32)
    za = jnp.zeros((tq, LANES), F32)
    carry = lax.fori_loop(0, qi, lambda j, c: block(j, c, False), (neg, zl, za, neg, zl, za))
    m1, l1, a1, m2, l2, a2 = block(qi, carry, True)

    lam = lam_ref[...]
    lam_full = (jnp.exp(jnp.sum(lam[0:1] * lam[1:2], axis=1, keepdims=True))
                - jnp.exp(jnp.sum(lam[2:3] * lam[3:4], axis=1, keepdims=True)) + lambda_init)
    o = a1 / l1 - lam_full * (a2 / l2)
    o = o * lax.rsqrt(jnp.mean(o * o, axis=-1, keepdims=True) + SUBLN_EPS) * sw_ref[...] * (1.0 - lambda_init)
    g = g_ref[...].astype(F32)
    o_ref[...] = (o * (g / (1.0 + jnp.exp(-g)))).astype(o_ref.dtype)


def _diff_attn(qkvg, lam, sw, *, batch, seq, tq, lambda_init):
    m, n4 = qkvg.shape
    w = n4 // 4
    nh = w // LANES
    nq = seq // tq
    q_spec = pl.BlockSpec((tq, LANES), lambda b, h, i: (b * nq + i, h))
    k_spec = pl.BlockSpec((seq, LANES), lambda b, h, i: (b, nh + h))
    v_spec = pl.BlockSpec((seq, LANES), lambda b, h, i: (b, 2 * nh + h))
    g_spec = pl.BlockSpec((tq, LANES), lambda b, h, i: (b * nq + i, 3 * nh + h))
    return pl.pallas_call(
        functools.partial(_attn_kernel, n_heads=nh, lambda_init=lambda_init),
        grid=(batch, nh, nq),
        in_specs=[q_spec, k_spec, v_spec, g_spec,
                  pl.BlockSpec(lam.shape, lambda b, h, i: (0, 0)),
                  pl.BlockSpec((1, LANES), lambda b, h, i: (0, 0))],
        out_specs=pl.BlockSpec((tq, LANES), lambda b, h, i: (b * nq + i, h)),
        out_shape=jax.ShapeDtypeStruct((m, w), BF16),
        compiler_params=_cparams(("parallel", "parallel", "arbitrary")),
        name="diff_attn",
    )(qkvg, qkvg, qkvg, qkvg, lam, sw)


def kernel(x, norm_w, rwkv_mu, rwkv_w_in, rwkv_w0, rwkv_w1, rwkv_w2, rwkv_a0, rwkv_a1, rwkv_a2, rwkv_k_k, rwkv_k_a, rwkv_r_k, rwkv_ln_w, rwkv_ln_b, rwkv_w_out, diff_w_in, diff_lambda, diff_subln_w, diff_w_out, final_norm_w):
    batch, seq, d = x.shape
    m = batch * seq
    assert d % LANES == 0 and seq % CHUNK == 0
    assert norm_w.shape[0] == 2 and rwkv_mu.shape[0] == 1 and diff_w_in.shape[0] == 1
    tm_big = min(1024, m)
    tn_big = min(1024, d)
    tm_prep = min(256, seq)
    tm_last = min(512, m)
    tq = min(256, seq)

    x2 = x.reshape(m, d)
    row = lambda a: a.reshape(1, -1)
    lora = rwkv_w1.shape[-1]
    pad_c = lambda a: jnp.pad(a, ((0, 0), (0, LORA_PAD - lora))).astype(BF16)
    pad_r = lambda a: jnp.pad(a, ((0, LORA_PAD - lora), (0, 0))).astype(BF16)

    xs, lw, ic = _rwkv_prep(x2, row(norm_w[0]), rwkv_mu[0], row(rwkv_w0[0]), pad_c(rwkv_w1[0]), pad_r(rwkv_w2[0]),
                            row(rwkv_a0[0]), pad_c(rwkv_a1[0]), pad_r(rwkv_a2[0]),
                            batch=batch, seq=seq, tm=tm_prep)
    proj = _proj4(xs, rwkv_w_in[0].astype(BF16), tm=tm_big, tn=tn_big, out_dtype=F32)
    yg = _wkv(proj, lw, ic, row(rwkv_k_k[0]), row(rwkv_k_a[0]), row(rwkv_r_k[0]), row(rwkv_ln_w[0]),
              row(rwkv_ln_b[0]), batch=batch, seq=seq)
    x1 = _matmul_residual(yg, rwkv_w_out[0].astype(BF16), x2, tm=tm_big, tn=tn_big)

    lambda_init = 0.8 - 0.6 * math.exp(-0.3 * 1)
    h1 = _rms_norm(x1, row(norm_w[1]), tm=tm_last, out_dtype=BF16)
    qkvg = _matmul(h1, diff_w_in[0].astype(BF16), tm=tm_big, tn=tn_big, out_dtype=BF16)
    att = _diff_attn(qkvg, diff_lambda[0], row(diff_subln_w[0]), batch=batch, seq=seq, tq=tq,
                     lambda_init=lambda_init)
    out = _matmul_residual_norm(att, diff_w_out[0].astype(BF16), x1, row(final_norm_w), tm=tm_last)
    return out.reshape(batch, seq, d)
```

```python
import functools
import math

import jax
import jax.numpy as jnp
from jax import lax
from jax.experimental import pallas as pl
from jax.experimental.pallas import tpu as pltpu

F32 = jnp.float32
BF16 = jnp.bfloat16

NORM_EPS = 1e-6
GN_EPS = 64e-5
SUBLN_EPS = 1e-5
HEAD = 64
LANES = 128
CHUNK = 64
LORA_PAD = 128
VMEM_LIMIT = 48 * 1024 * 1024
NEG_BIG = -1e30


def _cparams(sem):
    return pltpu.CompilerParams(dimension_semantics=sem, vmem_limit_bytes=VMEM_LIMIT)


def _dot(a, b):
    return jnp.dot(a, b, preferred_element_type=F32)


def _dot_nt(a, b):
    return lax.dot_general(a, b, (((1,), (1,)), ((), ())), preferred_element_type=F32)


def _dot_tn(a, b):
    return lax.dot_general(a, b, (((0,), (0,)), ((), ())), preferred_element_type=F32)


def _split2(x):
    hi = x.astype(BF16)
    lo = (x - hi.astype(F32)).astype(BF16)
    return hi, lo


def _mm(a, b, passes, nt=False):
    d = _dot_nt if nt else _dot
    if passes == 1:
        return d(a.astype(BF16), b.astype(BF16))
    ah, al = _split2(a)
    bh, bl = _split2(b)
    return d(ah, bh) + (d(al, bh) + d(ah, bl))


def _mm_exact_rhs(a, e_bf16, parts):
    acc = None
    rem = a
    for _ in range(parts):
        piece = rem.astype(BF16)
        term = _dot(piece, e_bf16)
        acc = term if acc is None else acc + term
        rem = rem - piece.astype(F32)
    return acc


def _mm_exact_lhs(e_bf16, a, parts):
    acc = None
    rem = a
    for _ in range(parts):
        piece = rem.astype(BF16)
        term = _dot(e_bf16, piece)
        acc = term if acc is None else acc + term
        rem = rem - piece.astype(F32)
    return acc


def _rwkv_prep_kernel(x_ref, nw_ref, mu_ref, w0_ref, w1_ref, w2_ref, a0_ref, a1_ref, a2_ref,
                      xs_ref, lw_ref, ic_ref, carry_ref):
    t = pl.program_id(1)
    x = x_ref[...]
    tm = x.shape[0]
    h = x * lax.rsqrt(jnp.mean(x * x, axis=-1, keepdims=True) + NORM_EPS) * nw_ref[...]

    @pl.when(t == 0)
    def _():
        carry_ref[...] = jnp.zeros_like(carry_ref)

    prev_last = carry_ref[0:1, :]
    rolled = pltpu.roll(h, 1, axis=0)
    row = lax.broadcasted_iota(jnp.int32, h.shape, 0)
    h_prev = jnp.where(row == 0, prev_last, rolled)
    carry_ref[0:1, :] = h[tm - 1:tm, :]
    d = h_prev - h
    mu = mu_ref[...]
    xs_ref[0] = (h + d * mu[0:1]).astype(BF16)
    xs_ref[1] = (h + d * mu[2:3]).astype(BF16)
    xs_ref[2] = (h + d * mu[3:4]).astype(BF16)
    xs_ref[3] = (h + d * mu[5:6]).astype(BF16)
    xw = (h + d * mu[1:2]).astype(BF16)
    xa = (h + d * mu[4:5]).astype(BF16)
    zw = w0_ref[...] + _dot(jnp.tanh(_dot(xw, w1_ref[...])).astype(BF16), w2_ref[...])
    lw_ref[...] = (-math.exp(-0.5)) / (1.0 + jnp.exp(-zw))
    za = a0_ref[...] + _dot(_dot(xa, a1_ref[...]).astype(BF16), a2_ref[...])
    ic_ref[...] = 1.0 / (1.0 + jnp.exp(-za))


def _rwkv_prep(x2, nw, mu, w0, w1p, w2p, a0, a1p, a2p, *, batch, seq, tm):
    m, d = x2.shape
    nt = seq // tm
    row_spec = pl.BlockSpec((tm, d), lambda b, t: (b * nt + t, 0))
    vec_spec = pl.BlockSpec((1, d), lambda b, t: (0, 0))

    def full(a):
        return pl.BlockSpec(a.shape, lambda b, t: (0,) * a.ndim)

    return pl.pallas_call(
        _rwkv_prep_kernel,
        grid=(batch, nt),
        in_specs=[row_spec, vec_spec, full(mu), vec_spec, full(w1p), full(w2p), vec_spec, full(a1p), full(a2p)],
        out_specs=[pl.BlockSpec((4, tm, d), lambda b, t: (0, b * nt + t, 0)), row_spec, row_spec],
        out_shape=[jax.ShapeDtypeStruct((4, m, d), BF16),
                   jax.ShapeDtypeStruct((m, d), F32),
                   jax.ShapeDtypeStruct((m, d), F32)],
        scratch_shapes=[pltpu.VMEM((8, d), F32)],
        compiler_params=_cparams(("parallel", "arbitrary")),
        name="rwkv_prep",
    )(x2, nw, mu, w0, w1p, w2p, a0, a1p, a2p)


def _mm_kernel(a_ref, b_ref, o_ref):
    o_ref[...] = _dot(a_ref[...], b_ref[...]).astype(o_ref.dtype)


def _proj4(xs, w, *, tm, tn, out_dtype):
    _, m, k = xs.shape
    n = w.shape[1]
    per_path = (n // 4) // tn
    return pl.pallas_call(
        _mm_kernel,
        grid=(m // tm, n // tn),
        in_specs=[pl.BlockSpec((None, tm, k), lambda i, j: (j // per_path, i, 0)),
                  pl.BlockSpec((k, tn), lambda i, j: (0, j))],
        out_specs=pl.BlockSpec((tm, tn), lambda i, j: (i, j)),
        out_shape=jax.ShapeDtypeStruct((m, n), out_dtype),
        compiler_params=_cparams(("parallel", "arbitrary")),
        name="proj4",
    )(xs, w)


def _matmul(a, w, *, tm, tn, out_dtype):
    m, k = a.shape
    n = w.shape[1]
    return pl.pallas_call(
        _mm_kernel,
        grid=(m // tm, n // tn),
        in_specs=[pl.BlockSpec((tm, k), lambda i, j: (i, 0)),
                  pl.BlockSpec((k, tn), lambda i, j: (0, j))],
        out_specs=pl.BlockSpec((tm, tn), lambda i, j: (i, j)),
        out_shape=jax.ShapeDtypeStruct((m, n), out_dtype),
        compiler_params=_cparams(("parallel", "arbitrary")),
        name="matmul",
    )(a, w)


def _mm_res_kernel(a_ref, b_ref, r_ref, o_ref):
    o_ref[...] = r_ref[...] + _dot(a_ref[...], b_ref[...])


def _matmul_residual(a, w, res, *, tm, tn):
    m, k = a.shape
    n = w.shape[1]
    return pl.pallas_call(
        _mm_res_kernel,
        grid=(m // tm, n // tn),
        in_specs=[pl.BlockSpec((tm, k), lambda i, j: (i, 0)),
                  pl.BlockSpec((k, tn), lambda i, j: (0, j)),
                  pl.BlockSpec((tm, tn), lambda i, j: (i, j))],
        out_specs=pl.BlockSpec((tm, tn), lambda i, j: (i, j)),
        out_shape=jax.ShapeDtypeStruct((m, n), F32),
        compiler_params=_cparams(("parallel", "arbitrary")),
        name="matmul_residual",
    )(a, w, res)


def _mm_res_norm_kernel(a_ref, b_ref, r_ref, nw_ref, o_ref):
    y = r_ref[...] + _dot(a_ref[...], b_ref[...])
    o_ref[...] = y * lax.rsqrt(jnp.mean(y * y, axis=-1, keepdims=True) + NORM_EPS) * nw_ref[...]


def _matmul_residual_norm(a, w, res, nw, *, tm):
    m, k = a.shape
    n = w.shape[1]
    return pl.pallas_call(
        _mm_res_norm_kernel,
        grid=(m // tm,),
        in_specs=[pl.BlockSpec((tm, k), lambda i: (i, 0)),
                  pl.BlockSpec((k, n), lambda i: (0, 0)),
                  pl.BlockSpec((tm, n), lambda i: (i, 0)),
                  pl.BlockSpec((1, n), lambda i: (0, 0))],
        out_specs=pl.BlockSpec((tm, n), lambda i: (i, 0)),
        out_shape=jax.ShapeDtypeStruct((m, n), F32),
        compiler_params=_cparams(("parallel",)),
        name="matmul_residual_norm",
    )(a, w, res, nw)


def _norm_kernel(x_ref, nw_ref, o_ref):
    x = x_ref[...]
    o_ref[...] = (x * lax.rsqrt(jnp.mean(x * x, axis=-1, keepdims=True) + NORM_EPS) * nw_ref[...]).astype(o_ref.dtype)


def _rms_norm(x2, nw, *, tm, out_dtype):
    m, d = x2.shape
    return pl.pallas_call(
        _norm_kernel,
        grid=(m // tm,),
        in_specs=[pl.BlockSpec((tm, d), lambda i: (i, 0)), pl.BlockSpec((1, d), lambda i: (0, 0))],
        out_specs=pl.BlockSpec((tm, d), lambda i: (i, 0)),
        out_shape=jax.ShapeDtypeStruct((m, d), out_dtype),
        compiler_params=_cparams(("parallel",)),
        name="rms_norm",
    )(x2, nw)


WKV_GROUP = 4


def _wkv_kernel(r_ref, k_ref, v_ref, g_ref, lw_ref, ic_ref, kk_ref, ka_ref, rk_ref, lnw_ref, lnb_ref,
                o_ref, rm_sc, y0_sc, g_sc, wc_sc, y_sc, *, group, post_rows):
    seq = r_ref.shape[0]
    ngroups = seq // (CHUNK * group)
    two = 2 * CHUNK

    lane1 = lax.broadcasted_iota(jnp.int32, (1, LANES), 1)
    m0 = (lane1 < HEAD).astype(F32)
    m1 = 1.0 - m0
    row = lax.broadcasted_iota(jnp.int32, (two, two), 0)
    col = lax.broadcasted_iota(jnp.int32, (two, two), 1)
    strict = row > col
    incl = row >= col
    eye = (row == col).astype(F32)
    ones_bd = ((row < HEAD) == (col < HEAD)).astype(BF16)
    tril_c = incl[:CHUNK, :CHUNK].astype(BF16)
    zeros_b = jnp.zeros((two, LANES), BF16)

    kk_p = kk_ref[...]
    ka_p = ka_ref[...]

    def segsum(z):
        return _mm_exact_rhs(z, ones_bd, 2)

    def stack(z):
        return jnp.concatenate([z * m0, z * m1], axis=0)

    def chunk_rows(c):
        return pl.ds(pl.multiple_of(c * CHUNK, CHUNK), CHUNK)

    def phase_a(gi):
        us = range(group)
        rows = [chunk_rows(gi * group + u) for u in us]
        k = [k_ref[rw, :] for rw in rows]
        lw = [lw_ref[rw, :] for rw in rows]
        kkv = [k[u] * kk_p for u in us]
        n2 = [segsum(kkv[u] * kkv[u]) for u in us]
        lin = [_mm_exact_lhs(tril_c, lw[u], 3) for u in us]
        yield
        a_s, r_s, v_s, bkt, wcb, a_all = [], [], [], [], [], []
        for u in us:
            ic = ic_ref[rows[u], :]
            kkn = kkv[u] / jnp.maximum(jnp.sqrt(n2[u]), 1e-12)
            kh = k[u] * (1.0 + (ic - 1.0) * ka_p)
            e_in = jnp.exp(lin[u])
            e_ex = jnp.exp(lin[u] - lw[u])
            e_neg = jnp.exp(-lin[u])
            b_f = stack(kkn * ic * e_neg)
            k_f = stack(kh * e_neg)
            a_s.append(stack(-kkn * e_ex).astype(BF16))
            r_s.append(stack(r_ref[rows[u], :] * e_in).astype(BF16))
            v_s.append(stack(v_ref[rows[u], :]).astype(BF16))
            a_all.append(_dot_nt(jnp.concatenate([a_s[u], r_s[u]], axis=0),
                                 jnp.concatenate([b_f, k_f], axis=0).astype(BF16)))
            bkt.append(jnp.concatenate([b_f.T, k_f.T], axis=1).astype(BF16))
            wcol = jnp.sum(eye * e_in[CHUNK - 1:CHUNK, :], axis=1, keepdims=True)
            wcb.append(jnp.broadcast_to(wcol, (two, LANES)))
        yield
        t_inv, pwb, a_rb, av = [], [], [], []
        for u in us:
            a_ab = jnp.where(strict, a_all[u][:two, :two], 0.0)
            a_ak = jnp.where(strict, a_all[u][:two, two:], 0.0).astype(BF16)
            a_rk = jnp.where(incl, a_all[u][two:, two:], 0.0).astype(BF16)
            a_rb.append(jnp.where(incl, a_all[u][two:, :two], 0.0).astype(BF16))
            av.append(_dot(jnp.concatenate([a_ak, a_rk], axis=0), v_s[u]))
            t_inv.append(eye + a_ab)
            pwb.append(a_ab.astype(BF16))
        yield
        pwb = [_dot(pwb[u], pwb[u]).astype(BF16) for u in us]
        yield
        for _ in range(4):
            t_inv = [t_inv[u] + _dot(t_inv[u].astype(BF16), pwb[u]) for u in us]
            pwb = [_dot(pwb[u], pwb[u]).astype(BF16) for u in us]
            yield
        t_inv = [t_inv[u] + _dot(t_inv[u].astype(BF16), pwb[u]) for u in us]
        yield
        pqb = [_dot(t_inv[u].astype(BF16),
                    jnp.concatenate([a_s[u], av[u][:two].astype(BF16)], axis=1)).astype(BF16) for u in us]
        yield
        rq = [_dot(a_rb[u], pqb[u]) for u in us]
        mg = [_dot(bkt[u], jnp.concatenate([pqb[u], jnp.concatenate([zeros_b, v_s[u]], axis=1)], axis=0)) for u in us]
        yield
        for u in us:
            y0 = rq[u][:, LANES:] + av[u][two:]
            rm_sc[u] = jnp.concatenate([r_s[u].astype(F32) + rq[u][:, :LANES], wcb[u] * mg[u][:, :LANES]],
                                       axis=0).astype(BF16)
            g_sc[u] = wcb[u] * mg[u][:, LANES:]
            y0_sc[u] = y0[:CHUNK] + y0[CHUNK:]
            wc_sc[u] = wcb[u]

    def state_step(gi, u, h):
        res = _dot(rm_sc[u], h.astype(BF16))
        y_sc[chunk_rows(gi * group + u), :] = res[:CHUNK] + res[CHUNK:two] + y0_sc[u]
        return wc_sc[u] * h + res[two:] + g_sc[u]

    def run(a_gi, b_gi, h):
        done = 0
        if a_gi is None:
            for u in range(group):
                h = state_step(b_gi, u, h)
            return h
        for i, _ in enumerate(phase_a(a_gi)):
            if b_gi is not None and done < group and i % 2 == 1:
                h = state_step(b_gi, done, h)
                done += 1
        assert b_gi is None or done == group
        return h

    run(0, None, None)
    h = lax.fori_loop(0, ngroups - 1, lambda gi, h: run(gi + 1, gi, h), jnp.zeros((two, LANES), F32))
    run(None, ngroups - 1, h)

    rk_p = rk_ref[...]
    lnw_p = lnw_ref[...]
    lnb_p = lnb_ref[...]
    inv_n = 1.0 / HEAD

    def phase_c(i, carry):
        rows = pl.ds(pl.multiple_of(i * post_rows, post_rows), post_rows)
        y = y_sc[rows, :]
        mean = segsum(y) * inv_n
        yc = y - mean
        var = segsum(yc * yc) * inv_n
        gn = yc * lax.rsqrt(var + GN_EPS) * lnw_p + lnb_p
        r = r_ref[rows, :]
        kh = k_ref[rows, :] * (1.0 + (ic_ref[rows, :] - 1.0) * ka_p)
        bonus = segsum(r * kh * rk_p) * v_ref[rows, :]
        g = g_ref[rows, :]
        o_ref[rows, :] = ((gn + bonus) * (g / (1.0 + jnp.exp(-g)))).astype(o_ref.dtype)
        return carry

    lax.fori_loop(0, seq // post_rows, phase_c, 0)


def _wkv(proj, lw, ic, kk, ka, rk, lnw, lnb, *, batch, seq):
    m, n4 = proj.shape
    w = n4 // 4
    nhp = w // LANES
    two = 2 * CHUNK
    group = WKV_GROUP
    assert seq % (CHUNK * group) == 0

    def col_spec(p):
        return pl.BlockSpec((seq, LANES), lambda b, hp: (b, p * nhp + hp))

    par_spec = pl.BlockSpec((1, LANES), lambda b, hp: (0, hp))
    act_spec = pl.BlockSpec((seq, LANES), lambda b, hp: (b, hp))
    return pl.pallas_call(
        functools.partial(_wkv_kernel, group=group, post_rows=min(seq, 256)),
        grid=(batch, nhp),
        in_specs=[col_spec(0), col_spec(1), col_spec(2), col_spec(3), act_spec, act_spec,
                  par_spec, par_spec, par_spec, par_spec, par_spec],
        out_specs=act_spec,
        out_shape=jax.ShapeDtypeStruct((m, w), BF16),
        scratch_shapes=[pltpu.VMEM((group, 2 * two, LANES), BF16),
                        pltpu.VMEM((group, CHUNK, LANES), F32),
                        pltpu.VMEM((group, two, LANES), F32),
                        pltpu.VMEM((group, two, LANES), F32),
                        pltpu.VMEM((seq, LANES), F32)],
        compiler_params=_cparams(("parallel", "parallel")),
        name="wkv7",
    )(proj, proj, proj, proj, lw, ic, kk, ka, rk, lnw, lnb)


def _attn_kernel(q_ref, k_ref, v_ref, g_ref, lam_ref, sw_ref, o_ref, z_a, z_b, p_a, p_b, acc_sc,
                 *, tk, n_heads, lambda_init):
    tq = q_ref.shape[0]
    ng = 2 * tq // LANES
    h = pl.program_id(1)
    qi = pl.program_id(2)
    lane = lax.broadcasted_iota(jnp.int32, (1, LANES), 1)
    qf = q_ref[...].astype(F32) * (1.0 / math.sqrt(HEAD))
    qq = jnp.concatenate([jnp.where(lane < HEAD, qf, 0.0), jnp.where(lane >= HEAD, qf, 0.0)], axis=0).astype(BF16)
    slope = jnp.exp2(jnp.full((tk, LANES), -8.0 / n_heads, F32) * (h + 1).astype(F32))
    key_l = lax.broadcasted_iota(jnp.int32, (tk, LANES), 0)
    qry_l = lax.broadcasted_iota(jnp.int32, (tk, LANES), 1)
    bias0 = slope * key_l.astype(F32)
    slope_row = slope[0:1, :]

    def key_rows(j):
        return pl.ds(pl.multiple_of(j * tk, tk), tk)

    def scores_to(j, z_ref):
        z_ref[...] = _dot_nt(k_ref[key_rows(j), :], qq)

    def accumulate(j, p_ref, alpha):
        acc_sc[...] = alpha * acc_sc[...] + _dot_tn(v_ref[key_rows(j), :], p_ref[...])

    def softmax(j, z_ref, p_ref, m, l, diag):
        c_blk = slope_row * (j * tk - qi * tq).astype(F32)
        ms, ls, alphas = [], [], []
        for g in range(ng):
            sl = slice(g * LANES, (g + 1) * LANES)
            q_off = (g % (ng // 2)) * LANES
            if diag is not None and q_off + LANES <= diag * tk:
                p_ref[:, sl] = jnp.zeros((tk, LANES), BF16)
                ms.append(m[:, sl])
                ls.append(l[:, sl])
                alphas.append(jnp.ones((1, LANES), F32))
                continue
            zt = z_ref[:, sl] + bias0
            if diag is not None and q_off < (diag + 1) * tk:
                zt = jnp.where(key_l + diag * tk <= qry_l + q_off, zt, NEG_BIG)
            mg = m[:, sl]
            mn = jnp.maximum(mg, jnp.max(zt, axis=0, keepdims=True) + c_blk)
            alpha = jnp.exp(mg - mn)
            pt = jnp.exp(zt - (mn - c_blk))
            ls.append(alpha * l[:, sl] + jnp.sum(pt, axis=0, keepdims=True))
            p_ref[:, sl] = pt.astype(BF16)
            ms.append(mn)
            alphas.append(alpha)
        return jnp.concatenate(ms, axis=1), jnp.concatenate(ls, axis=1), jnp.concatenate(alphas, axis=1)

    def pair(j0, carry, diag, last):
        m, l, alpha = carry
        accumulate(jnp.maximum(j0 - 1, 0), p_b, alpha)
        scores_to(j0 + 1, z_b)
        m, l, alpha = softmax(j0, z_a, p_a, m, l, 0 if diag else None)
        accumulate(j0, p_a, alpha)
        if not last:
            scores_to(j0 + 2, z_a)
        m, l, alpha = softmax(j0 + 1, z_b, p_b, m, l, 1 if diag else None)
        return m, l, alpha

    assert tq == 2 * tk
    acc_sc[...] = jnp.zeros_like(acc_sc)
    p_b[...] = jnp.zeros_like(p_b)
    scores_to(0, z_a)
    init = (jnp.full((1, 2 * tq), NEG_BIG, F32), jnp.zeros((1, 2 * tq), F32), jnp.ones((1, 2 * tq), F32))
    carry = lax.fori_loop(0, qi, lambda i, c: pair(2 * i, c, False, False), init)
    m, l, alpha = pair(2 * qi, carry, True, True)
    accumulate(2 * qi + 1, p_b, alpha)
    acc = acc_sc[...]

    lam = lam_ref[...]
    lam_full = (jnp.exp(jnp.sum(lam[0:1] * lam[1:2], axis=1, keepdims=True))
                - jnp.exp(jnp.sum(lam[2:3] * lam[3:4], axis=1, keepdims=True)) + lambda_init)
    on = acc / l
    ot = on[:, :tq] - lam_full * on[:, tq:]
    ot = ot * lax.rsqrt(jnp.mean(ot * ot, axis=0, keepdims=True) + SUBLN_EPS)
    o = ot.T * (sw_ref[...] * (1.0 - lambda_init))
    g = g_ref[...].astype(F32)
    o_ref[...] = (o * (g / (1.0 + jnp.exp(-g)))).astype(o_ref.dtype)


def _diff_attn(qkvg, lam, sw, *, batch, seq, tq, lambda_init):
    m, n4 = qkvg.shape
    w = n4 // 4
    nh = w // LANES
    nq = seq // tq
    tk = tq // 2
    q_spec = pl.BlockSpec((tq, LANES), lambda b, h, i: (b * nq + i, h))
    k_spec = pl.BlockSpec((seq, LANES), lambda b, h, i: (b, nh + h))
    v_spec = pl.BlockSpec((seq, LANES), lambda b, h, i: (b, 2 * nh + h))
    g_spec = pl.BlockSpec((tq, LANES), lambda b, h, i: (b * nq + i, 3 * nh + h))
    return pl.pallas_call(
        functools.partial(_attn_kernel, tk=tk, n_heads=nh, lambda_init=lambda_init),
        grid=(batch, nh, nq),
        in_specs=[q_spec, k_spec, v_spec, g_spec,
                  pl.BlockSpec(lam.shape, lambda b, h, i: (0, 0)),
                  pl.BlockSpec((1, LANES), lambda b, h, i: (0, 0))],
        out_specs=pl.BlockSpec((tq, LANES), lambda b, h, i: (b * nq + i, h)),
        out_shape=jax.ShapeDtypeStruct((m, w), BF16),
        scratch_shapes=[pltpu.VMEM((tk, 2 * tq), F32), pltpu.VMEM((tk, 2 * tq), F32),
                        pltpu.VMEM((tk, 2 * tq), BF16), pltpu.VMEM((tk, 2 * tq), BF16),
                        pltpu.VMEM((LANES, 2 * tq), F32)],
        compiler_params=_cparams(("parallel", "parallel", "arbitrary")),
        name="diff_attn",
    )(qkvg, qkvg, qkvg, qkvg, lam, sw)


def kernel(x, norm_w, rwkv_mu, rwkv_w_in, rwkv_w0, rwkv_w1, rwkv_w2, rwkv_a0, rwkv_a1, rwkv_a2, rwkv_k_k, rwkv_k_a, rwkv_r_k, rwkv_ln_w, rwkv_ln_b, rwkv_w_out, diff_w_in, diff_lambda, diff_subln_w, diff_w_out, final_norm_w):
    batch, seq, d = x.shape
    m = batch * seq
    assert d % LANES == 0 and seq % CHUNK == 0
    assert norm_w.shape[0] == 2 and rwkv_mu.shape[0] == 1 and diff_w_in.shape[0] == 1
    tm_big = min(1024, m)
    tn_big = min(1024, d)
    tm_prep = min(256, seq)
    tm_last = min(512, m)
    tq = min(512, seq)

    x2 = x.reshape(m, d)
    row = lambda a: a.reshape(1, -1)
    lora = rwkv_w1.shape[-1]
    pad_c = lambda a: jnp.pad(a, ((0, 0), (0, LORA_PAD - lora))).astype(BF16)
    pad_r = lambda a: jnp.pad(a, ((0, LORA_PAD - lora), (0, 0))).astype(BF16)

    xs, lw, ic = _rwkv_prep(x2, row(norm_w[0]), rwkv_mu[0], row(rwkv_w0[0]), pad_c(rwkv_w1[0]), pad_r(rwkv_w2[0]),
                            row(rwkv_a0[0]), pad_c(rwkv_a1[0]), pad_r(rwkv_a2[0]),
                            batch=batch, seq=seq, tm=tm_prep)
    proj = _proj4(xs, rwkv_w_in[0].astype(BF16), tm=tm_big, tn=tn_big, out_dtype=F32)
    yg = _wkv(proj, lw, ic, row(rwkv_k_k[0]), row(rwkv_k_a[0]), row(rwkv_r_k[0]), row(rwkv_ln_w[0]),
              row(rwkv_ln_b[0]), batch=batch, seq=seq)
    x1 = _matmul_residual(yg, rwkv_w_out[0].astype(BF16), x2, tm=tm_big, tn=tn_big)

    lambda_init = 0.8 - 0.6 * math.exp(-0.3 * 1)
    h1 = _rms_norm(x1, row(norm_w[1]), tm=tm_last, out_dtype=BF16)
    qkvg = _matmul(h1, diff_w_in[0].astype(BF16), tm=tm_big, tn=tn_big, out_dtype=BF16)
    att = _diff_attn(qkvg, diff_lambda[0], row(diff_subln_w[0]), batch=batch, seq=seq, tq=tq,
                     lambda_init=lambda_init)
    out = _matmul_residual_norm(att, diff_w_out[0].astype(BF16), x1, row(final_norm_w), tm=tm_last)
    return out.reshape(batch, seq, d)
```

```python
import functools
import math

import jax
import jax.numpy as jnp
from jax import lax
from jax.experimental import pallas as pl
from jax.experimental.pallas import tpu as pltpu

F32 = jnp.float32
BF16 = jnp.bfloat16

NORM_EPS = 1e-6
GN_EPS = 64e-5
SUBLN_EPS = 1e-5
HEAD = 64
LANES = 128
CHUNK = 64
LORA_PAD = 128
VMEM_LIMIT = 48 * 1024 * 1024
NEG_BIG = -1e30
LOG2E = 1.0 / math.log(2.0)


def _cparams(sem):
    return pltpu.CompilerParams(dimension_semantics=sem, vmem_limit_bytes=VMEM_LIMIT)


def _dot(a, b):
    return jnp.dot(a, b, preferred_element_type=F32)


def _dot_nt(a, b):
    return lax.dot_general(a, b, (((1,), (1,)), ((), ())), preferred_element_type=F32)


def _dot_tn(a, b):
    return lax.dot_general(a, b, (((0,), (0,)), ((), ())), preferred_element_type=F32)


def _split2(x):
    hi = x.astype(BF16)
    lo = (x - hi.astype(F32)).astype(BF16)
    return hi, lo


def _mm(a, b, passes, nt=False):
    d = _dot_nt if nt else _dot
    if passes == 1:
        return d(a.astype(BF16), b.astype(BF16))
    ah, al = _split2(a)
    bh, bl = _split2(b)
    return d(ah, bh) + (d(al, bh) + d(ah, bl))


def _mm_exact_rhs(a, e_bf16, parts):
    acc = None
    rem = a
    for _ in range(parts):
        piece = rem.astype(BF16)
        term = _dot(piece, e_bf16)
        acc = term if acc is None else acc + term
        rem = rem - piece.astype(F32)
    return acc


def _mm_exact_lhs(e_bf16, a, parts):
    acc = None
    rem = a
    for _ in range(parts):
        piece = rem.astype(BF16)
        term = _dot(e_bf16, piece)
        acc = term if acc is None else acc + term
        rem = rem - piece.astype(F32)
    return acc


def _rwkv_prep_kernel(x_ref, nw_ref, mu_ref, w0_ref, w1_ref, w2_ref, a0_ref, a1_ref, a2_ref,
                      xs_ref, lw_ref, ic_ref, carry_ref):
    t = pl.program_id(1)
    x = x_ref[...]
    tm = x.shape[0]
    h = x * lax.rsqrt(jnp.mean(x * x, axis=-1, keepdims=True) + NORM_EPS) * nw_ref[...]

    @pl.when(t == 0)
    def _():
        carry_ref[...] = jnp.zeros_like(carry_ref)

    prev_last = carry_ref[0:1, :]
    rolled = pltpu.roll(h, 1, axis=0)
    row = lax.broadcasted_iota(jnp.int32, h.shape, 0)
    h_prev = jnp.where(row == 0, prev_last, rolled)
    carry_ref[0:1, :] = h[tm - 1:tm, :]
    d = h_prev - h
    mu = mu_ref[...]
    xs_ref[0] = (h + d * mu[0:1]).astype(BF16)
    xs_ref[1] = (h + d * mu[2:3]).astype(BF16)
    xs_ref[2] = (h + d * mu[3:4]).astype(BF16)
    xs_ref[3] = (h + d * mu[5:6]).astype(BF16)
    xw = (h + d * mu[1:2]).astype(BF16)
    xa = (h + d * mu[4:5]).astype(BF16)
    zw = w0_ref[...] + _dot(jnp.tanh(_dot(xw, w1_ref[...])).astype(BF16), w2_ref[...])
    lw_ref[...] = (-math.exp(-0.5)) / (1.0 + jnp.exp(-zw))
    za = a0_ref[...] + _dot(_dot(xa, a1_ref[...]).astype(BF16), a2_ref[...])
    ic_ref[...] = 1.0 / (1.0 + jnp.exp(-za))


def _rwkv_prep(x2, nw, mu, w0, w1p, w2p, a0, a1p, a2p, *, batch, seq, tm):
    m, d = x2.shape
    nt = seq // tm
    row_spec = pl.BlockSpec((tm, d), lambda b, t: (b * nt + t, 0))
    vec_spec = pl.BlockSpec((1, d), lambda b, t: (0, 0))

    def full(a):
        return pl.BlockSpec(a.shape, lambda b, t: (0,) * a.ndim)

    return pl.pallas_call(
        _rwkv_prep_kernel,
        grid=(batch, nt),
        in_specs=[row_spec, vec_spec, full(mu), vec_spec, full(w1p), full(w2p), vec_spec, full(a1p), full(a2p)],
        out_specs=[pl.BlockSpec((4, tm, d), lambda b, t: (0, b * nt + t, 0)), row_spec, row_spec],
        out_shape=[jax.ShapeDtypeStruct((4, m, d), BF16),
                   jax.ShapeDtypeStruct((m, d), F32),
                   jax.ShapeDtypeStruct((m, d), F32)],
        scratch_shapes=[pltpu.VMEM((8, d), F32)],
        compiler_params=_cparams(("parallel", "arbitrary")),
        name="rwkv_prep",
    )(x2, nw, mu, w0, w1p, w2p, a0, a1p, a2p)


def _mm_kernel(a_ref, b_ref, o_ref):
    o_ref[...] = _dot(a_ref[...], b_ref[...]).astype(o_ref.dtype)


def _proj4(xs, w, *, tm, tn, out_dtype):
    _, m, k = xs.shape
    n = w.shape[1]
    per_path = (n // 4) // tn
    return pl.pallas_call(
        _mm_kernel,
        grid=(m // tm, n // tn),
        in_specs=[pl.BlockSpec((None, tm, k), lambda i, j: (j // per_path, i, 0)),
                  pl.BlockSpec((k, tn), lambda i, j: (0, j))],
        out_specs=pl.BlockSpec((tm, tn), lambda i, j: (i, j)),
        out_shape=jax.ShapeDtypeStruct((m, n), out_dtype),
        compiler_params=_cparams(("parallel", "arbitrary")),
        name="proj4",
    )(xs, w)


def _matmul(a, w, *, tm, tn, out_dtype):
    m, k = a.shape
    n = w.shape[1]
    return pl.pallas_call(
        _mm_kernel,
        grid=(m // tm, n // tn),
        in_specs=[pl.BlockSpec((tm, k), lambda i, j: (i, 0)),
                  pl.BlockSpec((k, tn), lambda i, j: (0, j))],
        out_specs=pl.BlockSpec((tm, tn), lambda i, j: (i, j)),
        out_shape=jax.ShapeDtypeStruct((m, n), out_dtype),
        compiler_params=_cparams(("parallel", "arbitrary")),
        name="matmul",
    )(a, w)


def _mm_res_kernel(a_ref, b_ref, r_ref, o_ref):
    o_ref[...] = r_ref[...] + _dot(a_ref[...], b_ref[...])


def _matmul_residual(a, w, res, *, tm, tn):
    m, k = a.shape
    n = w.shape[1]
    return pl.pallas_call(
        _mm_res_kernel,
        grid=(m // tm, n // tn),
        in_specs=[pl.BlockSpec((tm, k), lambda i, j: (i, 0)),
                  pl.BlockSpec((k, tn), lambda i, j: (0, j)),
                  pl.BlockSpec((tm, tn), lambda i, j: (i, j))],
        out_specs=pl.BlockSpec((tm, tn), lambda i, j: (i, j)),
        out_shape=jax.ShapeDtypeStruct((m, n), F32),
        compiler_params=_cparams(("parallel", "arbitrary")),
        name="matmul_residual",
    )(a, w, res)


def _mm_res_norm_kernel(a_ref, b_ref, r_ref, nw_ref, o_ref):
    y = r_ref[...] + _dot(a_ref[...], b_ref[...])
    o_ref[...] = y * lax.rsqrt(jnp.mean(y * y, axis=-1, keepdims=True) + NORM_EPS) * nw_ref[...]


def _matmul_residual_norm(a, w, res, nw, *, tm):
    m, k = a.shape
    n = w.shape[1]
    return pl.pallas_call(
        _mm_res_norm_kernel,
        grid=(m // tm,),
        in_specs=[pl.BlockSpec((tm, k), lambda i: (i, 0)),
                  pl.BlockSpec((k, n), lambda i: (0, 0)),
                  pl.BlockSpec((tm, n), lambda i: (i, 0)),
                  pl.BlockSpec((1, n), lambda i: (0, 0))],
        out_specs=pl.BlockSpec((tm, n), lambda i: (i, 0)),
        out_shape=jax.ShapeDtypeStruct((m, n), F32),
        compiler_params=_cparams(("parallel",)),
        name="matmul_residual_norm",
    )(a, w, res, nw)


def _norm_kernel(x_ref, nw_ref, o_ref):
    x = x_ref[...]
    o_ref[...] = (x * lax.rsqrt(jnp.mean(x * x, axis=-1, keepdims=True) + NORM_EPS) * nw_ref[...]).astype(o_ref.dtype)


def _rms_norm(x2, nw, *, tm, out_dtype):
    m, d = x2.shape
    return pl.pallas_call(
        _norm_kernel,
        grid=(m // tm,),
        in_specs=[pl.BlockSpec((tm, d), lambda i: (i, 0)), pl.BlockSpec((1, d), lambda i: (0, 0))],
        out_specs=pl.BlockSpec((tm, d), lambda i: (i, 0)),
        out_shape=jax.ShapeDtypeStruct((m, d), out_dtype),
        compiler_params=_cparams(("parallel",)),
        name="rms_norm",
    )(x2, nw)


WKV_GROUP = 4


def _wkv_kernel(r_ref, k_ref, v_ref, g_ref, lw_ref, ic_ref, kk_ref, ka_ref, rk_ref, lnw_ref, lnb_ref,
                o_ref, rm_sc, y0_sc, g_sc, wc_sc, y_sc, *, group, post_rows):
    seq = r_ref.shape[0]
    ngroups = seq // (CHUNK * group)
    two = 2 * CHUNK

    lane1 = lax.broadcasted_iota(jnp.int32, (1, LANES), 1)
    m0 = (lane1 < HEAD).astype(F32)
    m1 = 1.0 - m0
    row = lax.broadcasted_iota(jnp.int32, (two, two), 0)
    col = lax.broadcasted_iota(jnp.int32, (two, two), 1)
    strict = row > col
    incl = row >= col
    eye = (row == col).astype(F32)
    ones_bd = ((row < HEAD) == (col < HEAD)).astype(BF16)
    tril_c = incl[:CHUNK, :CHUNK].astype(BF16)
    zeros_b = jnp.zeros((two, LANES), BF16)

    kk_p = kk_ref[...]
    ka_p = ka_ref[...]

    def segsum(z):
        return _mm_exact_rhs(z, ones_bd, 2)

    def stack(z):
        return jnp.concatenate([z * m0, z * m1], axis=0)

    def chunk_rows(c):
        return pl.ds(pl.multiple_of(c * CHUNK, CHUNK), CHUNK)

    def phase_a(gi):
        us = range(group)
        rows = [chunk_rows(gi * group + u) for u in us]
        k = [k_ref[rw, :] for rw in rows]
        lw = [lw_ref[rw, :] for rw in rows]
        kkv = [k[u] * kk_p for u in us]
        n2 = [segsum(kkv[u] * kkv[u]) for u in us]
        lin = [_mm_exact_lhs(tril_c, lw[u], 3) for u in us]
        yield
        a_s, r_s, v_s, bkt, wcb, a_all = [], [], [], [], [], []
        for u in us:
            ic = ic_ref[rows[u], :]
            kkn = kkv[u] / jnp.maximum(jnp.sqrt(n2[u]), 1e-12)
            kh = k[u] * (1.0 + (ic - 1.0) * ka_p)
            e_in = jnp.exp(lin[u])
            e_ex = jnp.exp(lin[u] - lw[u])
            e_neg = jnp.exp(-lin[u])
            b_f = stack(kkn * ic * e_neg)
            k_f = stack(kh * e_neg)
            a_s.append(stack(-kkn * e_ex).astype(BF16))
            r_s.append(stack(r_ref[rows[u], :] * e_in).astype(BF16))
            v_s.append(stack(v_ref[rows[u], :]).astype(BF16))
            a_all.append(_dot_nt(jnp.concatenate([a_s[u], r_s[u]], axis=0),
                                 jnp.concatenate([b_f, k_f], axis=0).astype(BF16)))
            bkt.append(jnp.concatenate([b_f.T, k_f.T], axis=1).astype(BF16))
            wcol = jnp.sum(eye * e_in[CHUNK - 1:CHUNK, :], axis=1, keepdims=True)
            wcb.append(jnp.broadcast_to(wcol, (two, LANES)))
        yield
        t_inv, pwb, a_rb, av = [], [], [], []
        for u in us:
            a_ab = jnp.where(strict, a_all[u][:two, :two], 0.0)
            a_ak = jnp.where(strict, a_all[u][:two, two:], 0.0).astype(BF16)
            a_rk = jnp.where(incl, a_all[u][two:, two:], 0.0).astype(BF16)
            a_rb.append(jnp.where(incl, a_all[u][two:, :two], 0.0).astype(BF16))
            av.append(_dot(jnp.concatenate([a_ak, a_rk], axis=0), v_s[u]))
            t_inv.append(eye + a_ab)
            pwb.append(a_ab.astype(BF16))
        yield
        pwb = [_dot(pwb[u], pwb[u]).astype(BF16) for u in us]
        yield
        for _ in range(4):
            t_inv = [t_inv[u] + _dot(t_inv[u].astype(BF16), pwb[u]) for u in us]
            pwb = [_dot(pwb[u], pwb[u]).astype(BF16) for u in us]
            yield
        t_inv = [t_inv[u] + _dot(t_inv[u].astype(BF16), pwb[u]) for u in us]
        yield
        pqb = [_dot(t_inv[u].astype(BF16),
                    jnp.concatenate([a_s[u], av[u][:two].astype(BF16)], axis=1)).astype(BF16) for u in us]
        yield
        rq = [_dot(a_rb[u], pqb[u]) for u in us]
        mg = [_dot(bkt[u], jnp.concatenate([pqb[u], jnp.concatenate([zeros_b, v_s[u]], axis=1)], axis=0)) for u in us]
        yield
        for u in us:
            y0 = rq[u][:, LANES:] + av[u][two:]
            rm_sc[u] = jnp.concatenate([r_s[u].astype(F32) + rq[u][:, :LANES], wcb[u] * mg[u][:, :LANES]],
                                       axis=0).astype(BF16)
            g_sc[u] = wcb[u] * mg[u][:, LANES:]
            y0_sc[u] = y0[:CHUNK] + y0[CHUNK:]
            wc_sc[u] = wcb[u]

    def state_step(gi, u, h):
        res = _dot(rm_sc[u], h.astype(BF16))
        y_sc[chunk_rows(gi * group + u), :] = res[:CHUNK] + res[CHUNK:two] + y0_sc[u]
        return wc_sc[u] * h + res[two:] + g_sc[u]

    def run(a_gi, b_gi, h):
        done = 0
        if a_gi is None:
            for u in range(group):
                h = state_step(b_gi, u, h)
            return h
        for i, _ in enumerate(phase_a(a_gi)):
            if b_gi is not None and done < group and i % 2 == 1:
                h = state_step(b_gi, done, h)
                done += 1
        assert b_gi is None or done == group
        return h

    run(0, None, None)
    h = lax.fori_loop(0, ngroups - 1, lambda gi, h: run(gi + 1, gi, h), jnp.zeros((two, LANES), F32))
    run(None, ngroups - 1, h)

    rk_p = rk_ref[...]
    lnw_p = lnw_ref[...]
    lnb_p = lnb_ref[...]
    inv_n = 1.0 / HEAD

    def phase_c(i, carry):
        rows = pl.ds(pl.multiple_of(i * post_rows, post_rows), post_rows)
        y = y_sc[rows, :]
        mean = segsum(y) * inv_n
        yc = y - mean
        var = segsum(yc * yc) * inv_n
        gn = yc * lax.rsqrt(var + GN_EPS) * lnw_p + lnb_p
        r = r_ref[rows, :]
        kh = k_ref[rows, :] * (1.0 + (ic_ref[rows, :] - 1.0) * ka_p)
        bonus = segsum(r * kh * rk_p) * v_ref[rows, :]
        g = g_ref[rows, :]
        o_ref[rows, :] = ((gn + bonus) * (g / (1.0 + jnp.exp(-g)))).astype(o_ref.dtype)
        return carry

    lax.fori_loop(0, seq // post_rows, phase_c, 0)


def _wkv(proj, lw, ic, kk, ka, rk, lnw, lnb, *, batch, seq):
    m, n4 = proj.shape
    w = n4 // 4
    nhp = w // LANES
    two = 2 * CHUNK
    group = WKV_GROUP
    assert seq % (CHUNK * group) == 0

    def col_spec(p):
        return pl.BlockSpec((seq, LANES), lambda b, hp: (b, p * nhp + hp))

    par_spec = pl.BlockSpec((1, LANES), lambda b, hp: (0, hp))
    act_spec = pl.BlockSpec((seq, LANES), lambda b, hp: (b, hp))
    return pl.pallas_call(
        functools.partial(_wkv_kernel, group=group, post_rows=min(seq, 256)),
        grid=(batch, nhp),
        in_specs=[col_spec(0), col_spec(1), col_spec(2), col_spec(3), act_spec, act_spec,
                  par_spec, par_spec, par_spec, par_spec, par_spec],
        out_specs=act_spec,
        out_shape=jax.ShapeDtypeStruct((m, w), BF16),
        scratch_shapes=[pltpu.VMEM((group, 2 * two, LANES), BF16),
                        pltpu.VMEM((group, CHUNK, LANES), F32),
                        pltpu.VMEM((group, two, LANES), F32),
                        pltpu.VMEM((group, two, LANES), F32),
                        pltpu.VMEM((seq, LANES), F32)],
        compiler_params=_cparams(("parallel", "parallel")),
        name="wkv7",
    )(proj, proj, proj, proj, lw, ic, kk, ka, rk, lnw, lnb)


def _attn_kernel(q_ref, k_ref, v_ref, g_ref, lam_ref, sw_ref, o_ref,
                 z_a, z_b, p_a, p_b, acc_sc, ka1_sc, ka2_sc, q1_sc, q2_sc, *, tq, n_heads, lambda_init):
    seq = q_ref.shape[0]
    nq = seq // tq
    tk = tq
    half = 256
    ng = 2 * tq // LANES
    h = pl.program_id(1)
    lane = lax.broadcasted_iota(jnp.int32, (1, LANES), 1)
    slope = jnp.exp2(jnp.full((1, LANES), -8.0 / n_heads, F32) * (h + 1).astype(F32)) * LOG2E
    s_hi = slope.astype(BF16).astype(F32)
    s_lo = slope - s_hi
    s_pat = jnp.where(lane % 2 == 0, s_hi, s_lo)

    row_s = lax.broadcasted_iota(jnp.int32, (seq, LANES), 0) % tk
    lane_s = lax.broadcasted_iota(jnp.int32, (seq, LANES), 1)
    kidx = jnp.where(lane_s % 4 < 2, row_s % half, (row_s // half) * half).astype(F32)
    ks_all = k_ref[...].astype(F32)
    ka1_sc[...] = jnp.where(lane_s < HEAD, ks_all, jnp.where(lane_s < HEAD + 4, kidx, 0.0)).astype(BF16)
    ka2_sc[...] = jnp.where(lane_s >= HEAD, ks_all, jnp.where(lane_s < 4, kidx, 0.0)).astype(BF16)

    tri = (lax.broadcasted_iota(jnp.int32, (LANES, LANES), 0) <= lax.broadcasted_iota(jnp.int32, (LANES, LANES), 1))
    lam = lam_ref[...]
    lam_full = (jnp.exp(jnp.sum(lam[0:1] * lam[1:2], axis=1, keepdims=True))
                - jnp.exp(jnp.sum(lam[2:3] * lam[3:4], axis=1, keepdims=True)) + lambda_init)
    sw_scaled = sw_ref[...] * (1.0 - lambda_init)

    def load_q(qi):
        qf = q_ref[qi * tq:(qi + 1) * tq, :].astype(F32) * (LOG2E / math.sqrt(HEAD))
        q1_sc[...] = jnp.where(lane < HEAD, qf, jnp.where(lane < HEAD + 4, s_pat, 0.0)).astype(BF16)
        q2_sc[...] = jnp.where(lane >= HEAD, qf, jnp.where(lane < 4, s_pat, 0.0)).astype(BF16)

    def scores_to(j, z_ref):
        z_ref[:, :tq] = _dot_nt(ka1_sc[j * tk:(j + 1) * tk, :], q1_sc[...])
        z_ref[:, tq:] = _dot_nt(ka2_sc[j * tk:(j + 1) * tk, :], q2_sc[...])

    def accumulate(j, p_ref, alpha, first):
        pv = _dot_tn(v_ref[j * tk:(j + 1) * tk, :], p_ref[...])
        acc_sc[...] = pv if first else alpha * acc_sc[...] + pv

    def softmax(qi, j, z_ref, p_ref, m, l):
        c_blk = slope * float((j - qi) * tq)
        ms, ls, alphas = [], [], []
        for g in range(ng):
            sl = slice(g * LANES, (g + 1) * LANES)
            nrows = tk
            if j == qi:
                nrows = (g % (ng // 2)) * LANES + LANES
                if nrows < tk:
                    p_ref[nrows:, sl] = jnp.zeros((tk - nrows, LANES), BF16)
            zt = z_ref[0:nrows, sl]
            if j == qi:
                tail = jnp.where(tri, zt[nrows - LANES:], NEG_BIG)
                zt = tail if nrows == LANES else jnp.concatenate([zt[:nrows - LANES], tail], axis=0)
            mg = m[:, sl]
            mn = jnp.maximum(mg, jnp.max(zt, axis=0, keepdims=True) + c_blk)
            alpha = jnp.exp2(mg - mn)
            pt = jnp.exp2(zt - (mn - c_blk))
            ls.append(alpha * l[:, sl] + jnp.sum(pt, axis=0, keepdims=True))
            p_ref[0:nrows, sl] = pt.astype(BF16)
            ms.append(mn)
            alphas.append(alpha)
        return jnp.concatenate(ms, axis=1), jnp.concatenate(ls, axis=1), jnp.concatenate(alphas, axis=1)

    def finalize(qi, l):
        on = acc_sc[...] / l
        ot = on[:, :tq] - lam_full * on[:, tq:]
        ot = ot * lax.rsqrt(jnp.mean(ot * ot, axis=0, keepdims=True) + SUBLN_EPS)
        g = g_ref[qi * tq:(qi + 1) * tq, :].astype(F32)
        o_ref[qi * tq:(qi + 1) * tq, :] = (ot.T * sw_scaled * (g / (1.0 + jnp.exp(-g)))).astype(o_ref.dtype)

    items = [(qi, j) for qi in range(nq) for j in range(qi + 1)]
    zs, ps = (z_a, z_b), (p_a, p_b)
    load_q(0)
    scores_to(0, z_a)
    m = l = alpha = None
    for n, (qi, j) in enumerate(items):
        if n > 0:
            pqi, pj = items[n - 1]
            accumulate(pj, ps[(n - 1) % 2], alpha, pj == 0)
            if pqi != qi:
                finalize(pqi, l)
        if j == 0:
            m = jnp.full((1, 2 * tq), NEG_BIG, F32)
            l = jnp.zeros((1, 2 * tq), F32)
        if n + 1 < len(items):
            nqi, nj = items[n + 1]
            if nqi != qi:
                load_q(nqi)
            scores_to(nj, zs[(n + 1) % 2])
        m, l, alpha = softmax(qi, j, zs[n % 2], ps[n % 2], m, l)
    pqi, pj = items[-1]
    accumulate(pj, ps[(len(items) - 1) % 2], alpha, pj == 0)
    finalize(pqi, l)


def _diff_attn(qkvg, lam, sw, *, batch, seq, tq, lambda_init):
    m, n4 = qkvg.shape
    w = n4 // 4
    nh = w // LANES

    def col_spec(p):
        return pl.BlockSpec((seq, LANES), lambda b, h: (b, p * nh + h))

    return pl.pallas_call(
        functools.partial(_attn_kernel, tq=tq, n_heads=nh, lambda_init=lambda_init),
        grid=(batch, nh),
        in_specs=[col_spec(0), col_spec(1), col_spec(2), col_spec(3),
                  pl.BlockSpec(lam.shape, lambda b, h: (0, 0)),
                  pl.BlockSpec((1, LANES), lambda b, h: (0, 0))],
        out_specs=pl.BlockSpec((seq, LANES), lambda b, h: (b, h)),
        out_shape=jax.ShapeDtypeStruct((m, w), BF16),
        scratch_shapes=[pltpu.VMEM((tq, 2 * tq), F32), pltpu.VMEM((tq, 2 * tq), F32),
                        pltpu.VMEM((tq, 2 * tq), BF16), pltpu.VMEM((tq, 2 * tq), BF16),
                        pltpu.VMEM((LANES, 2 * tq), F32),
                        pltpu.VMEM((seq, LANES), BF16), pltpu.VMEM((seq, LANES), BF16),
                        pltpu.VMEM((tq, LANES), BF16), pltpu.VMEM((tq, LANES), BF16)],
        compiler_params=_cparams(("parallel", "parallel")),
        name="diff_attn",
    )(qkvg, qkvg, qkvg, qkvg, lam, sw)


def kernel(x, norm_w, rwkv_mu, rwkv_w_in, rwkv_w0, rwkv_w1, rwkv_w2, rwkv_a0, rwkv_a1, rwkv_a2, rwkv_k_k, rwkv_k_a, rwkv_r_k, rwkv_ln_w, rwkv_ln_b, rwkv_w_out, diff_w_in, diff_lambda, diff_subln_w, diff_w_out, final_norm_w):
    batch, seq, d = x.shape
    m = batch * seq
    assert d % LANES == 0 and seq % CHUNK == 0
    assert norm_w.shape[0] == 2 and rwkv_mu.shape[0] == 1 and diff_w_in.shape[0] == 1
    tm_big = min(1024, m)
    tn_big = min(1024, d)
    tm_prep = min(256, seq)
    tm_last = min(512, m)
    tq = min(512, seq)

    x2 = x.reshape(m, d)
    row = lambda a: a.reshape(1, -1)
    lora = rwkv_w1.shape[-1]
    pad_c = lambda a: jnp.pad(a, ((0, 0), (0, LORA_PAD - lora))).astype(BF16)
    pad_r = lambda a: jnp.pad(a, ((0, LORA_PAD - lora), (0, 0))).astype(BF16)

    xs, lw, ic = _rwkv_prep(x2, row(norm_w[0]), rwkv_mu[0], row(rwkv_w0[0]), pad_c(rwkv_w1[0]), pad_r(rwkv_w2[0]),
                            row(rwkv_a0[0]), pad_c(rwkv_a1[0]), pad_r(rwkv_a2[0]),
                            batch=batch, seq=seq, tm=tm_prep)
    proj = _proj4(xs, rwkv_w_in[0].astype(BF16), tm=tm_big, tn=tn_big, out_dtype=F32)
    yg = _wkv(proj, lw, ic, row(rwkv_k_k[0]), row(rwkv_k_a[0]), row(rwkv_r_k[0]), row(rwkv_ln_w[0]),
              row(rwkv_ln_b[0]), batch=batch, seq=seq)
    x1 = _matmul_residual(yg, rwkv_w_out[0].astype(BF16), x2, tm=tm_big, tn=tn_big)

    lambda_init = 0.8 - 0.6 * math.exp(-0.3 * 1)
    h1 = _rms_norm(x1, row(norm_w[1]), tm=tm_last, out_dtype=BF16)
    qkvg = _matmul(h1, diff_w_in[0].astype(BF16), tm=tm_big, tn=tn_big, out_dtype=BF16)
    att = _diff_attn(qkvg, diff_lambda[0], row(diff_subln_w[0]), batch=batch, seq=seq, tq=tq,
                     lambda_init=lambda_init)
    out = _matmul_residual_norm(att, diff_w_out[0].astype(BF16), x1, row(final_norm_w), tm=tm_last)
    return out.reshape(batch, seq, d)
```

```python
import functools
import math

import jax
import jax.numpy as jnp
from jax import lax
from jax.experimental import pallas as pl
from jax.experimental.pallas import tpu as pltpu

F32 = jnp.float32
BF16 = jnp.bfloat16

NORM_EPS = 1e-6
GN_EPS = 64e-5
SUBLN_EPS = 1e-5
HEAD = 64
LANES = 128
CHUNK = 64
LORA_PAD = 128
VMEM_LIMIT = 48 * 1024 * 1024
NEG_BIG = -1e30
LOG2E = 1.0 / math.log(2.0)


def _cparams(sem):
    return pltpu.CompilerParams(dimension_semantics=sem, vmem_limit_bytes=VMEM_LIMIT)


def _dot(a, b):
    return jnp.dot(a, b, preferred_element_type=F32)


def _dot_nt(a, b):
    return lax.dot_general(a, b, (((1,), (1,)), ((), ())), preferred_element_type=F32)


def _dot_tn(a, b):
    return lax.dot_general(a, b, (((0,), (0,)), ((), ())), preferred_element_type=F32)


def _split2(x):
    hi = x.astype(BF16)
    lo = (x - hi.astype(F32)).astype(BF16)
    return hi, lo


def _mm(a, b, passes, nt=False):
    d = _dot_nt if nt else _dot
    if passes == 1:
        return d(a.astype(BF16), b.astype(BF16))
    ah, al = _split2(a)
    bh, bl = _split2(b)
    return d(ah, bh) + (d(al, bh) + d(ah, bl))


def _mm_exact_rhs(a, e_bf16, parts):
    acc = None
    rem = a
    for _ in range(parts):
        piece = rem.astype(BF16)
        term = _dot(piece, e_bf16)
        acc = term if acc is None else acc + term
        rem = rem - piece.astype(F32)
    return acc


def _mm_exact_lhs(e_bf16, a, parts):
    acc = None
    rem = a
    for _ in range(parts):
        piece = rem.astype(BF16)
        term = _dot(e_bf16, piece)
        acc = term if acc is None else acc + term
        rem = rem - piece.astype(F32)
    return acc


def _rwkv_prep_kernel(x_ref, nw_ref, mu_ref, w0_ref, w1_ref, w2_ref, a0_ref, a1_ref, a2_ref,
                      xs_ref, lw_ref, ic_ref, carry_ref):
    t = pl.program_id(1)
    x = x_ref[...]
    tm = x.shape[0]
    h = x * lax.rsqrt(jnp.mean(x * x, axis=-1, keepdims=True) + NORM_EPS) * nw_ref[...]

    @pl.when(t == 0)
    def _():
        carry_ref[...] = jnp.zeros_like(carry_ref)

    prev_last = carry_ref[0:1, :]
    rolled = pltpu.roll(h, 1, axis=0)
    row = lax.broadcasted_iota(jnp.int32, h.shape, 0)
    h_prev = jnp.where(row == 0, prev_last, rolled)
    carry_ref[0:1, :] = h[tm - 1:tm, :]
    d = h_prev - h
    mu = mu_ref[...]
    xs_ref[0] = (h + d * mu[0:1]).astype(BF16)
    xs_ref[1] = (h + d * mu[2:3]).astype(BF16)
    xs_ref[2] = (h + d * mu[3:4]).astype(BF16)
    xs_ref[3] = (h + d * mu[5:6]).astype(BF16)
    xw = (h + d * mu[1:2]).astype(BF16)
    xa = (h + d * mu[4:5]).astype(BF16)
    zw = w0_ref[...] + _dot(jnp.tanh(_dot(xw, w1_ref[...])).astype(BF16), w2_ref[...])
    lw_ref[...] = (-math.exp(-0.5)) / (1.0 + jnp.exp(-zw))
    za = a0_ref[...] + _dot(_dot(xa, a1_ref[...]).astype(BF16), a2_ref[...])
    ic_ref[...] = 1.0 / (1.0 + jnp.exp(-za))


def _rwkv_prep(x2, nw, mu, w0, w1p, w2p, a0, a1p, a2p, *, batch, seq, tm):
    m, d = x2.shape
    nt = seq // tm
    row_spec = pl.BlockSpec((tm, d), lambda b, t: (b * nt + t, 0))
    vec_spec = pl.BlockSpec((1, d), lambda b, t: (0, 0))

    def full(a):
        return pl.BlockSpec(a.shape, lambda b, t: (0,) * a.ndim)

    return pl.pallas_call(
        _rwkv_prep_kernel,
        grid=(batch, nt),
        in_specs=[row_spec, vec_spec, full(mu), vec_spec, full(w1p), full(w2p), vec_spec, full(a1p), full(a2p)],
        out_specs=[pl.BlockSpec((4, tm, d), lambda b, t: (0, b * nt + t, 0)), row_spec, row_spec],
        out_shape=[jax.ShapeDtypeStruct((4, m, d), BF16),
                   jax.ShapeDtypeStruct((m, d), F32),
                   jax.ShapeDtypeStruct((m, d), F32)],
        scratch_shapes=[pltpu.VMEM((8, d), F32)],
        compiler_params=_cparams(("parallel", "arbitrary")),
        name="rwkv_prep",
    )(x2, nw, mu, w0, w1p, w2p, a0, a1p, a2p)


def _mm_kernel(a_ref, b_ref, o_ref):
    o_ref[...] = _dot(a_ref[...], b_ref[...]).astype(o_ref.dtype)


def _proj4(xs, w, *, tm, tn, out_dtype):
    _, m, k = xs.shape
    n = w.shape[1]
    per_path = (n // 4) // tn
    return pl.pallas_call(
        _mm_kernel,
        grid=(m // tm, n // tn),
        in_specs=[pl.BlockSpec((None, tm, k), lambda i, j: (j // per_path, i, 0)),
                  pl.BlockSpec((k, tn), lambda i, j: (0, j))],
        out_specs=pl.BlockSpec((tm, tn), lambda i, j: (i, j)),
        out_shape=jax.ShapeDtypeStruct((m, n), out_dtype),
        compiler_params=_cparams(("parallel", "arbitrary")),
        name="proj4",
    )(xs, w)


def _matmul(a, w, *, tm, tn, out_dtype):
    m, k = a.shape
    n = w.shape[1]
    return pl.pallas_call(
        _mm_kernel,
        grid=(m // tm, n // tn),
        in_specs=[pl.BlockSpec((tm, k), lambda i, j: (i, 0)),
                  pl.BlockSpec((k, tn), lambda i, j: (0, j))],
        out_specs=pl.BlockSpec((tm, tn), lambda i, j: (i, j)),
        out_shape=jax.ShapeDtypeStruct((m, n), out_dtype),
        compiler_params=_cparams(("parallel", "arbitrary")),
        name="matmul",
    )(a, w)


def _mm_res_kernel(a_ref, b_ref, r_ref, o_ref):
    o_ref[...] = r_ref[...] + _dot(a_ref[...], b_ref[...])


def _matmul_residual(a, w, res, *, tm, tn):
    m, k = a.shape
    n = w.shape[1]
    return pl.pallas_call(
        _mm_res_kernel,
        grid=(m // tm, n // tn),
        in_specs=[pl.BlockSpec((tm, k), lambda i, j: (i, 0)),
                  pl.BlockSpec((k, tn), lambda i, j: (0, j)),
                  pl.BlockSpec((tm, tn), lambda i, j: (i, j))],
        out_specs=pl.BlockSpec((tm, tn), lambda i, j: (i, j)),
        out_shape=jax.ShapeDtypeStruct((m, n), F32),
        compiler_params=_cparams(("parallel", "arbitrary")),
        name="matmul_residual",
    )(a, w, res)


def _mm_res_norm_kernel(a_ref, b_ref, r_ref, nw_ref, o_ref):
    y = r_ref[...] + _dot(a_ref[...], b_ref[...])
    o_ref[...] = y * lax.rsqrt(jnp.mean(y * y, axis=-1, keepdims=True) + NORM_EPS) * nw_ref[...]


def _matmul_residual_norm(a, w, res, nw, *, tm):
    m, k = a.shape
    n = w.shape[1]
    return pl.pallas_call(
        _mm_res_norm_kernel,
        grid=(m // tm,),
        in_specs=[pl.BlockSpec((tm, k), lambda i: (i, 0)),
                  pl.BlockSpec((k, n), lambda i: (0, 0)),
                  pl.BlockSpec((tm, n), lambda i: (i, 0)),
                  pl.BlockSpec((1, n), lambda i: (0, 0))],
        out_specs=pl.BlockSpec((tm, n), lambda i: (i, 0)),
        out_shape=jax.ShapeDtypeStruct((m, n), F32),
        compiler_params=_cparams(("parallel",)),
        name="matmul_residual_norm",
    )(a, w, res, nw)


def _norm_kernel(x_ref, nw_ref, o_ref):
    x = x_ref[...]
    o_ref[...] = (x * lax.rsqrt(jnp.mean(x * x, axis=-1, keepdims=True) + NORM_EPS) * nw_ref[...]).astype(o_ref.dtype)


def _rms_norm(x2, nw, *, tm, out_dtype):
    m, d = x2.shape
    return pl.pallas_call(
        _norm_kernel,
        grid=(m // tm,),
        in_specs=[pl.BlockSpec((tm, d), lambda i: (i, 0)), pl.BlockSpec((1, d), lambda i: (0, 0))],
        out_specs=pl.BlockSpec((tm, d), lambda i: (i, 0)),
        out_shape=jax.ShapeDtypeStruct((m, d), out_dtype),
        compiler_params=_cparams(("parallel",)),
        name="rms_norm",
    )(x2, nw)


WKV_GROUP = 8


def _wkv_kernel(r_ref, k_ref, v_ref, g_ref, lw_ref, ic_ref, kk_ref, ka_ref, rk_ref, lnw_ref, lnb_ref,
                o_ref, rm_sc, y0_sc, g_sc, wc_sc, y_sc, *, group):
    seq = r_ref.shape[0]
    ngroups = seq // (CHUNK * group)
    two = 2 * CHUNK

    lane1 = lax.broadcasted_iota(jnp.int32, (1, LANES), 1)
    m0 = (lane1 < HEAD).astype(F32)
    m1 = 1.0 - m0
    row = lax.broadcasted_iota(jnp.int32, (two, two), 0)
    col = lax.broadcasted_iota(jnp.int32, (two, two), 1)
    strict = row > col
    incl = row >= col
    eye = (row == col).astype(F32)
    ones_bd = ((row < HEAD) == (col < HEAD)).astype(BF16)
    tril_c = incl[:CHUNK, :CHUNK].astype(BF16)
    zeros_b = jnp.zeros((two, LANES), BF16)

    kk_p = kk_ref[...]
    ka_p = ka_ref[...]

    def segsum(z):
        return _mm_exact_rhs(z, ones_bd, 2)

    def stack(z):
        return jnp.concatenate([z * m0, z * m1], axis=0)

    def chunk_rows(c):
        return pl.ds(pl.multiple_of(c * CHUNK, CHUNK), CHUNK)

    def phase_a(gi):
        us = range(group)
        rows = [chunk_rows(gi * group + u) for u in us]
        k = [k_ref[rw, :] for rw in rows]
        lw = [lw_ref[rw, :] for rw in rows]
        kkv = [k[u] * kk_p for u in us]
        n2 = [segsum(kkv[u] * kkv[u]) for u in us]
        lin = [_mm_exact_lhs(tril_c, lw[u], 2) for u in us]
        yield
        a_s, r_s, v_s, bkt, wcb, a_all = [], [], [], [], [], []
        for u in us:
            ic = ic_ref[rows[u], :]
            kkn = kkv[u] / jnp.maximum(jnp.sqrt(n2[u]), 1e-12)
            kh = k[u] * (1.0 + (ic - 1.0) * ka_p)
            e_in = jnp.exp(lin[u])
            e_ex = jnp.exp(lin[u] - lw[u])
            e_neg = jnp.exp(-lin[u])
            b_f = stack(kkn * ic * e_neg)
            k_f = stack(kh * e_neg)
            a_s.append(stack(-kkn * e_ex).astype(BF16))
            r_s.append(stack(r_ref[rows[u], :] * e_in).astype(BF16))
            v_s.append(stack(v_ref[rows[u], :]).astype(BF16))
            a_all.append(_dot_nt(jnp.concatenate([a_s[u], r_s[u]], axis=0),
                                 jnp.concatenate([b_f, k_f], axis=0).astype(BF16)))
            bkt.append(jnp.concatenate([b_f.T, k_f.T], axis=1).astype(BF16))
            wcol = jnp.sum(eye * e_in[CHUNK - 1:CHUNK, :], axis=1, keepdims=True)
            wcb.append(jnp.broadcast_to(wcol, (two, LANES)))
        yield
        t_inv, pwb, a_rb, av = [], [], [], []
        for u in us:
            a_ab = jnp.where(strict, a_all[u][:two, :two], 0.0)
            a_ak = jnp.where(strict, a_all[u][:two, two:], 0.0).astype(BF16)
            a_rk = jnp.where(incl, a_all[u][two:, two:], 0.0).astype(BF16)
            a_rb.append(jnp.where(incl, a_all[u][two:, :two], 0.0).astype(BF16))
            av.append(_dot(jnp.concatenate([a_ak, a_rk], axis=0), v_s[u]))
            t_inv.append(eye + a_ab)
            pwb.append(a_ab.astype(BF16))
        yield
        pwb = [_dot(pwb[u], pwb[u]).astype(BF16) for u in us]
        yield
        for _ in range(4):
            both = [_dot(jnp.concatenate([t_inv[u].astype(BF16), pwb[u]], axis=0), pwb[u]) for u in us]
            t_inv = [t_inv[u] + both[u][:two] for u in us]
            pwb = [both[u][two:].astype(BF16) for u in us]
            yield
        t_inv = [t_inv[u] + _dot(t_inv[u].astype(BF16), pwb[u]) for u in us]
        yield
        pqb = [_dot(t_inv[u].astype(BF16),
                    jnp.concatenate([a_s[u], av[u][:two].astype(BF16)], axis=1)).astype(BF16) for u in us]
        yield
        rq = [_dot(a_rb[u], pqb[u]) for u in us]
        mg = [_dot(bkt[u], jnp.concatenate([pqb[u], jnp.concatenate([zeros_b, v_s[u]], axis=1)], axis=0)) for u in us]
        yield
        for u in us:
            y0 = rq[u][:, LANES:] + av[u][two:]
            rm_sc[u] = jnp.concatenate([r_s[u].astype(F32) + rq[u][:, :LANES], wcb[u] * mg[u][:, :LANES]],
                                       axis=0).astype(BF16)
            g_sc[u] = wcb[u] * mg[u][:, LANES:]
            y0_sc[u] = y0[:CHUNK] + y0[CHUNK:]
            wc_sc[u] = wcb[u]

    def state_step(gi, u, h):
        res = _dot(rm_sc[u], h.astype(BF16))
        y_sc[chunk_rows(gi * group + u), :] = res[:CHUNK] + res[CHUNK:two] + y0_sc[u]
        return wc_sc[u] * h + res[two:] + g_sc[u]

    rk_p = rk_ref[...]
    lnw_p = lnw_ref[...]
    lnb_p = lnb_ref[...]
    inv_n = 1.0 / HEAD
    grows = group * CHUNK

    def finish(gi):
        rows = pl.ds(pl.multiple_of(gi * grows, grows), grows)
        y = y_sc[rows, :]
        mean = segsum(y) * inv_n
        yc = y - mean
        var = segsum(yc * yc) * inv_n
        gn = yc * lax.rsqrt(var + GN_EPS) * lnw_p + lnb_p
        kh = k_ref[rows, :] * (1.0 + (ic_ref[rows, :] - 1.0) * ka_p)
        bonus = segsum(r_ref[rows, :] * kh * rk_p) * v_ref[rows, :]
        g = g_ref[rows, :]
        o_ref[rows, :] = ((gn + bonus) * (g / (1.0 + jnp.exp(-g)))).astype(o_ref.dtype)

    def run(a_gi, b_gi, c_gi, h):
        done = 0
        if a_gi is None:
            for u in range(group):
                h = state_step(b_gi, u, h)
        else:
            for i, _ in enumerate(phase_a(a_gi)):
                if b_gi is not None and done < group and i >= 1:
                    h = state_step(b_gi, done, h)
                    done += 1
            assert b_gi is None or done == group
        if c_gi is not None:
            finish(c_gi)
        return h

    run(0, None, None, None)
    h = run(1, 0, None, jnp.zeros((two, LANES), F32))
    h = lax.fori_loop(1, ngroups - 1, lambda gi, h: run(gi + 1, gi, gi - 1, h), h)
    run(None, ngroups - 1, ngroups - 2, h)
    finish(ngroups - 1)


def _wkv(proj, lw, ic, kk, ka, rk, lnw, lnb, *, batch, seq):
    m, n4 = proj.shape
    w = n4 // 4
    nhp = w // LANES
    two = 2 * CHUNK
    group = WKV_GROUP
    assert seq % (CHUNK * group) == 0 and seq // (CHUNK * group) >= 3

    def col_spec(p):
        return pl.BlockSpec((seq, LANES), lambda b, hp: (b, p * nhp + hp))

    par_spec = pl.BlockSpec((1, LANES), lambda b, hp: (0, hp))
    act_spec = pl.BlockSpec((seq, LANES), lambda b, hp: (b, hp))
    return pl.pallas_call(
        functools.partial(_wkv_kernel, group=group),
        grid=(batch, nhp),
        in_specs=[col_spec(0), col_spec(1), col_spec(2), col_spec(3), act_spec, act_spec,
                  par_spec, par_spec, par_spec, par_spec, par_spec],
        out_specs=act_spec,
        out_shape=jax.ShapeDtypeStruct((m, w), BF16),
        scratch_shapes=[pltpu.VMEM((group, 2 * two, LANES), BF16),
                        pltpu.VMEM((group, CHUNK, LANES), F32),
                        pltpu.VMEM((group, two, LANES), F32),
                        pltpu.VMEM((group, two, LANES), F32),
                        pltpu.VMEM((seq, LANES), F32)],
        compiler_params=_cparams(("parallel", "parallel")),
        name="wkv7",
    )(proj, proj, proj, proj, lw, ic, kk, ka, rk, lnw, lnb)


def _attn_kernel(q_ref, k_ref, v_ref, g_ref, lam_ref, sw_ref, o_ref,
                 z_a, z_b, p_a, p_b, acc_sc, ka1_sc, ka2_sc, q1_sc, q2_sc, *, tq, n_heads, lambda_init):
    seq = q_ref.shape[0]
    nq = seq // tq
    tk = tq
    half = 256
    ng = 2 * tq // LANES
    h = pl.program_id(1)
    lane = lax.broadcasted_iota(jnp.int32, (1, LANES), 1)
    slope = jnp.exp2(jnp.full((1, LANES), -8.0 / n_heads, F32) * (h + 1).astype(F32)) * LOG2E
    s_hi = slope.astype(BF16).astype(F32)
    s_lo = slope - s_hi
    s_pat = jnp.where(lane % 2 == 0, s_hi, s_lo)

    row_s = lax.broadcasted_iota(jnp.int32, (seq, LANES), 0) % tk
    lane_s = lax.broadcasted_iota(jnp.int32, (seq, LANES), 1)
    kidx = jnp.where(lane_s % 4 < 2, row_s % half, (row_s // half) * half).astype(F32)
    ks_all = k_ref[...].astype(F32)
    ka1_sc[...] = jnp.where(lane_s < HEAD, ks_all, jnp.where(lane_s < HEAD + 4, kidx, 0.0)).astype(BF16)
    ka2_sc[...] = jnp.where(lane_s >= HEAD, ks_all, jnp.where(lane_s < 4, kidx, 0.0)).astype(BF16)

    tri = (lax.broadcasted_iota(jnp.int32, (LANES, LANES), 0) <= lax.broadcasted_iota(jnp.int32, (LANES, LANES), 1))
    lam = lam_ref[...]
    lam_full = (jnp.exp(jnp.sum(lam[0:1] * lam[1:2], axis=1, keepdims=True))
                - jnp.exp(jnp.sum(lam[2:3] * lam[3:4], axis=1, keepdims=True)) + lambda_init)
    sw_scaled = sw_ref[...] * (1.0 - lambda_init)

    def load_q(qi):
        qf = q_ref[qi * tq:(qi + 1) * tq, :].astype(F32) * (LOG2E / math.sqrt(HEAD))
        q1_sc[...] = jnp.where(lane < HEAD, qf, jnp.where(lane < HEAD + 4, s_pat, 0.0)).astype(BF16)
        q2_sc[...] = jnp.where(lane >= HEAD, qf, jnp.where(lane < 4, s_pat, 0.0)).astype(BF16)

    def scores_to(j, z_ref):
        z_ref[:, :tq] = _dot_nt(ka1_sc[j * tk:(j + 1) * tk, :], q1_sc[...])
        z_ref[:, tq:] = _dot_nt(ka2_sc[j * tk:(j + 1) * tk, :], q2_sc[...])

    def accumulate(j, p_ref, alpha, first):
        pv = _dot_tn(v_ref[j * tk:(j + 1) * tk, :], p_ref[...])
        acc_sc[...] = pv if first else alpha * acc_sc[...] + pv

    def softmax(qi, j, z_ref, p_ref, m, l):
        c_blk = slope * float((j - qi) * tq)
        ms, ls, alphas = [], [], []
        for g in range(ng):
            sl = slice(g * LANES, (g + 1) * LANES)
            nrows = tk
            if j == qi:
                nrows = (g % (ng // 2)) * LANES + LANES
                if nrows < tk:
                    p_ref[nrows:, sl] = jnp.zeros((tk - nrows, LANES), BF16)
            zt = z_ref[0:nrows, sl]
            if j == qi:
                tail = jnp.where(tri, zt[nrows - LANES:], NEG_BIG)
                zt = tail if nrows == LANES else jnp.concatenate([zt[:nrows - LANES], tail], axis=0)
            mg = m[:, sl]
            mn = jnp.maximum(mg, jnp.max(zt, axis=0, keepdims=True) + c_blk)
            alpha = jnp.exp2(mg - mn)
            pt = jnp.exp2(zt - (mn - c_blk))
            ls.append(alpha * l[:, sl] + jnp.sum(pt, axis=0, keepdims=True))
            p_ref[0:nrows, sl] = pt.astype(BF16)
            ms.append(mn)
            alphas.append(alpha)
        return jnp.concatenate(ms, axis=1), jnp.concatenate(ls, axis=1), jnp.concatenate(alphas, axis=1)

    def finalize(qi, l):
        on = acc_sc[...] / l
        ot = on[:, :tq] - lam_full * on[:, tq:]
        ot = ot * lax.rsqrt(jnp.mean(ot * ot, axis=0, keepdims=True) + SUBLN_EPS)
        g = g_ref[qi * tq:(qi + 1) * tq, :].astype(F32)
        o_ref[qi * tq:(qi + 1) * tq, :] = (ot.T * sw_scaled * (g / (1.0 + jnp.exp(-g)))).astype(o_ref.dtype)

    items = [(qi, j) for qi in range(nq) for j in range(qi + 1)]
    zs, ps = (z_a, z_b), (p_a, p_b)
    load_q(0)
    scores_to(0, z_a)
    m = l = alpha = None
    for n, (qi, j) in enumerate(items):
        if n > 0:
            pqi, pj = items[n - 1]
            accumulate(pj, ps[(n - 1) % 2], alpha, pj == 0)
            if pqi != qi:
                finalize(pqi, l)
        if j == 0:
            m = jnp.full((1, 2 * tq), NEG_BIG, F32)
            l = jnp.zeros((1, 2 * tq), F32)
        if n + 1 < len(items):
            nqi, nj = items[n + 1]
            if nqi != qi:
                load_q(nqi)
            scores_to(nj, zs[(n + 1) % 2])
        m, l, alpha = softmax(qi, j, zs[n % 2], ps[n % 2], m, l)
    pqi, pj = items[-1]
    accumulate(pj, ps[(len(items) - 1) % 2], alpha, pj == 0)
    finalize(pqi, l)


def _diff_attn(qkvg, lam, sw, *, batch, seq, tq, lambda_init):
    m, n4 = qkvg.shape
    w = n4 // 4
    nh = w // LANES

    def col_spec(p):
        return pl.BlockSpec((seq, LANES), lambda b, h: (b, p * nh + h))

    return pl.pallas_call(
        functools.partial(_attn_kernel, tq=tq, n_heads=nh, lambda_init=lambda_init),
        grid=(batch, nh),
        in_specs=[col_spec(0), col_spec(1), col_spec(2), col_spec(3),
                  pl.BlockSpec(lam.shape, lambda b, h: (0, 0)),
                  pl.BlockSpec((1, LANES), lambda b, h: (0, 0))],
        out_specs=pl.BlockSpec((seq, LANES), lambda b, h: (b, h)),
        out_shape=jax.ShapeDtypeStruct((m, w), BF16),
        scratch_shapes=[pltpu.VMEM((tq, 2 * tq), F32), pltpu.VMEM((tq, 2 * tq), F32),
                        pltpu.VMEM((tq, 2 * tq), BF16), pltpu.VMEM((tq, 2 * tq), BF16),
                        pltpu.VMEM((LANES, 2 * tq), F32),
                        pltpu.VMEM((seq, LANES), BF16), pltpu.VMEM((seq, LANES), BF16),
                        pltpu.VMEM((tq, LANES), BF16), pltpu.VMEM((tq, LANES), BF16)],
        compiler_params=_cparams(("parallel", "parallel")),
        name="diff_attn",
    )(qkvg, qkvg, qkvg, qkvg, lam, sw)


def kernel(x, norm_w, rwkv_mu, rwkv_w_in, rwkv_w0, rwkv_w1, rwkv_w2, rwkv_a0, rwkv_a1, rwkv_a2, rwkv_k_k, rwkv_k_a, rwkv_r_k, rwkv_ln_w, rwkv_ln_b, rwkv_w_out, diff_w_in, diff_lambda, diff_subln_w, diff_w_out, final_norm_w):
    batch, seq, d = x.shape
    m = batch * seq
    assert d % LANES == 0 and seq % CHUNK == 0
    assert norm_w.shape[0] == 2 and rwkv_mu.shape[0] == 1 and diff_w_in.shape[0] == 1
    tm_big = min(1024, m)
    tn_big = min(1024, d)
    tm_prep = min(256, seq)
    tm_last = min(512, m)
    tq = min(512, seq)

    x2 = x.reshape(m, d)
    row = lambda a: a.reshape(1, -1)
    lora = rwkv_w1.shape[-1]
    pad_c = lambda a: jnp.pad(a, ((0, 0), (0, LORA_PAD - lora))).astype(BF16)
    pad_r = lambda a: jnp.pad(a, ((0, LORA_PAD - lora), (0, 0))).astype(BF16)

    xs, lw, ic = _rwkv_prep(x2, row(norm_w[0]), rwkv_mu[0], row(rwkv_w0[0]), pad_c(rwkv_w1[0]), pad_r(rwkv_w2[0]),
                            row(rwkv_a0[0]), pad_c(rwkv_a1[0]), pad_r(rwkv_a2[0]),
                            batch=batch, seq=seq, tm=tm_prep)
    proj = _proj4(xs, rwkv_w_in[0].astype(BF16), tm=tm_big, tn=tn_big, out_dtype=F32)
    yg = _wkv(proj, lw, ic, row(rwkv_k_k[0]), row(rwkv_k_a[0]), row(rwkv_r_k[0]), row(rwkv_ln_w[0]),
              row(rwkv_ln_b[0]), batch=batch, seq=seq)
    x1 = _matmul_residual(yg, rwkv_w_out[0].astype(BF16), x2, tm=tm_big, tn=tn_big)

    lambda_init = 0.8 - 0.6 * math.exp(-0.3 * 1)
    h1 = _rms_norm(x1, row(norm_w[1]), tm=tm_last, out_dtype=BF16)
    qkvg = _matmul(h1, diff_w_in[0].astype(BF16), tm=tm_big, tn=tn_big, out_dtype=BF16)
    att = _diff_attn(qkvg, diff_lambda[0], row(diff_subln_w[0]), batch=batch, seq=seq, tq=tq,
                     lambda_init=lambda_init)
    out = _matmul_residual_norm(att, diff_w_out[0].astype(BF16), x1, row(final_norm_w), tm=tm_last)
    return out.reshape(batch, seq, d)
```

```python
import functools
import math

import jax
import jax.numpy as jnp
from jax import lax
from jax.experimental import pallas as pl
from jax.experimental.pallas import tpu as pltpu

F32 = jnp.float32
BF16 = jnp.bfloat16

NORM_EPS = 1e-6
GN_EPS = 64e-5
SUBLN_EPS = 1e-5
HEAD = 64
LANES = 128
CHUNK = 64
LORA_PAD = 128
VMEM_LIMIT = 48 * 1024 * 1024
NEG_BIG = -1e30
LOG2E = 1.0 / math.log(2.0)


def _cparams(sem):
    return pltpu.CompilerParams(dimension_semantics=sem, vmem_limit_bytes=VMEM_LIMIT)


def _dot(a, b):
    return jnp.dot(a, b, preferred_element_type=F32)


def _dot_nt(a, b):
    return lax.dot_general(a, b, (((1,), (1,)), ((), ())), preferred_element_type=F32)


def _dot_tn(a, b):
    return lax.dot_general(a, b, (((0,), (0,)), ((), ())), preferred_element_type=F32)


def _split2(x):
    hi = x.astype(BF16)
    lo = (x - hi.astype(F32)).astype(BF16)
    return hi, lo


def _mm(a, b, passes, nt=False):
    d = _dot_nt if nt else _dot
    if passes == 1:
        return d(a.astype(BF16), b.astype(BF16))
    ah, al = _split2(a)
    bh, bl = _split2(b)
    return d(ah, bh) + (d(al, bh) + d(ah, bl))


def _mm_exact_rhs(a, e_bf16, parts):
    acc = None
    rem = a
    for _ in range(parts):
        piece = rem.astype(BF16)
        term = _dot(piece, e_bf16)
        acc = term if acc is None else acc + term
        rem = rem - piece.astype(F32)
    return acc


def _mm_exact_lhs(e_bf16, a, parts):
    acc = None
    rem = a
    for _ in range(parts):
        piece = rem.astype(BF16)
        term = _dot(e_bf16, piece)
        acc = term if acc is None else acc + term
        rem = rem - piece.astype(F32)
    return acc


def _rwkv_prep_kernel(x_ref, nw_ref, mu_ref, w0_ref, w1_ref, w2_ref, a0_ref, a1_ref, a2_ref,
                      xs_ref, lw_ref, ic_ref, carry_ref):
    t = pl.program_id(1)
    x = x_ref[...]
    tm = x.shape[0]
    h = x * lax.rsqrt(jnp.mean(x * x, axis=-1, keepdims=True) + NORM_EPS) * nw_ref[...]

    @pl.when(t == 0)
    def _():
        carry_ref[...] = jnp.zeros_like(carry_ref)

    prev_last = carry_ref[0:1, :]
    rolled = pltpu.roll(h, 1, axis=0)
    row = lax.broadcasted_iota(jnp.int32, h.shape, 0)
    h_prev = jnp.where(row == 0, prev_last, rolled)
    carry_ref[0:1, :] = h[tm - 1:tm, :]
    d = h_prev - h
    mu = mu_ref[...]
    xs_ref[0] = (h + d * mu[0:1]).astype(BF16)
    xs_ref[1] = (h + d * mu[2:3]).astype(BF16)
    xs_ref[2] = (h + d * mu[3:4]).astype(BF16)
    xs_ref[3] = (h + d * mu[5:6]).astype(BF16)
    xw = (h + d * mu[1:2]).astype(BF16)
    xa = (h + d * mu[4:5]).astype(BF16)
    zw = w0_ref[...] + _dot(jnp.tanh(_dot(xw, w1_ref[...])).astype(BF16), w2_ref[...])
    lw_ref[...] = (-math.exp(-0.5)) / (1.0 + jnp.exp(-zw))
    za = a0_ref[...] + _dot(_dot(xa, a1_ref[...]).astype(BF16), a2_ref[...])
    ic_ref[...] = 1.0 / (1.0 + jnp.exp(-za))


def _rwkv_prep(x2, nw, mu, w0, w1p, w2p, a0, a1p, a2p, *, batch, seq, tm):
    m, d = x2.shape
    nt = seq // tm
    row_spec = pl.BlockSpec((tm, d), lambda b, t: (b * nt + t, 0))
    vec_spec = pl.BlockSpec((1, d), lambda b, t: (0, 0))

    def full(a):
        return pl.BlockSpec(a.shape, lambda b, t: (0,) * a.ndim)

    return pl.pallas_call(
        _rwkv_prep_kernel,
        grid=(batch, nt),
        in_specs=[row_spec, vec_spec, full(mu), vec_spec, full(w1p), full(w2p), vec_spec, full(a1p), full(a2p)],
        out_specs=[pl.BlockSpec((4, tm, d), lambda b, t: (0, b * nt + t, 0)), row_spec, row_spec],
        out_shape=[jax.ShapeDtypeStruct((4, m, d), BF16),
                   jax.ShapeDtypeStruct((m, d), F32),
                   jax.ShapeDtypeStruct((m, d), F32)],
        scratch_shapes=[pltpu.VMEM((8, d), F32)],
        compiler_params=_cparams(("parallel", "arbitrary")),
        name="rwkv_prep",
    )(x2, nw, mu, w0, w1p, w2p, a0, a1p, a2p)


def _mm_kernel(a_ref, b_ref, o_ref):
    o_ref[...] = _dot(a_ref[...], b_ref[...]).astype(o_ref.dtype)


def _proj4(xs, w, *, tm, tn, out_dtype):
    _, m, k = xs.shape
    n = w.shape[1]
    per_path = (n // 4) // tn
    return pl.pallas_call(
        _mm_kernel,
        grid=(m // tm, n // tn),
        in_specs=[pl.BlockSpec((None, tm, k), lambda i, j: (j // per_path, i, 0)),
                  pl.BlockSpec((k, tn), lambda i, j: (0, j))],
        out_specs=pl.BlockSpec((tm, tn), lambda i, j: (i, j)),
        out_shape=jax.ShapeDtypeStruct((m, n), out_dtype),
        compiler_params=_cparams(("parallel", "arbitrary")),
        name="proj4",
    )(xs, w)


def _matmul(a, w, *, tm, tn, out_dtype):
    m, k = a.shape
    n = w.shape[1]
    return pl.pallas_call(
        _mm_kernel,
        grid=(m // tm, n // tn),
        in_specs=[pl.BlockSpec((tm, k), lambda i, j: (i, 0)),
                  pl.BlockSpec((k, tn), lambda i, j: (0, j))],
        out_specs=pl.BlockSpec((tm, tn), lambda i, j: (i, j)),
        out_shape=jax.ShapeDtypeStruct((m, n), out_dtype),
        compiler_params=_cparams(("parallel", "arbitrary")),
        name="matmul",
    )(a, w)


def _mm_res_kernel(a_ref, b_ref, r_ref, o_ref):
    o_ref[...] = r_ref[...] + _dot(a_ref[...], b_ref[...])


def _matmul_residual(a, w, res, *, tm, tn):
    m, k = a.shape
    n = w.shape[1]
    return pl.pallas_call(
        _mm_res_kernel,
        grid=(m // tm, n // tn),
        in_specs=[pl.BlockSpec((tm, k), lambda i, j: (i, 0)),
                  pl.BlockSpec((k, tn), lambda i, j: (0, j)),
                  pl.BlockSpec((tm, tn), lambda i, j: (i, j))],
        out_specs=pl.BlockSpec((tm, tn), lambda i, j: (i, j)),
        out_shape=jax.ShapeDtypeStruct((m, n), F32),
        compiler_params=_cparams(("parallel", "arbitrary")),
        name="matmul_residual",
    )(a, w, res)


def _mm_res_norm_kernel(a_ref, b_ref, r_ref, nw_ref, o_ref):
    y = r_ref[...] + _dot(a_ref[...], b_ref[...])
    o_ref[...] = y * lax.rsqrt(jnp.mean(y * y, axis=-1, keepdims=True) + NORM_EPS) * nw_ref[...]


def _matmul_residual_norm(a, w, res, nw, *, tm):
    m, k = a.shape
    n = w.shape[1]
    return pl.pallas_call(
        _mm_res_norm_kernel,
        grid=(m // tm,),
        in_specs=[pl.BlockSpec((tm, k), lambda i: (i, 0)),
                  pl.BlockSpec((k, n), lambda i: (0, 0)),
                  pl.BlockSpec((tm, n), lambda i: (i, 0)),
                  pl.BlockSpec((1, n), lambda i: (0, 0))],
        out_specs=pl.BlockSpec((tm, n), lambda i: (i, 0)),
        out_shape=jax.ShapeDtypeStruct((m, n), F32),
        compiler_params=_cparams(("parallel",)),
        name="matmul_residual_norm",
    )(a, w, res, nw)


def _norm_kernel(x_ref, nw_ref, o_ref):
    x = x_ref[...]
    o_ref[...] = (x * lax.rsqrt(jnp.mean(x * x, axis=-1, keepdims=True) + NORM_EPS) * nw_ref[...]).astype(o_ref.dtype)


def _rms_norm(x2, nw, *, tm, out_dtype):
    m, d = x2.shape
    return pl.pallas_call(
        _norm_kernel,
        grid=(m // tm,),
        in_specs=[pl.BlockSpec((tm, d), lambda i: (i, 0)), pl.BlockSpec((1, d), lambda i: (0, 0))],
        out_specs=pl.BlockSpec((tm, d), lambda i: (i, 0)),
        out_shape=jax.ShapeDtypeStruct((m, d), out_dtype),
        compiler_params=_cparams(("parallel",)),
        name="rms_norm",
    )(x2, nw)


WKV_GROUP = 8


def _wkv_kernel(r_ref, k_ref, v_ref, g_ref, lw_ref, ic_ref, kk_ref, ka_ref, rk_ref, lnw_ref, lnb_ref,
                o_ref, rm_sc, y0_sc, g_sc, wc_sc, y_sc, *, group):
    seq = r_ref.shape[0]
    ngroups = seq // (CHUNK * group)
    two = 2 * CHUNK

    lane1 = lax.broadcasted_iota(jnp.int32, (1, LANES), 1)
    m0 = (lane1 < HEAD).astype(F32)
    m1 = 1.0 - m0
    row = lax.broadcasted_iota(jnp.int32, (two, two), 0)
    col = lax.broadcasted_iota(jnp.int32, (two, two), 1)
    strict = row > col
    incl = row >= col
    eye = (row == col).astype(F32)
    ones_bd = ((row < HEAD) == (col < HEAD)).astype(BF16)
    tril_c = incl[:CHUNK, :CHUNK].astype(BF16)
    zeros_b = jnp.zeros((two, LANES), BF16)

    kk_p = kk_ref[...]
    ka_p = ka_ref[...]

    def segsum(z):
        return _mm_exact_rhs(z, ones_bd, 2)

    def stack(z):
        return jnp.concatenate([z * m0, z * m1], axis=0)

    def chunk_rows(c):
        return pl.ds(pl.multiple_of(c * CHUNK, CHUNK), CHUNK)

    def phase_a(gi):
        us = range(group)
        rows = [chunk_rows(gi * group + u) for u in us]
        k = [k_ref[rw, :] for rw in rows]
        lw = [lw_ref[rw, :] for rw in rows]
        kkv = [k[u] * kk_p for u in us]
        n2 = [segsum(kkv[u] * kkv[u]) for u in us]
        lin = [_mm_exact_lhs(tril_c, lw[u], 2) for u in us]
        yield
        a_s, r_s, v_s, bkt, wcb, a_all = [], [], [], [], [], []
        for u in us:
            ic = ic_ref[rows[u], :]
            kkn = kkv[u] / jnp.maximum(jnp.sqrt(n2[u]), 1e-12)
            kh = k[u] * (1.0 + (ic - 1.0) * ka_p)
            e_in = jnp.exp(lin[u])
            e_ex = jnp.exp(lin[u] - lw[u])
            e_neg = jnp.exp(-lin[u])
            b_f = stack(kkn * ic * e_neg)
            k_f = stack(kh * e_neg)
            a_s.append(stack(-kkn * e_ex).astype(BF16))
            r_s.append(stack(r_ref[rows[u], :] * e_in).astype(BF16))
            v_s.append(stack(v_ref[rows[u], :]).astype(BF16))
            a_all.append(_dot_nt(jnp.concatenate([a_s[u], r_s[u]], axis=0),
                                 jnp.concatenate([b_f, k_f], axis=0).astype(BF16)))
            bkt.append(jnp.concatenate([b_f.T, k_f.T], axis=1).astype(BF16))
            wcol = jnp.sum(eye * e_in[CHUNK - 1:CHUNK, :], axis=1, keepdims=True)
            wcb.append(jnp.broadcast_to(wcol, (two, LANES)))
        yield
        t_inv, pwb, a_rb, av = [], [], [], []
        for u in us:
            a_ab = jnp.where(strict, a_all[u][:two, :two], 0.0)
            a_ak = jnp.where(strict, a_all[u][:two, two:], 0.0).astype(BF16)
            a_rk = jnp.where(incl, a_all[u][two:, two:], 0.0).astype(BF16)
            a_rb.append(jnp.where(incl, a_all[u][two:, :two], 0.0).astype(BF16))
            av.append(_dot(jnp.concatenate([a_ak, a_rk], axis=0), v_s[u]))
            t_inv.append(eye + a_ab)
            pwb.append(a_ab.astype(BF16))
        yield
        pwb = [_dot(pwb[u], pwb[u]).astype(BF16) for u in us]
        yield
        for _ in range(4):
            both = [_dot(jnp.concatenate([t_inv[u].astype(BF16), pwb[u]], axis=0), pwb[u]) for u in us]
            t_inv = [t_inv[u] + both[u][:two] for u in us]
            pwb = [both[u][two:].astype(BF16) for u in us]
            yield
        t_inv = [t_inv[u] + _dot(t_inv[u].astype(BF16), pwb[u]) for u in us]
        yield
        pqb = [_dot(t_inv[u].astype(BF16),
                    jnp.concatenate([a_s[u], av[u][:two].astype(BF16)], axis=1)).astype(BF16) for u in us]
        yield
        rq = [_dot(a_rb[u], pqb[u]) for u in us]
        mg = [_dot(bkt[u], jnp.concatenate([pqb[u], jnp.concatenate([zeros_b, v_s[u]], axis=1)], axis=0)) for u in us]
        yield
        for u in us:
            y0 = rq[u][:, LANES:] + av[u][two:]
            rm_sc[u] = jnp.concatenate([r_s[u].astype(F32) + rq[u][:, :LANES], wcb[u] * mg[u][:, :LANES]],
                                       axis=0).astype(BF16)
            g_sc[u] = wcb[u] * mg[u][:, LANES:]
            y0_sc[u] = y0[:CHUNK] + y0[CHUNK:]
            wc_sc[u] = wcb[u]

    def state_step(gi, u, h):
        res = _dot(rm_sc[u], h.astype(BF16))
        y_sc[chunk_rows(gi * group + u), :] = res[:CHUNK] + res[CHUNK:two] + y0_sc[u]
        return wc_sc[u] * h + res[two:] + g_sc[u]

    rk_p = rk_ref[...]
    lnw_p = lnw_ref[...]
    lnb_p = lnb_ref[...]
    inv_n = 1.0 / HEAD
    grows = group * CHUNK

    def finish(gi):
        rows = pl.ds(pl.multiple_of(gi * grows, grows), grows)
        y = y_sc[rows, :]
        mean = segsum(y) * inv_n
        yc = y - mean
        var = segsum(yc * yc) * inv_n
        gn = yc * lax.rsqrt(var + GN_EPS) * lnw_p + lnb_p
        kh = k_ref[rows, :] * (1.0 + (ic_ref[rows, :] - 1.0) * ka_p)
        bonus = segsum(r_ref[rows, :] * kh * rk_p) * v_ref[rows, :]
        g = g_ref[rows, :]
        o_ref[rows, :] = ((gn + bonus) * (g / (1.0 + jnp.exp(-g)))).astype(o_ref.dtype)

    def run(a_gi, b_gi, c_gi, h):
        done = 0
        if a_gi is None:
            for u in range(group):
                h = state_step(b_gi, u, h)
        else:
            for i, _ in enumerate(phase_a(a_gi)):
                if b_gi is not None and done < group and i >= 1:
                    h = state_step(b_gi, done, h)
                    done += 1
            assert b_gi is None or done == group
        if c_gi is not None:
            finish(c_gi)
        return h

    run(0, None, None, None)
    h = run(1, 0, None, jnp.zeros((two, LANES), F32))
    h = lax.fori_loop(1, ngroups - 1, lambda gi, h: run(gi + 1, gi, gi - 1, h), h)
    run(None, ngroups - 1, ngroups - 2, h)
    finish(ngroups - 1)


def _wkv(proj, lw, ic, kk, ka, rk, lnw, lnb, *, batch, seq):
    m, n4 = proj.shape
    w = n4 // 4
    nhp = w // LANES
    two = 2 * CHUNK
    group = WKV_GROUP
    assert seq % (CHUNK * group) == 0 and seq // (CHUNK * group) >= 3

    def col_spec(p):
        return pl.BlockSpec((seq, LANES), lambda b, hp: (b, p * nhp + hp))

    par_spec = pl.BlockSpec((1, LANES), lambda b, hp: (0, hp))
    act_spec = pl.BlockSpec((seq, LANES), lambda b, hp: (b, hp))
    return pl.pallas_call(
        functools.partial(_wkv_kernel, group=group),
        grid=(batch, nhp),
        in_specs=[col_spec(0), col_spec(1), col_spec(2), col_spec(3), act_spec, act_spec,
                  par_spec, par_spec, par_spec, par_spec, par_spec],
        out_specs=act_spec,
        out_shape=jax.ShapeDtypeStruct((m, w), BF16),
        scratch_shapes=[pltpu.VMEM((group, 2 * two, LANES), BF16),
                        pltpu.VMEM((group, CHUNK, LANES), F32),
                        pltpu.VMEM((group, two, LANES), F32),
                        pltpu.VMEM((group, two, LANES), F32),
                        pltpu.VMEM((seq, LANES), F32)],
        compiler_params=_cparams(("parallel", "parallel")),
        name="wkv7",
    )(proj, proj, proj, proj, lw, ic, kk, ka, rk, lnw, lnb)


def _attn_kernel(q_ref, k_ref, v_ref, g_ref, lam_ref, sw_ref, o_ref,
                 z_a, z_b, p_a, p_b, acc_sc, ka1_sc, ka2_sc, q1_sc, q2_sc, *, tq, n_heads, lambda_init):
    seq = q_ref.shape[0]
    nq = seq // tq
    tk = tq
    half = 256
    ng = 2 * tq // LANES
    h = pl.program_id(1)
    lane = lax.broadcasted_iota(jnp.int32, (1, LANES), 1)
    slope = jnp.exp2(jnp.full((1, LANES), -8.0 / n_heads, F32) * (h + 1).astype(F32)) * LOG2E
    s_hi = slope.astype(BF16).astype(F32)
    s_lo = slope - s_hi
    s_pat = jnp.where(lane % 2 == 0, s_hi, s_lo)

    row_s = lax.broadcasted_iota(jnp.int32, (seq, LANES), 0) % tk
    lane_s = lax.broadcasted_iota(jnp.int32, (seq, LANES), 1)
    kidx = jnp.where(lane_s % 4 < 2, row_s % half, (row_s // half) * half).astype(F32)
    ks_all = k_ref[...].astype(F32)
    ka1_sc[...] = jnp.where(lane_s < HEAD, ks_all, jnp.where(lane_s < HEAD + 4, kidx, 0.0)).astype(BF16)
    ka2_sc[...] = jnp.where(lane_s >= HEAD, ks_all, jnp.where(lane_s < 4, kidx, 0.0)).astype(BF16)

    tri = (lax.broadcasted_iota(jnp.int32, (LANES, LANES), 0) <= lax.broadcasted_iota(jnp.int32, (LANES, LANES), 1))
    lam = lam_ref[...]
    lam_full = (jnp.exp(jnp.sum(lam[0:1] * lam[1:2], axis=1, keepdims=True))
                - jnp.exp(jnp.sum(lam[2:3] * lam[3:4], axis=1, keepdims=True)) + lambda_init)
    sw_scaled = sw_ref[...] * (1.0 - lambda_init)

    def load_q(qi):
        qf = q_ref[qi * tq:(qi + 1) * tq, :].astype(F32) * (LOG2E / math.sqrt(HEAD))
        q1_sc[...] = jnp.where(lane < HEAD, qf, jnp.where(lane < HEAD + 4, s_pat, 0.0)).astype(BF16)
        q2_sc[...] = jnp.where(lane >= HEAD, qf, jnp.where(lane < 4, s_pat, 0.0)).astype(BF16)

    def scores_to(j, z_ref):
        z1 = _dot_nt(ka1_sc[j * tk:(j + 1) * tk, :], q1_sc[...])
        z2 = _dot_nt(ka2_sc[j * tk:(j + 1) * tk, :], q2_sc[...])
        for g in range(ng // 2):
            z_ref[g] = z1[:, g * LANES:(g + 1) * LANES]
            z_ref[ng // 2 + g] = z2[:, g * LANES:(g + 1) * LANES]

    def accumulate(j, p_ref, alpha, first):
        p_all = jnp.concatenate([p_ref[g] for g in range(ng)], axis=1)
        pv = _dot_tn(v_ref[j * tk:(j + 1) * tk, :], p_all)
        for g in range(ng):
            sl = slice(g * LANES, (g + 1) * LANES)
            acc_sc[g] = pv[:, sl] if first else alpha[:, sl] * acc_sc[g] + pv[:, sl]

    def softmax(qi, j, z_ref, p_ref, m, l):
        c_blk = slope * float((j - qi) * tq)
        ms, ls, alphas = [], [], []
        for g in range(ng):
            sl = slice(g * LANES, (g + 1) * LANES)
            nrows = tk
            if j == qi:
                nrows = (g % (ng // 2)) * LANES + LANES
                if nrows < tk:
                    p_ref[g, nrows:, :] = jnp.zeros((tk - nrows, LANES), BF16)
            diag_tile = jnp.where(tri, z_ref[g, nrows - LANES:nrows, :], NEG_BIG) if j == qi else None
            nplain = nrows - LANES if j == qi else nrows
            mx = diag_tile.max(axis=0, keepdims=True) if diag_tile is not None else None
            if nplain:
                mp = jnp.max(z_ref[g, 0:nplain, :], axis=0, keepdims=True)
                mx = mp if mx is None else jnp.maximum(mx, mp)
            mg = m[:, sl]
            mn = jnp.maximum(mg, mx + c_blk)
            alpha = jnp.exp2(mg - mn)
            shift = mn - c_blk
            lsum = None
            for r0 in range(0, nrows, LANES):
                zt = diag_tile if (j == qi and r0 == nplain) else z_ref[g, r0:r0 + LANES, :]
                pt = jnp.exp2(zt - shift)
                part = jnp.sum(pt, axis=0, keepdims=True)
                lsum = part if lsum is None else lsum + part
                p_ref[g, r0:r0 + LANES, :] = pt.astype(BF16)
            ls.append(alpha * l[:, sl] + lsum)
            ms.append(mn)
            alphas.append(alpha)
        return jnp.concatenate(ms, axis=1), jnp.concatenate(ls, axis=1), jnp.concatenate(alphas, axis=1)

    def finalize(qi, l):
        on = jnp.concatenate([acc_sc[g] for g in range(ng)], axis=1) / l
        ot = on[:, :tq] - lam_full * on[:, tq:]
        ot = ot * lax.rsqrt(jnp.mean(ot * ot, axis=0, keepdims=True) + SUBLN_EPS)
        g = g_ref[qi * tq:(qi + 1) * tq, :].astype(F32)
        o_ref[qi * tq:(qi + 1) * tq, :] = (ot.T * sw_scaled * (g / (1.0 + jnp.exp(-g)))).astype(o_ref.dtype)

    items = [(qi, j) for qi in range(nq) for j in range(qi + 1)]
    zs, ps = (z_a, z_b), (p_a, p_b)
    load_q(0)
    scores_to(0, z_a)
    m = l = alpha = None
    for n, (qi, j) in enumerate(items):
        if n > 0:
            pqi, pj = items[n - 1]
            accumulate(pj, ps[(n - 1) % 2], alpha, pj == 0)
            if pqi != qi:
                finalize(pqi, l)
        if j == 0:
            m = jnp.full((1, 2 * tq), NEG_BIG, F32)
            l = jnp.zeros((1, 2 * tq), F32)
        if n + 1 < len(items):
            nqi, nj = items[n + 1]
            if nqi != qi:
                load_q(nqi)
            scores_to(nj, zs[(n + 1) % 2])
        m, l, alpha = softmax(qi, j, zs[n % 2], ps[n % 2], m, l)
    pqi, pj = items[-1]
    accumulate(pj, ps[(len(items) - 1) % 2], alpha, pj == 0)
    finalize(pqi, l)


def _diff_attn(qkvg, lam, sw, *, batch, seq, tq, lambda_init):
    m, n4 = qkvg.shape
    w = n4 // 4
    nh = w // LANES
    ngl = 2 * tq // LANES

    def col_spec(p):
        return pl.BlockSpec((seq, LANES), lambda b, h: (b, p * nh + h))

    return pl.pallas_call(
        functools.partial(_attn_kernel, tq=tq, n_heads=nh, lambda_init=lambda_init),
        grid=(batch, nh),
        in_specs=[col_spec(0), col_spec(1), col_spec(2), col_spec(3),
                  pl.BlockSpec(lam.shape, lambda b, h: (0, 0)),
                  pl.BlockSpec((1, LANES), lambda b, h: (0, 0))],
        out_specs=pl.BlockSpec((seq, LANES), lambda b, h: (b, h)),
        out_shape=jax.ShapeDtypeStruct((m, w), BF16),
        scratch_shapes=[pltpu.VMEM((ngl, tq, LANES), F32), pltpu.VMEM((ngl, tq, LANES), F32),
                        pltpu.VMEM((ngl, tq, LANES), BF16), pltpu.VMEM((ngl, tq, LANES), BF16),
                        pltpu.VMEM((ngl, LANES, LANES), F32),
                        pltpu.VMEM((seq, LANES), BF16), pltpu.VMEM((seq, LANES), BF16),
                        pltpu.VMEM((tq, LANES), BF16), pltpu.VMEM((tq, LANES), BF16)],
        compiler_params=_cparams(("parallel", "parallel")),
        name="diff_attn",
    )(qkvg, qkvg, qkvg, qkvg, lam, sw)


def kernel(x, norm_w, rwkv_mu, rwkv_w_in, rwkv_w0, rwkv_w1, rwkv_w2, rwkv_a0, rwkv_a1, rwkv_a2, rwkv_k_k, rwkv_k_a, rwkv_r_k, rwkv_ln_w, rwkv_ln_b, rwkv_w_out, diff_w_in, diff_lambda, diff_subln_w, diff_w_out, final_norm_w):
    batch, seq, d = x.shape
    m = batch * seq
    assert d % LANES == 0 and seq % CHUNK == 0
    assert norm_w.shape[0] == 2 and rwkv_mu.shape[0] == 1 and diff_w_in.shape[0] == 1
    tm_big = min(1024, m)
    tn_big = min(1024, d)
    tm_prep = min(256, seq)
    tm_last = min(512, m)
    tq = min(512, seq)

    x2 = x.reshape(m, d)
    row = lambda a: a.reshape(1, -1)
    lora = rwkv_w1.shape[-1]
    pad_c = lambda a: jnp.pad(a, ((0, 0), (0, LORA_PAD - lora))).astype(BF16)
    pad_r = lambda a: jnp.pad(a, ((0, LORA_PAD - lora), (0, 0))).astype(BF16)

    xs, lw, ic = _rwkv_prep(x2, row(norm_w[0]), rwkv_mu[0], row(rwkv_w0[0]), pad_c(rwkv_w1[0]), pad_r(rwkv_w2[0]),
                            row(rwkv_a0[0]), pad_c(rwkv_a1[0]), pad_r(rwkv_a2[0]),
                            batch=batch, seq=seq, tm=tm_prep)
    proj = _proj4(xs, rwkv_w_in[0].astype(BF16), tm=tm_big, tn=tn_big, out_dtype=F32)
    yg = _wkv(proj, lw, ic, row(rwkv_k_k[0]), row(rwkv_k_a[0]), row(rwkv_r_k[0]), row(rwkv_ln_w[0]),
              row(rwkv_ln_b[0]), batch=batch, seq=seq)
    x1 = _matmul_residual(yg, rwkv_w_out[0].astype(BF16), x2, tm=tm_big, tn=tn_big)

    lambda_init = 0.8 - 0.6 * math.exp(-0.3 * 1)
    h1 = _rms_norm(x1, row(norm_w[1]), tm=tm_last, out_dtype=BF16)
    qkvg = _matmul(h1, diff_w_in[0].astype(BF16), tm=tm_big, tn=tn_big, out_dtype=BF16)
    att = _diff_attn(qkvg, diff_lambda[0], row(diff_subln_w[0]), batch=batch, seq=seq, tq=tq,
                     lambda_init=lambda_init)
    out = _matmul_residual_norm(att, diff_w_out[0].astype(BF16), x1, row(final_norm_w), tm=tm_last)
    return out.reshape(batch, seq, d)
```

```python
import functools
import math

import jax
import jax.numpy as jnp
from jax import lax
from jax.experimental import pallas as pl
from jax.experimental.pallas import tpu as pltpu

F32 = jnp.float32
BF16 = jnp.bfloat16

NORM_EPS = 1e-6
GN_EPS = 64e-5
SUBLN_EPS = 1e-5
HEAD = 64
LANES = 128
CHUNK = 64
LORA_PAD = 128
VMEM_LIMIT = 48 * 1024 * 1024
NEG_BIG = -1e30
LOG2E = 1.0 / math.log(2.0)


def _cparams(sem):
    return pltpu.CompilerParams(dimension_semantics=sem, vmem_limit_bytes=VMEM_LIMIT)


def _dot(a, b):
    return jnp.dot(a, b, preferred_element_type=F32)


def _dot_nt(a, b):
    return lax.dot_general(a, b, (((1,), (1,)), ((), ())), preferred_element_type=F32)


def _dot_tn(a, b):
    return lax.dot_general(a, b, (((0,), (0,)), ((), ())), preferred_element_type=F32)


def _split2(x):
    hi = x.astype(BF16)
    lo = (x - hi.astype(F32)).astype(BF16)
    return hi, lo


def _mm(a, b, passes, nt=False):
    d = _dot_nt if nt else _dot
    if passes == 1:
        return d(a.astype(BF16), b.astype(BF16))
    ah, al = _split2(a)
    bh, bl = _split2(b)
    return d(ah, bh) + (d(al, bh) + d(ah, bl))


def _mm_exact_rhs(a, e_bf16, parts):
    acc = None
    rem = a
    for _ in range(parts):
        piece = rem.astype(BF16)
        term = _dot(piece, e_bf16)
        acc = term if acc is None else acc + term
        rem = rem - piece.astype(F32)
    return acc


def _mm_exact_lhs(e_bf16, a, parts):
    acc = None
    rem = a
    for _ in range(parts):
        piece = rem.astype(BF16)
        term = _dot(e_bf16, piece)
        acc = term if acc is None else acc + term
        rem = rem - piece.astype(F32)
    return acc


def _rwkv_prep_kernel(x_ref, nw_ref, mu_ref, w0_ref, w1_ref, w2_ref, a0_ref, a1_ref, a2_ref,
                      xs_ref, lw_ref, ic_ref, carry_ref):
    t = pl.program_id(1)
    x = x_ref[...]
    tm = x.shape[0]
    h = x * lax.rsqrt(jnp.mean(x * x, axis=-1, keepdims=True) + NORM_EPS) * nw_ref[...]

    @pl.when(t == 0)
    def _():
        carry_ref[...] = jnp.zeros_like(carry_ref)

    prev_last = carry_ref[0:1, :]
    rolled = pltpu.roll(h, 1, axis=0)
    row = lax.broadcasted_iota(jnp.int32, h.shape, 0)
    h_prev = jnp.where(row == 0, prev_last, rolled)
    carry_ref[0:1, :] = h[tm - 1:tm, :]
    d = h_prev - h
    mu = mu_ref[...]
    xs_ref[0] = (h + d * mu[0:1]).astype(BF16)
    xs_ref[1] = (h + d * mu[2:3]).astype(BF16)
    xs_ref[2] = (h + d * mu[3:4]).astype(BF16)
    xs_ref[3] = (h + d * mu[5:6]).astype(BF16)
    xw = (h + d * mu[1:2]).astype(BF16)
    xa = (h + d * mu[4:5]).astype(BF16)
    zw = w0_ref[...] + _dot(jnp.tanh(_dot(xw, w1_ref[...])).astype(BF16), w2_ref[...])
    lw_ref[...] = (-math.exp(-0.5)) / (1.0 + jnp.exp(-zw))
    za = a0_ref[...] + _dot(_dot(xa, a1_ref[...]).astype(BF16), a2_ref[...])
    ic_ref[...] = 1.0 / (1.0 + jnp.exp(-za))


def _rwkv_prep(x2, nw, mu, w0, w1p, w2p, a0, a1p, a2p, *, batch, seq, tm):
    m, d = x2.shape
    nt = seq // tm
    row_spec = pl.BlockSpec((tm, d), lambda b, t: (b * nt + t, 0))
    vec_spec = pl.BlockSpec((1, d), lambda b, t: (0, 0))

    def full(a):
        return pl.BlockSpec(a.shape, lambda b, t: (0,) * a.ndim)

    return pl.pallas_call(
        _rwkv_prep_kernel,
        grid=(batch, nt),
        in_specs=[row_spec, vec_spec, full(mu), vec_spec, full(w1p), full(w2p), vec_spec, full(a1p), full(a2p)],
        out_specs=[pl.BlockSpec((4, tm, d), lambda b, t: (0, b * nt + t, 0)), row_spec, row_spec],
        out_shape=[jax.ShapeDtypeStruct((4, m, d), BF16),
                   jax.ShapeDtypeStruct((m, d), F32),
                   jax.ShapeDtypeStruct((m, d), F32)],
        scratch_shapes=[pltpu.VMEM((8, d), F32)],
        compiler_params=_cparams(("parallel", "arbitrary")),
        name="rwkv_prep",
    )(x2, nw, mu, w0, w1p, w2p, a0, a1p, a2p)


def _mm_wcast_kernel(a_ref, b_ref, o_ref, w_sc):
    @pl.when(pl.program_id(1) == 0)
    def _():
        w_sc[...] = b_ref[...].astype(BF16)

    o_ref[...] = _dot(a_ref[...], w_sc[...]).astype(o_ref.dtype)


def _matmul_wcast(a, w, *, tm, tn, out_dtype, groups=1):
    m, k = a.shape[-2:]
    n = w.shape[1]
    per_group = (n // groups) // tn
    if groups > 1:
        a_spec = pl.BlockSpec((None, tm, k), lambda j, i: (j // per_group, i, 0))
    else:
        a_spec = pl.BlockSpec((tm, k), lambda j, i: (i, 0))
    return pl.pallas_call(
        _mm_wcast_kernel,
        grid=(n // tn, m // tm),
        in_specs=[a_spec, pl.BlockSpec((k, tn), lambda j, i: (0, j))],
        out_specs=pl.BlockSpec((tm, tn), lambda j, i: (i, j)),
        out_shape=jax.ShapeDtypeStruct((m, n), out_dtype),
        scratch_shapes=[pltpu.VMEM((k, tn), BF16)],
        compiler_params=_cparams(("parallel", "arbitrary")),
        name="proj" if groups > 1 else "matmul",
    )(a, w)


def _out_proj_kernel(a_ref, b_ref, r_ref, nw_ref, *o_refs):
    y = r_ref[...] + _dot(a_ref[...], b_ref[...])
    yn = y * lax.rsqrt(jnp.mean(y * y, axis=-1, keepdims=True) + NORM_EPS) * nw_ref[...]
    if len(o_refs) == 2:
        o_refs[0][...] = y
    o_refs[-1][...] = yn.astype(o_refs[-1].dtype)


def _out_proj(a, w, res, nw, *, tm, keep_sum, norm_dtype):
    m, k = a.shape
    n = w.shape[1]
    row_spec = pl.BlockSpec((tm, n), lambda i: (i, 0))
    out_specs = [row_spec, row_spec] if keep_sum else [row_spec]
    out_shape = [jax.ShapeDtypeStruct((m, n), norm_dtype)]
    if keep_sum:
        out_shape = [jax.ShapeDtypeStruct((m, n), F32)] + out_shape
    return pl.pallas_call(
        _out_proj_kernel,
        grid=(m // tm,),
        in_specs=[pl.BlockSpec((tm, k), lambda i: (i, 0)),
                  pl.BlockSpec((k, n), lambda i: (0, 0)),
                  row_spec,
                  pl.BlockSpec((1, n), lambda i: (0, 0))],
        out_specs=out_specs,
        out_shape=out_shape,
        compiler_params=_cparams(("parallel",)),
        name="out_proj_sum_norm" if keep_sum else "out_proj_norm",
    )(a, w, res, nw)


WKV_GROUP = 8


def _wkv_kernel(r_ref, k_ref, v_ref, g_ref, lw_ref, ic_ref, kk_ref, ka_ref, rk_ref, lnw_ref, lnb_ref,
                o_ref, rm_sc, y0_sc, g_sc, wc_sc, y_sc, *, group):
    seq = r_ref.shape[0]
    ngroups = seq // (CHUNK * group)
    two = 2 * CHUNK

    lane1 = lax.broadcasted_iota(jnp.int32, (1, LANES), 1)
    m0 = (lane1 < HEAD).astype(F32)
    m1 = 1.0 - m0
    row = lax.broadcasted_iota(jnp.int32, (two, two), 0)
    col = lax.broadcasted_iota(jnp.int32, (two, two), 1)
    strict = row > col
    incl = row >= col
    eye = (row == col).astype(F32)
    ones_bd = ((row < HEAD) == (col < HEAD)).astype(BF16)
    tril_c = incl[:CHUNK, :CHUNK].astype(BF16)
    zeros_b = jnp.zeros((two, LANES), BF16)

    kk_p = kk_ref[...]
    ka_p = ka_ref[...]

    def segsum(z):
        return _mm_exact_rhs(z, ones_bd, 2)

    def stack(z):
        return jnp.concatenate([z * m0, z * m1], axis=0)

    def chunk_rows(c):
        return pl.ds(pl.multiple_of(c * CHUNK, CHUNK), CHUNK)

    def phase_a(gi):
        us = range(group)
        rows = [chunk_rows(gi * group + u) for u in us]
        k = [k_ref[rw, :] for rw in rows]
        lw = [lw_ref[rw, :] for rw in rows]
        kkv = [k[u] * kk_p for u in us]
        n2 = [segsum(kkv[u] * kkv[u]) for u in us]
        lin = [_mm_exact_lhs(tril_c, lw[u], 2) for u in us]
        yield
        a_s, r_s, v_s, bkt, wcb, a_all = [], [], [], [], [], []
        for u in us:
            ic = ic_ref[rows[u], :]
            kkn = kkv[u] / jnp.maximum(jnp.sqrt(n2[u]), 1e-12)
            kh = k[u] * (1.0 + (ic - 1.0) * ka_p)
            e_in = jnp.exp(lin[u])
            e_ex = jnp.exp(lin[u] - lw[u])
            e_neg = jnp.exp(-lin[u])
            b_f = stack(kkn * ic * e_neg)
            k_f = stack(kh * e_neg)
            a_s.append(stack(-kkn * e_ex).astype(BF16))
            r_s.append(stack(r_ref[rows[u], :] * e_in).astype(BF16))
            v_s.append(stack(v_ref[rows[u], :]).astype(BF16))
            a_all.append(_dot_nt(jnp.concatenate([a_s[u], r_s[u]], axis=0),
                                 jnp.concatenate([b_f, k_f], axis=0).astype(BF16)))
            bkt.append(jnp.concatenate([b_f.T, k_f.T], axis=1).astype(BF16))
            wcol = jnp.sum(eye * e_in[CHUNK - 1:CHUNK, :], axis=1, keepdims=True)
            wcb.append(jnp.broadcast_to(wcol, (two, LANES)))
        yield
        t_inv, pwb, a_rb, av = [], [], [], []
        for u in us:
            a_ab = jnp.where(strict, a_all[u][:two, :two], 0.0)
            a_ak = jnp.where(strict, a_all[u][:two, two:], 0.0).astype(BF16)
            a_rk = jnp.where(incl, a_all[u][two:, two:], 0.0).astype(BF16)
            a_rb.append(jnp.where(incl, a_all[u][two:, :two], 0.0).astype(BF16))
            av.append(_dot(jnp.concatenate([a_ak, a_rk], axis=0), v_s[u]))
            t_inv.append(eye + a_ab)
            pwb.append(a_ab.astype(BF16))
        yield
        pwb = [_dot(pwb[u], pwb[u]).astype(BF16) for u in us]
        yield
        for _ in range(4):
            both = [_dot(jnp.concatenate([t_inv[u].astype(BF16), pwb[u]], axis=0), pwb[u]) for u in us]
            t_inv = [t_inv[u] + both[u][:two] for u in us]
            pwb = [both[u][two:].astype(BF16) for u in us]
            yield
        t_inv = [t_inv[u] + _dot(t_inv[u].astype(BF16), pwb[u]) for u in us]
        yield
        pqb = [_dot(t_inv[u].astype(BF16),
                    jnp.concatenate([a_s[u], av[u][:two].astype(BF16)], axis=1)).astype(BF16) for u in us]
        yield
        rq = [_dot(a_rb[u], pqb[u]) for u in us]
        mg = [_dot(bkt[u], jnp.concatenate([pqb[u], jnp.concatenate([zeros_b, v_s[u]], axis=1)], axis=0)) for u in us]
        yield
        for u in us:
            y0 = rq[u][:, LANES:] + av[u][two:]
            rm_sc[u] = jnp.concatenate([r_s[u].astype(F32) + rq[u][:, :LANES], wcb[u] * mg[u][:, :LANES]],
                                       axis=0).astype(BF16)
            g_sc[u] = wcb[u] * mg[u][:, LANES:]
            y0_sc[u] = y0[:CHUNK] + y0[CHUNK:]
            wc_sc[u] = wcb[u]

    def state_step(gi, u, h):
        res = _dot(rm_sc[u], h.astype(BF16))
        y_sc[chunk_rows(gi * group + u), :] = res[:CHUNK] + res[CHUNK:two] + y0_sc[u]
        return wc_sc[u] * h + res[two:] + g_sc[u]

    rk_p = rk_ref[...]
    lnw_p = lnw_ref[...]
    lnb_p = lnb_ref[...]
    inv_n = 1.0 / HEAD
    grows = group * CHUNK

    def finish(gi):
        rows = pl.ds(pl.multiple_of(gi * grows, grows), grows)
        y = y_sc[rows, :]
        mean = segsum(y) * inv_n
        yc = y - mean
        var = segsum(yc * yc) * inv_n
        gn = yc * lax.rsqrt(var + GN_EPS) * lnw_p + lnb_p
        kh = k_ref[rows, :] * (1.0 + (ic_ref[rows, :] - 1.0) * ka_p)
        bonus = segsum(r_ref[rows, :] * kh * rk_p) * v_ref[rows, :]
        g = g_ref[rows, :]
        o_ref[rows, :] = ((gn + bonus) * (g / (1.0 + jnp.exp(-g)))).astype(o_ref.dtype)

    def run(a_gi, b_gi, c_gi, h):
        done = 0
        if a_gi is None:
            for u in range(group):
                h = state_step(b_gi, u, h)
        else:
            for i, _ in enumerate(phase_a(a_gi)):
                if b_gi is not None and done < group and i >= 1:
                    h = state_step(b_gi, done, h)
                    done += 1
            assert b_gi is None or done == group
        if c_gi is not None:
            finish(c_gi)
        return h

    run(0, None, None, None)
    h = run(1, 0, None, jnp.zeros((two, LANES), F32))
    h = lax.fori_loop(1, ngroups - 1, lambda gi, h: run(gi + 1, gi, gi - 1, h), h)
    run(None, ngroups - 1, ngroups - 2, h)
    finish(ngroups - 1)


def _wkv(proj, lw, ic, kk, ka, rk, lnw, lnb, *, batch, seq):
    m, n4 = proj.shape
    w = n4 // 4
    nhp = w // LANES
    two = 2 * CHUNK
    group = WKV_GROUP
    assert seq % (CHUNK * group) == 0 and seq // (CHUNK * group) >= 3

    def col_spec(p):
        return pl.BlockSpec((seq, LANES), lambda b, hp: (b, p * nhp + hp))

    par_spec = pl.BlockSpec((1, LANES), lambda b, hp: (0, hp))
    act_spec = pl.BlockSpec((seq, LANES), lambda b, hp: (b, hp))
    return pl.pallas_call(
        functools.partial(_wkv_kernel, group=group),
        grid=(batch, nhp),
        in_specs=[col_spec(0), col_spec(1), col_spec(2), col_spec(3), act_spec, act_spec,
                  par_spec, par_spec, par_spec, par_spec, par_spec],
        out_specs=act_spec,
        out_shape=jax.ShapeDtypeStruct((m, w), BF16),
        scratch_shapes=[pltpu.VMEM((group, 2 * two, LANES), BF16),
                        pltpu.VMEM((group, CHUNK, LANES), F32),
                        pltpu.VMEM((group, two, LANES), F32),
                        pltpu.VMEM((group, two, LANES), F32),
                        pltpu.VMEM((seq, LANES), F32)],
        compiler_params=_cparams(("parallel", "parallel")),
        name="wkv7",
    )(proj, proj, proj, proj, lw, ic, kk, ka, rk, lnw, lnb)


def _attn_kernel(q_ref, k_ref, v_ref, g_ref, lam_ref, sw_ref, o_ref,
                 z_a, z_b, p_a, p_b, acc_sc, ka1_sc, ka2_sc, q1_sc, q2_sc, *, tq, n_heads, lambda_init):
    seq = q_ref.shape[0]
    nq = seq // tq
    tk = tq
    half = 256
    ng = 2 * tq // LANES
    h = pl.program_id(1)
    lane = lax.broadcasted_iota(jnp.int32, (1, LANES), 1)
    slope = jnp.exp2(jnp.full((1, LANES), -8.0 / n_heads, F32) * (h + 1).astype(F32)) * LOG2E
    s_hi = slope.astype(BF16).astype(F32)
    s_lo = slope - s_hi
    s_pat = jnp.where(lane % 2 == 0, s_hi, s_lo)

    row_s = lax.broadcasted_iota(jnp.int32, (seq, LANES), 0) % tk
    lane_s = lax.broadcasted_iota(jnp.int32, (seq, LANES), 1)
    kidx = jnp.where(lane_s % 4 < 2, row_s % half, (row_s // half) * half).astype(F32)
    ks_all = k_ref[...].astype(F32)
    ka1_sc[...] = jnp.where(lane_s < HEAD, ks_all, jnp.where(lane_s < HEAD + 4, kidx, 0.0)).astype(BF16)
    ka2_sc[...] = jnp.where(lane_s >= HEAD, ks_all, jnp.where(lane_s < 4, kidx, 0.0)).astype(BF16)

    tri = (lax.broadcasted_iota(jnp.int32, (LANES, LANES), 0) <= lax.broadcasted_iota(jnp.int32, (LANES, LANES), 1))
    lam = lam_ref[...]
    lam_full = (jnp.exp(jnp.sum(lam[0:1] * lam[1:2], axis=1, keepdims=True))
                - jnp.exp(jnp.sum(lam[2:3] * lam[3:4], axis=1, keepdims=True)) + lambda_init)
    sw_scaled = sw_ref[...] * (1.0 - lambda_init)

    def load_q(qi):
        qf = q_ref[qi * tq:(qi + 1) * tq, :].astype(F32) * (LOG2E / math.sqrt(HEAD))
        q1_sc[...] = jnp.where(lane < HEAD, qf, jnp.where(lane < HEAD + 4, s_pat, 0.0)).astype(BF16)
        q2_sc[...] = jnp.where(lane >= HEAD, qf, jnp.where(lane < 4, s_pat, 0.0)).astype(BF16)

    def scores_to(j, z_ref):
        z1 = _dot_nt(ka1_sc[j * tk:(j + 1) * tk, :], q1_sc[...])
        z2 = _dot_nt(ka2_sc[j * tk:(j + 1) * tk, :], q2_sc[...])
        for g in range(ng // 2):
            z_ref[g] = z1[:, g * LANES:(g + 1) * LANES]
            z_ref[ng // 2 + g] = z2[:, g * LANES:(g + 1) * LANES]

    def accumulate(j, p_ref, alpha, first):
        p_all = jnp.concatenate([p_ref[g] for g in range(ng)], axis=1)
        pv = _dot_tn(v_ref[j * tk:(j + 1) * tk, :], p_all)
        for g in range(ng):
            sl = slice(g * LANES, (g + 1) * LANES)
            acc_sc[g] = pv[:, sl] if first else alpha[:, sl] * acc_sc[g] + pv[:, sl]

    def softmax(qi, j, z_ref, p_ref, m, l):
        c_blk = slope * float((j - qi) * tq)
        ms, ls, alphas = [], [], []
        for g in range(ng):
            sl = slice(g * LANES, (g + 1) * LANES)
            nrows = tk
            if j == qi:
                nrows = (g % (ng // 2)) * LANES + LANES
                if nrows < tk:
                    p_ref[g, nrows:, :] = jnp.zeros((tk - nrows, LANES), BF16)
            diag_tile = jnp.where(tri, z_ref[g, nrows - LANES:nrows, :], NEG_BIG) if j == qi else None
            nplain = nrows - LANES if j == qi else nrows
            mx = diag_tile.max(axis=0, keepdims=True) if diag_tile is not None else None
            if nplain:
                mp = jnp.max(z_ref[g, 0:nplain, :], axis=0, keepdims=True)
                mx = mp if mx is None else jnp.maximum(mx, mp)
            mg = m[:, sl]
            mn = jnp.maximum(mg, mx + c_blk)
            alpha = jnp.exp2(mg - mn)
            shift = mn - c_blk
            lsum = None
            for r0 in range(0, nrows, LANES):
                zt = diag_tile if (j == qi and r0 == nplain) else z_ref[g, r0:r0 + LANES, :]
                pt = jnp.exp2(zt - shift)
                part = jnp.sum(pt, axis=0, keepdims=True)
                lsum = part if lsum is None else lsum + part
                p_ref[g, r0:r0 + LANES, :] = pt.astype(BF16)
            ls.append(alpha * l[:, sl] + lsum)
            ms.append(mn)
            alphas.append(alpha)
        return jnp.concatenate(ms, axis=1), jnp.concatenate(ls, axis=1), jnp.concatenate(alphas, axis=1)

    def finalize(qi, l):
        on = jnp.concatenate([acc_sc[g] for g in range(ng)], axis=1) / l
        ot = on[:, :tq] - lam_full * on[:, tq:]
        ot = ot * lax.rsqrt(jnp.mean(ot * ot, axis=0, keepdims=True) + SUBLN_EPS)
        g = g_ref[qi * tq:(qi + 1) * tq, :].astype(F32)
        o_ref[qi * tq:(qi + 1) * tq, :] = (ot.T * sw_scaled * (g / (1.0 + jnp.exp(-g)))).astype(o_ref.dtype)

    items = [(qi, j) for qi in range(nq) for j in range(qi + 1)]
    zs, ps = (z_a, z_b), (p_a, p_b)
    load_q(0)
    scores_to(0, z_a)
    m = l = alpha = None
    for n, (qi, j) in enumerate(items):
        if n > 0:
            pqi, pj = items[n - 1]
            accumulate(pj, ps[(n - 1) % 2], alpha, pj == 0)
            if pqi != qi:
                finalize(pqi, l)
        if j == 0:
            m = jnp.full((1, 2 * tq), NEG_BIG, F32)
            l = jnp.zeros((1, 2 * tq), F32)
        if n + 1 < len(items):
            nqi, nj = items[n + 1]
            if nqi != qi:
                load_q(nqi)
            scores_to(nj, zs[(n + 1) % 2])
        m, l, alpha = softmax(qi, j, zs[n % 2], ps[n % 2], m, l)
    pqi, pj = items[-1]
    accumulate(pj, ps[(len(items) - 1) % 2], alpha, pj == 0)
    finalize(pqi, l)


def _diff_attn(qkvg, lam, sw, *, batch, seq, tq, lambda_init):
    m, n4 = qkvg.shape
    w = n4 // 4
    nh = w // LANES
    ngl = 2 * tq // LANES

    def col_spec(p):
        return pl.BlockSpec((seq, LANES), lambda b, h: (b, p * nh + h))

    return pl.pallas_call(
        functools.partial(_attn_kernel, tq=tq, n_heads=nh, lambda_init=lambda_init),
        grid=(batch, nh),
        in_specs=[col_spec(0), col_spec(1), col_spec(2), col_spec(3),
                  pl.BlockSpec(lam.shape, lambda b, h: (0, 0)),
                  pl.BlockSpec((1, LANES), lambda b, h: (0, 0))],
        out_specs=pl.BlockSpec((seq, LANES), lambda b, h: (b, h)),
        out_shape=jax.ShapeDtypeStruct((m, w), BF16),
        scratch_shapes=[pltpu.VMEM((ngl, tq, LANES), F32), pltpu.VMEM((ngl, tq, LANES), F32),
                        pltpu.VMEM((ngl, tq, LANES), BF16), pltpu.VMEM((ngl, tq, LANES), BF16),
                        pltpu.VMEM((ngl, LANES, LANES), F32),
                        pltpu.VMEM((seq, LANES), BF16), pltpu.VMEM((seq, LANES), BF16),
                        pltpu.VMEM((tq, LANES), BF16), pltpu.VMEM((tq, LANES), BF16)],
        compiler_params=_cparams(("parallel", "parallel")),
        name="diff_attn",
    )(qkvg, qkvg, qkvg, qkvg, lam, sw)


def kernel(x, norm_w, rwkv_mu, rwkv_w_in, rwkv_w0, rwkv_w1, rwkv_w2, rwkv_a0, rwkv_a1, rwkv_a2, rwkv_k_k, rwkv_k_a, rwkv_r_k, rwkv_ln_w, rwkv_ln_b, rwkv_w_out, diff_w_in, diff_lambda, diff_subln_w, diff_w_out, final_norm_w):
    batch, seq, d = x.shape
    m = batch * seq
    assert d % LANES == 0 and seq % CHUNK == 0
    assert norm_w.shape[0] == 2 and rwkv_mu.shape[0] == 1 and diff_w_in.shape[0] == 1
    tm_big = min(1024, m)
    tn_big = min(1024, d)
    tm_prep = min(256, seq)
    tm_last = min(512, m)
    tq = min(512, seq)

    x2 = x.reshape(m, d)
    row = lambda a: a.reshape(1, -1)
    lora = rwkv_w1.shape[-1]
    pad_c = lambda a: jnp.pad(a, ((0, 0), (0, LORA_PAD - lora))).astype(BF16)
    pad_r = lambda a: jnp.pad(a, ((0, LORA_PAD - lora), (0, 0))).astype(BF16)

    xs, lw, ic = _rwkv_prep(x2, row(norm_w[0]), rwkv_mu[0], row(rwkv_w0[0]), pad_c(rwkv_w1[0]), pad_r(rwkv_w2[0]),
                            row(rwkv_a0[0]), pad_c(rwkv_a1[0]), pad_r(rwkv_a2[0]),
                            batch=batch, seq=seq, tm=tm_prep)
    proj = _matmul_wcast(xs, rwkv_w_in[0], tm=tm_big, tn=tn_big, out_dtype=F32, groups=4)
    yg = _wkv(proj, lw, ic, row(rwkv_k_k[0]), row(rwkv_k_a[0]), row(rwkv_r_k[0]), row(rwkv_ln_w[0]),
              row(rwkv_ln_b[0]), batch=batch, seq=seq)
    x1, h1 = _out_proj(yg, rwkv_w_out[0].astype(BF16), x2, row(norm_w[1]), tm=tm_last, keep_sum=True,
                       norm_dtype=BF16)

    lambda_init = 0.8 - 0.6 * math.exp(-0.3 * 1)
    qkvg = _matmul_wcast(h1, diff_w_in[0], tm=tm_big, tn=tn_big, out_dtype=BF16)
    att = _diff_attn(qkvg, diff_lambda[0], row(diff_subln_w[0]), batch=batch, seq=seq, tq=tq,
                     lambda_init=lambda_init)
    (out,) = _out_proj(att, diff_w_out[0].astype(BF16), x1, row(final_norm_w), tm=tm_last, keep_sum=False,
                       norm_dtype=F32)
    return out.reshape(batch, seq, d)
```

```python
import functools
import math

import jax
import jax.numpy as jnp
from jax import lax
from jax.experimental import pallas as pl
from jax.experimental.pallas import tpu as pltpu

F32 = jnp.float32
BF16 = jnp.bfloat16

NORM_EPS = 1e-6
GN_EPS = 64e-5
SUBLN_EPS = 1e-5
HEAD = 64
LANES = 128
CHUNK = 64
LORA_PAD = 128
VMEM_LIMIT = 48 * 1024 * 1024
NEG_BIG = -1e30
LOG2E = 1.0 / math.log(2.0)


def _cparams(sem):
    return pltpu.CompilerParams(dimension_semantics=sem, vmem_limit_bytes=VMEM_LIMIT)


def _dot(a, b):
    return jnp.dot(a, b, preferred_element_type=F32)


def _dot_nt(a, b):
    return lax.dot_general(a, b, (((1,), (1,)), ((), ())), preferred_element_type=F32)


def _dot_tn(a, b):
    return lax.dot_general(a, b, (((0,), (0,)), ((), ())), preferred_element_type=F32)


def _split2(x):
    hi = x.astype(BF16)
    lo = (x - hi.astype(F32)).astype(BF16)
    return hi, lo


def _mm(a, b, passes, nt=False):
    d = _dot_nt if nt else _dot
    if passes == 1:
        return d(a.astype(BF16), b.astype(BF16))
    ah, al = _split2(a)
    bh, bl = _split2(b)
    return d(ah, bh) + (d(al, bh) + d(ah, bl))


def _mm_exact_rhs(a, e_bf16, parts):
    acc = None
    rem = a
    for _ in range(parts):
        piece = rem.astype(BF16)
        term = _dot(piece, e_bf16)
        acc = term if acc is None else acc + term
        rem = rem - piece.astype(F32)
    return acc


def _mm_exact_lhs(e_bf16, a, parts):
    acc = None
    rem = a
    for _ in range(parts):
        piece = rem.astype(BF16)
        term = _dot(e_bf16, piece)
        acc = term if acc is None else acc + term
        rem = rem - piece.astype(F32)
    return acc


def _rwkv_prep_kernel(x_ref, nw_ref, mu_ref, w0_ref, w1_ref, w2_ref, a0_ref, a1_ref, a2_ref,
                      xs_ref, lw_ref, ic_ref, carry_ref):
    t = pl.program_id(1)
    x = x_ref[...]
    tm = x.shape[0]
    h = x * lax.rsqrt(jnp.mean(x * x, axis=-1, keepdims=True) + NORM_EPS) * nw_ref[...]

    @pl.when(t == 0)
    def _():
        carry_ref[...] = jnp.zeros_like(carry_ref)

    prev_last = carry_ref[0:1, :]
    rolled = pltpu.roll(h, 1, axis=0)
    row = lax.broadcasted_iota(jnp.int32, h.shape, 0)
    h_prev = jnp.where(row == 0, prev_last, rolled)
    carry_ref[0:1, :] = h[tm - 1:tm, :]
    d = h_prev - h
    mu = mu_ref[...]
    xs_ref[0] = (h + d * mu[0:1]).astype(BF16)
    xs_ref[1] = (h + d * mu[2:3]).astype(BF16)
    xs_ref[2] = (h + d * mu[3:4]).astype(BF16)
    xs_ref[3] = (h + d * mu[5:6]).astype(BF16)
    xw = (h + d * mu[1:2]).astype(BF16)
    xa = (h + d * mu[4:5]).astype(BF16)
    zw = w0_ref[...] + _dot(jnp.tanh(_dot(xw, w1_ref[...])).astype(BF16), w2_ref[...])
    lw_ref[...] = (-math.exp(-0.5)) / (1.0 + jnp.exp(-zw))
    za = a0_ref[...] + _dot(_dot(xa, a1_ref[...]).astype(BF16), a2_ref[...])
    ic_ref[...] = 1.0 / (1.0 + jnp.exp(-za))


def _rwkv_prep(x2, nw, mu, w0, w1p, w2p, a0, a1p, a2p, *, batch, seq, tm):
    m, d = x2.shape
    nt = seq // tm
    row_spec = pl.BlockSpec((tm, d), lambda b, t: (b * nt + t, 0))
    vec_spec = pl.BlockSpec((1, d), lambda b, t: (0, 0))

    def full(a):
        return pl.BlockSpec(a.shape, lambda b, t: (0,) * a.ndim)

    return pl.pallas_call(
        _rwkv_prep_kernel,
        grid=(batch, nt),
        in_specs=[row_spec, vec_spec, full(mu), vec_spec, full(w1p), full(w2p), vec_spec, full(a1p), full(a2p)],
        out_specs=[pl.BlockSpec((4, tm, d), lambda b, t: (0, b * nt + t, 0)), row_spec, row_spec],
        out_shape=[jax.ShapeDtypeStruct((4, m, d), BF16),
                   jax.ShapeDtypeStruct((m, d), F32),
                   jax.ShapeDtypeStruct((m, d), F32)],
        scratch_shapes=[pltpu.VMEM((8, d), F32)],
        compiler_params=_cparams(("parallel", "arbitrary")),
        name="rwkv_prep",
    )(x2, nw, mu, w0, w1p, w2p, a0, a1p, a2p)


def _mm_wcast_kernel(a_ref, b_ref, o_ref, w_sc):
    @pl.when(pl.program_id(1) == 0)
    def _():
        w_sc[...] = b_ref[...].astype(BF16)

    o_ref[...] = _dot(a_ref[...], w_sc[...]).astype(o_ref.dtype)


def _matmul_wcast(a, w, *, tm, tn, out_dtype, groups=1):
    m, k = a.shape[-2:]
    n = w.shape[1]
    per_group = (n // groups) // tn
    if groups > 1:
        a_spec = pl.BlockSpec((None, tm, k), lambda j, i: (j // per_group, i, 0))
    else:
        a_spec = pl.BlockSpec((tm, k), lambda j, i: (i, 0))
    return pl.pallas_call(
        _mm_wcast_kernel,
        grid=(n // tn, m // tm),
        in_specs=[a_spec, pl.BlockSpec((k, tn), lambda j, i: (0, j))],
        out_specs=pl.BlockSpec((tm, tn), lambda j, i: (i, j)),
        out_shape=jax.ShapeDtypeStruct((m, n), out_dtype),
        scratch_shapes=[pltpu.VMEM((k, tn), BF16)],
        compiler_params=_cparams(("parallel", "arbitrary")),
        name="proj" if groups > 1 else "matmul",
    )(a, w)


def _out_proj_kernel(a_ref, b_ref, r_ref, nw_ref, *o_refs):
    y = r_ref[...] + _dot(a_ref[...], b_ref[...])
    yn = y * lax.rsqrt(jnp.mean(y * y, axis=-1, keepdims=True) + NORM_EPS) * nw_ref[...]
    if len(o_refs) == 2:
        o_refs[0][...] = y
    o_refs[-1][...] = yn.astype(o_refs[-1].dtype)


def _out_proj(a, w, res, nw, *, tm, keep_sum, norm_dtype):
    m, k = a.shape
    n = w.shape[1]
    row_spec = pl.BlockSpec((tm, n), lambda i: (i, 0))
    out_specs = [row_spec, row_spec] if keep_sum else [row_spec]
    out_shape = [jax.ShapeDtypeStruct((m, n), norm_dtype)]
    if keep_sum:
        out_shape = [jax.ShapeDtypeStruct((m, n), F32)] + out_shape
    return pl.pallas_call(
        _out_proj_kernel,
        grid=(m // tm,),
        in_specs=[pl.BlockSpec((tm, k), lambda i: (i, 0)),
                  pl.BlockSpec((k, n), lambda i: (0, 0)),
                  row_spec,
                  pl.BlockSpec((1, n), lambda i: (0, 0))],
        out_specs=out_specs,
        out_shape=out_shape,
        compiler_params=_cparams(("parallel",)),
        name="out_proj_sum_norm" if keep_sum else "out_proj_norm",
    )(a, w, res, nw)


WKV_GROUP = 8


def _wkv_kernel(r_ref, k_ref, v_ref, g_ref, lw_ref, ic_ref, kk_ref, ka_ref, rk_ref, lnw_ref, lnb_ref,
                o_ref, rm_sc, y0_sc, g_sc, wc_sc, y_sc, *, group):
    seq = r_ref.shape[0]
    ngroups = seq // (CHUNK * group)
    two = 2 * CHUNK

    lane1 = lax.broadcasted_iota(jnp.int32, (1, LANES), 1)
    m0 = (lane1 < HEAD).astype(F32)
    m1 = 1.0 - m0
    row = lax.broadcasted_iota(jnp.int32, (two, two), 0)
    col = lax.broadcasted_iota(jnp.int32, (two, two), 1)
    strict = row > col
    incl = row >= col
    eye = (row == col).astype(F32)
    ones_bd = ((row < HEAD) == (col < HEAD)).astype(BF16)
    tril_c = incl[:CHUNK, :CHUNK].astype(BF16)
    zeros_b = jnp.zeros((two, LANES), BF16)

    kk_p = kk_ref[...]
    ka_p = ka_ref[...]

    def segsum(z):
        return _mm_exact_rhs(z, ones_bd, 2)

    def stack(z):
        return jnp.concatenate([z * m0, z * m1], axis=0)

    def chunk_rows(c):
        return pl.ds(pl.multiple_of(c * CHUNK, CHUNK), CHUNK)

    def phase_a(gi):
        us = range(group)
        rows = [chunk_rows(gi * group + u) for u in us]
        k = [k_ref[rw, :] for rw in rows]
        lw = [lw_ref[rw, :] for rw in rows]
        kkv = [k[u] * kk_p for u in us]
        n2 = [segsum(kkv[u] * kkv[u]) for u in us]
        lin = [_mm_exact_lhs(tril_c, lw[u], 2) for u in us]
        yield
        a_s, r_s, v_s, bkt, wcb, a_all = [], [], [], [], [], []
        for u in us:
            ic = ic_ref[rows[u], :]
            kkn = kkv[u] / jnp.maximum(jnp.sqrt(n2[u]), 1e-12)
            kh = k[u] * (1.0 + (ic - 1.0) * ka_p)
            e_in = jnp.exp(lin[u])
            e_ex = jnp.exp(lin[u] - lw[u])
            e_neg = jnp.exp(-lin[u])
            b_f = stack(kkn * ic * e_neg)
            k_f = stack(kh * e_neg)
            a_s.append(stack(-kkn * e_ex).astype(BF16))
            r_s.append(stack(r_ref[rows[u], :] * e_in).astype(BF16))
            v_s.append(stack(v_ref[rows[u], :]).astype(BF16))
            a_all.append(_dot_nt(jnp.concatenate([a_s[u], r_s[u]], axis=0),
                                 jnp.concatenate([b_f, k_f], axis=0).astype(BF16)))
            bkt.append(jnp.concatenate([b_f.T, k_f.T], axis=1).astype(BF16))
            wcol = jnp.sum(eye * e_in[CHUNK - 1:CHUNK, :], axis=1, keepdims=True)
            wcb.append(jnp.broadcast_to(wcol, (two, LANES)))
        yield
        t_inv, pwb, a_rb, av = [], [], [], []
        for u in us:
            a_ab = jnp.where(strict, a_all[u][:two, :two], 0.0)
            a_ak = jnp.where(strict, a_all[u][:two, two:], 0.0).astype(BF16)
            a_rk = jnp.where(incl, a_all[u][two:, two:], 0.0).astype(BF16)
            a_rb.append(jnp.where(incl, a_all[u][two:, :two], 0.0).astype(BF16))
            av.append(_dot(jnp.concatenate([a_ak, a_rk], axis=0), v_s[u]))
            t_inv.append(eye + a_ab)
            pwb.append(a_ab.astype(BF16))
        yield
        pwb = [_dot(pwb[u], pwb[u]).astype(BF16) for u in us]
        yield
        for _ in range(4):
            both = [_dot(jnp.concatenate([t_inv[u].astype(BF16), pwb[u]], axis=0), pwb[u]) for u in us]
            t_inv = [t_inv[u] + both[u][:two] for u in us]
            pwb = [both[u][two:].astype(BF16) for u in us]
            yield
        t_inv = [t_inv[u] + _dot(t_inv[u].astype(BF16), pwb[u]) for u in us]
        yield
        pqb = [_dot(t_inv[u].astype(BF16),
                    jnp.concatenate([a_s[u], av[u][:two].astype(BF16)], axis=1)).astype(BF16) for u in us]
        yield
        rq = [_dot(a_rb[u], pqb[u]) for u in us]
        mg = [_dot(bkt[u], jnp.concatenate([pqb[u], jnp.concatenate([zeros_b, v_s[u]], axis=1)], axis=0)) for u in us]
        yield
        for u in us:
            y0 = rq[u][:, LANES:] + av[u][two:]
            rm_sc[u] = jnp.concatenate([r_s[u].astype(F32) + rq[u][:, :LANES], wcb[u] * mg[u][:, :LANES]],
                                       axis=0).astype(BF16)
            g_sc[u] = wcb[u] * mg[u][:, LANES:]
            y0_sc[u] = y0[:CHUNK] + y0[CHUNK:]
            wc_sc[u] = wcb[u]

    def state_step(gi, u, h):
        res = _dot(rm_sc[u], h.astype(BF16))
        y_sc[chunk_rows(gi * group + u), :] = res[:CHUNK] + res[CHUNK:two] + y0_sc[u]
        return wc_sc[u] * h + res[two:] + g_sc[u]

    rk_p = rk_ref[...]
    lnw_p = lnw_ref[...]
    lnb_p = lnb_ref[...]
    inv_n = 1.0 / HEAD
    grows = group * CHUNK

    def finish(gi):
        rows = pl.ds(pl.multiple_of(gi * grows, grows), grows)
        y = y_sc[rows, :]
        mean = segsum(y) * inv_n
        yc = y - mean
        var = segsum(yc * yc) * inv_n
        gn = yc * lax.rsqrt(var + GN_EPS) * lnw_p + lnb_p
        kh = k_ref[rows, :] * (1.0 + (ic_ref[rows, :] - 1.0) * ka_p)
        bonus = segsum(r_ref[rows, :] * kh * rk_p) * v_ref[rows, :]
        g = g_ref[rows, :]
        o_ref[rows, :] = ((gn + bonus) * (g / (1.0 + jnp.exp(-g)))).astype(o_ref.dtype)

    def run(a_gi, b_gi, c_gi, h):
        done = 0
        if a_gi is None:
            for u in range(group):
                h = state_step(b_gi, u, h)
        else:
            for i, _ in enumerate(phase_a(a_gi)):
                if b_gi is not None and done < group and i >= 1:
                    h = state_step(b_gi, done, h)
                    done += 1
            assert b_gi is None or done == group
        if c_gi is not None:
            finish(c_gi)
        return h

    run(0, None, None, None)
    h = run(1, 0, None, jnp.zeros((two, LANES), F32))
    h = lax.fori_loop(1, ngroups - 1, lambda gi, h: run(gi + 1, gi, gi - 1, h), h)
    run(None, ngroups - 1, ngroups - 2, h)
    finish(ngroups - 1)


def _wkv(proj, lw, ic, kk, ka, rk, lnw, lnb, *, batch, seq):
    m, n4 = proj.shape
    w = n4 // 4
    nhp = w // LANES
    two = 2 * CHUNK
    group = WKV_GROUP
    assert seq % (CHUNK * group) == 0 and seq // (CHUNK * group) >= 3

    def col_spec(p):
        return pl.BlockSpec((seq, LANES), lambda b, hp: (b, p * nhp + hp))

    par_spec = pl.BlockSpec((1, LANES), lambda b, hp: (0, hp))
    act_spec = pl.BlockSpec((seq, LANES), lambda b, hp: (b, hp))
    return pl.pallas_call(
        functools.partial(_wkv_kernel, group=group),
        grid=(batch, nhp),
        in_specs=[col_spec(0), col_spec(1), col_spec(2), col_spec(3), act_spec, act_spec,
                  par_spec, par_spec, par_spec, par_spec, par_spec],
        out_specs=act_spec,
        out_shape=jax.ShapeDtypeStruct((m, w), BF16),
        scratch_shapes=[pltpu.VMEM((group, 2 * two, LANES), BF16),
                        pltpu.VMEM((group, CHUNK, LANES), F32),
                        pltpu.VMEM((group, two, LANES), F32),
                        pltpu.VMEM((group, two, LANES), F32),
                        pltpu.VMEM((seq, LANES), F32)],
        compiler_params=_cparams(("parallel", "parallel")),
        name="wkv7",
    )(proj, proj, proj, proj, lw, ic, kk, ka, rk, lnw, lnb)


def _attn_kernel(q_ref, k_ref, v_ref, g_ref, lam_ref, sw_ref, o_ref,
                 z_a, z_b, p_a, p_b, acc_sc, ka1_sc, ka2_sc, q1_sc, q2_sc, *, tq, n_heads, lambda_init):
    seq = q_ref.shape[0]
    nq = seq // tq
    tk = tq
    half = 256
    ng = 2 * tq // LANES
    h = pl.program_id(1)
    lane = lax.broadcasted_iota(jnp.int32, (1, LANES), 1)
    slope = jnp.exp2(jnp.full((1, LANES), -8.0 / n_heads, F32) * (h + 1).astype(F32)) * LOG2E
    s_hi = slope.astype(BF16).astype(F32)
    s_lo = slope - s_hi
    s_pat = jnp.where(lane % 2 == 0, s_hi, s_lo)

    row_s = lax.broadcasted_iota(jnp.int32, (seq, LANES), 0) % tk
    lane_s = lax.broadcasted_iota(jnp.int32, (seq, LANES), 1)
    kidx = jnp.where(lane_s % 4 < 2, row_s % half, (row_s // half) * half).astype(F32)
    ks_all = k_ref[...].astype(F32)
    ka1_sc[...] = jnp.where(lane_s < HEAD, ks_all, jnp.where(lane_s < HEAD + 4, kidx, 0.0)).astype(BF16)
    ka2_sc[...] = jnp.where(lane_s >= HEAD, ks_all, jnp.where(lane_s < 4, kidx, 0.0)).astype(BF16)

    tri = (lax.broadcasted_iota(jnp.int32, (LANES, LANES), 0) <= lax.broadcasted_iota(jnp.int32, (LANES, LANES), 1))
    lam = lam_ref[...]
    lam_full = (jnp.exp(jnp.sum(lam[0:1] * lam[1:2], axis=1, keepdims=True))
                - jnp.exp(jnp.sum(lam[2:3] * lam[3:4], axis=1, keepdims=True)) + lambda_init)
    sw_scaled = sw_ref[...] * (1.0 - lambda_init)

    def load_q(qi):
        qf = q_ref[qi * tq:(qi + 1) * tq, :].astype(F32) * (LOG2E / math.sqrt(HEAD))
        q1_sc[...] = jnp.where(lane < HEAD, qf, jnp.where(lane < HEAD + 4, s_pat, 0.0)).astype(BF16)
        q2_sc[...] = jnp.where(lane >= HEAD, qf, jnp.where(lane < 4, s_pat, 0.0)).astype(BF16)

    def scores_to(j, z_ref, diag):
        hq = ng // 4
        for ka_sc, q_sc, g0 in ((ka1_sc, q1_sc, 0), (ka2_sc, q2_sc, ng // 2)):
            if not diag:
                z = _dot_nt(ka_sc[j * tk:(j + 1) * tk, :], q_sc[...])
                for g in range(ng // 2):
                    z_ref[g0 + g] = z[:, g * LANES:(g + 1) * LANES]
            else:
                top = _dot_nt(ka_sc[j * tk:j * tk + tk // 2, :], q_sc[...])
                bot = _dot_nt(ka_sc[j * tk + tk // 2:(j + 1) * tk, :], q_sc[tq // 2:, :])
                for g in range(ng // 2):
                    z_ref[g0 + g, 0:tk // 2, :] = top[:, g * LANES:(g + 1) * LANES]
                for g in range(hq):
                    z_ref[g0 + hq + g, tk // 2:, :] = bot[:, g * LANES:(g + 1) * LANES]

    def accumulate(j, p_ref, alpha, first, diag):
        hq = ng // 4
        if not diag:
            p_all = jnp.concatenate([p_ref[g] for g in range(ng)], axis=1)
            pv = _dot_tn(v_ref[j * tk:(j + 1) * tk, :], p_all)
            parts = [pv[:, g * LANES:(g + 1) * LANES] for g in range(ng)]
        else:
            late = [g for g in range(ng) if g % (ng // 2) >= hq]
            p_top = jnp.concatenate([p_ref[g, 0:tk // 2, :] for g in range(ng)], axis=1)
            p_bot = jnp.concatenate([p_ref[g, tk // 2:, :] for g in late], axis=1)
            pv_top = _dot_tn(v_ref[j * tk:j * tk + tk // 2, :], p_top)
            pv_bot = _dot_tn(v_ref[j * tk + tk // 2:(j + 1) * tk, :], p_bot)
            parts = [pv_top[:, g * LANES:(g + 1) * LANES] for g in range(ng)]
            for n, g in enumerate(late):
                parts[g] = parts[g] + pv_bot[:, n * LANES:(n + 1) * LANES]
        for g in range(ng):
            sl = slice(g * LANES, (g + 1) * LANES)
            acc_sc[g] = parts[g] if first else alpha[:, sl] * acc_sc[g] + parts[g]

    def softmax(qi, j, z_ref, p_ref, m, l):
        c_blk = slope * float((j - qi) * tq)
        ms, ls, alphas = [], [], []
        for g in range(ng):
            sl = slice(g * LANES, (g + 1) * LANES)
            nrows = tk
            if j == qi:
                nrows = (g % (ng // 2)) * LANES + LANES
                fill_to = tk // 2 if nrows <= tk // 2 else tk
                if nrows < fill_to:
                    p_ref[g, nrows:fill_to, :] = jnp.zeros((fill_to - nrows, LANES), BF16)
            diag_tile = jnp.where(tri, z_ref[g, nrows - LANES:nrows, :], NEG_BIG) if j == qi else None
            nplain = nrows - LANES if j == qi else nrows
            mx = diag_tile.max(axis=0, keepdims=True) if diag_tile is not None else None
            if nplain:
                mp = jnp.max(z_ref[g, 0:nplain, :], axis=0, keepdims=True)
                mx = mp if mx is None else jnp.maximum(mx, mp)
            mg = m[:, sl]
            mn = jnp.maximum(mg, mx + c_blk)
            alpha = jnp.exp2(mg - mn)
            shift = mn - c_blk
            lsum = None
            for r0 in range(0, nrows, LANES):
                zt = diag_tile if (j == qi and r0 == nplain) else z_ref[g, r0:r0 + LANES, :]
                pt = jnp.exp2(zt - shift)
                part = jnp.sum(pt, axis=0, keepdims=True)
                lsum = part if lsum is None else lsum + part
                p_ref[g, r0:r0 + LANES, :] = pt.astype(BF16)
            ls.append(alpha * l[:, sl] + lsum)
            ms.append(mn)
            alphas.append(alpha)
        return jnp.concatenate(ms, axis=1), jnp.concatenate(ls, axis=1), jnp.concatenate(alphas, axis=1)

    def finalize(qi, l):
        on = jnp.concatenate([acc_sc[g] for g in range(ng)], axis=1) / l
        ot = on[:, :tq] - lam_full * on[:, tq:]
        ot = ot * lax.rsqrt(jnp.mean(ot * ot, axis=0, keepdims=True) + SUBLN_EPS)
        g = g_ref[qi * tq:(qi + 1) * tq, :].astype(F32)
        o_ref[qi * tq:(qi + 1) * tq, :] = (ot.T * sw_scaled * (g / (1.0 + jnp.exp(-g)))).astype(o_ref.dtype)

    items = [(qi, j) for qi in range(nq) for j in range(qi + 1)]
    zs, ps = (z_a, z_b), (p_a, p_b)
    load_q(0)
    scores_to(0, z_a, True)
    m = l = alpha = None
    for n, (qi, j) in enumerate(items):
        if n > 0:
            pqi, pj = items[n - 1]
            accumulate(pj, ps[(n - 1) % 2], alpha, pj == 0, pj == pqi)
            if pqi != qi:
                finalize(pqi, l)
        if j == 0:
            m = jnp.full((1, 2 * tq), NEG_BIG, F32)
            l = jnp.zeros((1, 2 * tq), F32)
        if n + 1 < len(items):
            nqi, nj = items[n + 1]
            if nqi != qi:
                load_q(nqi)
            scores_to(nj, zs[(n + 1) % 2], nj == nqi)
        m, l, alpha = softmax(qi, j, zs[n % 2], ps[n % 2], m, l)
    pqi, pj = items[-1]
    accumulate(pj, ps[(len(items) - 1) % 2], alpha, pj == 0, pj == pqi)
    finalize(pqi, l)


def _diff_attn(qkvg, lam, sw, *, batch, seq, tq, lambda_init):
    m, n4 = qkvg.shape
    w = n4 // 4
    nh = w // LANES
    ngl = 2 * tq // LANES

    def col_spec(p):
        return pl.BlockSpec((seq, LANES), lambda b, h: (b, p * nh + h))

    return pl.pallas_call(
        functools.partial(_attn_kernel, tq=tq, n_heads=nh, lambda_init=lambda_init),
        grid=(batch, nh),
        in_specs=[col_spec(0), col_spec(1), col_spec(2), col_spec(3),
                  pl.BlockSpec(lam.shape, lambda b, h: (0, 0)),
                  pl.BlockSpec((1, LANES), lambda b, h: (0, 0))],
        out_specs=pl.BlockSpec((seq, LANES), lambda b, h: (b, h)),
        out_shape=jax.ShapeDtypeStruct((m, w), BF16),
        scratch_shapes=[pltpu.VMEM((ngl, tq, LANES), F32), pltpu.VMEM((ngl, tq, LANES), F32),
                        pltpu.VMEM((ngl, tq, LANES), BF16), pltpu.VMEM((ngl, tq, LANES), BF16),
                        pltpu.VMEM((ngl, LANES, LANES), F32),
                        pltpu.VMEM((seq, LANES), BF16), pltpu.VMEM((seq, LANES), BF16),
                        pltpu.VMEM((tq, LANES), BF16), pltpu.VMEM((tq, LANES), BF16)],
        compiler_params=_cparams(("parallel", "parallel")),
        name="diff_attn",
    )(qkvg, qkvg, qkvg, qkvg, lam, sw)


def kernel(x, norm_w, rwkv_mu, rwkv_w_in, rwkv_w0, rwkv_w1, rwkv_w2, rwkv_a0, rwkv_a1, rwkv_a2, rwkv_k_k, rwkv_k_a, rwkv_r_k, rwkv_ln_w, rwkv_ln_b, rwkv_w_out, diff_w_in, diff_lambda, diff_subln_w, diff_w_out, final_norm_w):
    batch, seq, d = x.shape
    m = batch * seq
    assert d % LANES == 0 and seq % CHUNK == 0
    assert norm_w.shape[0] == 2 and rwkv_mu.shape[0] == 1 and diff_w_in.shape[0] == 1
    tm_big = min(1024, m)
    tn_big = min(1024, d)
    tm_prep = min(256, seq)
    tm_last = min(512, m)
    tq = min(512, seq)

    x2 = x.reshape(m, d)
    row = lambda a: a.reshape(1, -1)
    lora = rwkv_w1.shape[-1]
    pad_c = lambda a: jnp.pad(a, ((0, 0), (0, LORA_PAD - lora))).astype(BF16)
    pad_r = lambda a: jnp.pad(a, ((0, LORA_PAD - lora), (0, 0))).astype(BF16)

    xs, lw, ic = _rwkv_prep(x2, row(norm_w[0]), rwkv_mu[0], row(rwkv_w0[0]), pad_c(rwkv_w1[0]), pad_r(rwkv_w2[0]),
                            row(rwkv_a0[0]), pad_c(rwkv_a1[0]), pad_r(rwkv_a2[0]),
                            batch=batch, seq=seq, tm=tm_prep)
    proj = _matmul_wcast(xs, rwkv_w_in[0], tm=tm_big, tn=tn_big, out_dtype=F32, groups=4)
    yg = _wkv(proj, lw, ic, row(rwkv_k_k[0]), row(rwkv_k_a[0]), row(rwkv_r_k[0]), row(rwkv_ln_w[0]),
              row(rwkv_ln_b[0]), batch=batch, seq=seq)
    x1, h1 = _out_proj(yg, rwkv_w_out[0].astype(BF16), x2, row(norm_w[1]), tm=tm_last, keep_sum=True,
                       norm_dtype=BF16)

    lambda_init = 0.8 - 0.6 * math.exp(-0.3 * 1)
    qkvg = _matmul_wcast(h1, diff_w_in[0], tm=tm_big, tn=tn_big, out_dtype=BF16)
    att = _diff_attn(qkvg, diff_lambda[0], row(diff_subln_w[0]), batch=batch, seq=seq, tq=tq,
                     lambda_init=lambda_init)
    (out,) = _out_proj(att, diff_w_out[0].astype(BF16), x1, row(final_norm_w), tm=tm_last, keep_sum=False,
                       norm_dtype=F32)
    return out.reshape(batch, seq, d)
```

```python
import functools
import math

import jax
import jax.numpy as jnp
from jax import lax
from jax.experimental import pallas as pl
from jax.experimental.pallas import tpu as pltpu

F32 = jnp.float32
BF16 = jnp.bfloat16

NORM_EPS = 1e-6
GN_EPS = 64e-5
SUBLN_EPS = 1e-5
HEAD = 64
LANES = 128
CHUNK = 64
LORA_PAD = 128
VMEM_LIMIT = 48 * 1024 * 1024
NEG_BIG = -1e30
LOG2E = 1.0 / math.log(2.0)


def _cparams(sem):
    return pltpu.CompilerParams(dimension_semantics=sem, vmem_limit_bytes=VMEM_LIMIT)


def _dot(a, b):
    return jnp.dot(a, b, preferred_element_type=F32)


def _dot_nt(a, b):
    return lax.dot_general(a, b, (((1,), (1,)), ((), ())), preferred_element_type=F32)


def _dot_tn(a, b):
    return lax.dot_general(a, b, (((0,), (0,)), ((), ())), preferred_element_type=F32)


def _split2(x):
    hi = x.astype(BF16)
    lo = (x - hi.astype(F32)).astype(BF16)
    return hi, lo


def _mm(a, b, passes, nt=False):
    d = _dot_nt if nt else _dot
    if passes == 1:
        return d(a.astype(BF16), b.astype(BF16))
    ah, al = _split2(a)
    bh, bl = _split2(b)
    return d(ah, bh) + (d(al, bh) + d(ah, bl))


def _mm_exact_rhs(a, e_bf16, parts):
    acc = None
    rem = a
    for _ in range(parts):
        piece = rem.astype(BF16)
        term = _dot(piece, e_bf16)
        acc = term if acc is None else acc + term
        rem = rem - piece.astype(F32)
    return acc


def _mm_exact_lhs(e_bf16, a, parts):
    acc = None
    rem = a
    for _ in range(parts):
        piece = rem.astype(BF16)
        term = _dot(e_bf16, piece)
        acc = term if acc is None else acc + term
        rem = rem - piece.astype(F32)
    return acc


def _rwkv_prep_kernel(x_ref, nw_ref, mu_ref, w0_ref, w1_ref, w2_ref, a0_ref, a1_ref, a2_ref,
                      xs_ref, lw_ref, ic_ref, carry_ref):
    t = pl.program_id(1)
    x = x_ref[...]
    tm = x.shape[0]
    h = x * lax.rsqrt(jnp.mean(x * x, axis=-1, keepdims=True) + NORM_EPS) * nw_ref[...]

    @pl.when(t == 0)
    def _():
        carry_ref[...] = jnp.zeros_like(carry_ref)

    prev_last = carry_ref[0:1, :]
    rolled = pltpu.roll(h, 1, axis=0)
    row = lax.broadcasted_iota(jnp.int32, h.shape, 0)
    h_prev = jnp.where(row == 0, prev_last, rolled)
    carry_ref[0:1, :] = h[tm - 1:tm, :]
    d = h_prev - h
    mu = mu_ref[...]
    xs_ref[0] = (h + d * mu[0:1]).astype(BF16)
    xs_ref[1] = (h + d * mu[2:3]).astype(BF16)
    xs_ref[2] = (h + d * mu[3:4]).astype(BF16)
    xs_ref[3] = (h + d * mu[5:6]).astype(BF16)
    xw = (h + d * mu[1:2]).astype(BF16)
    xa = (h + d * mu[4:5]).astype(BF16)
    zw = w0_ref[...] + _dot(jnp.tanh(_dot(xw, w1_ref[...])).astype(BF16), w2_ref[...])
    lw_ref[...] = (-math.exp(-0.5)) / (1.0 + jnp.exp(-zw))
    za = a0_ref[...] + _dot(_dot(xa, a1_ref[...]).astype(BF16), a2_ref[...])
    ic_ref[...] = 1.0 / (1.0 + jnp.exp(-za))


def _rwkv_prep(x2, nw, mu, w0, w1p, w2p, a0, a1p, a2p, *, batch, seq, tm):
    m, d = x2.shape
    nt = seq // tm
    row_spec = pl.BlockSpec((tm, d), lambda b, t: (b * nt + t, 0))
    vec_spec = pl.BlockSpec((1, d), lambda b, t: (0, 0))

    def full(a):
        return pl.BlockSpec(a.shape, lambda b, t: (0,) * a.ndim)

    return pl.pallas_call(
        _rwkv_prep_kernel,
        grid=(batch, nt),
        in_specs=[row_spec, vec_spec, full(mu), vec_spec, full(w1p), full(w2p), vec_spec, full(a1p), full(a2p)],
        out_specs=[pl.BlockSpec((4, tm, d), lambda b, t: (0, b * nt + t, 0)), row_spec, row_spec],
        out_shape=[jax.ShapeDtypeStruct((4, m, d), BF16),
                   jax.ShapeDtypeStruct((m, d), F32),
                   jax.ShapeDtypeStruct((m, d), F32)],
        scratch_shapes=[pltpu.VMEM((8, d), F32)],
        compiler_params=_cparams(("parallel", "arbitrary")),
        name="rwkv_prep",
    )(x2, nw, mu, w0, w1p, w2p, a0, a1p, a2p)


def _mm_wcast_kernel(a_ref, b_ref, o_ref, w_sc):
    @pl.when(pl.program_id(1) == 0)
    def _():
        w_sc[...] = b_ref[...].astype(BF16)

    o_ref[...] = _dot(a_ref[...], w_sc[...]).astype(o_ref.dtype)


def _matmul_wcast(a, w, *, tm, tn, out_dtype, groups=1):
    m, k = a.shape[-2:]
    n = w.shape[1]
    per_group = (n // groups) // tn
    if groups > 1:
        a_spec = pl.BlockSpec((None, tm, k), lambda j, i: (j // per_group, i, 0))
    else:
        a_spec = pl.BlockSpec((tm, k), lambda j, i: (i, 0))
    return pl.pallas_call(
        _mm_wcast_kernel,
        grid=(n // tn, m // tm),
        in_specs=[a_spec, pl.BlockSpec((k, tn), lambda j, i: (0, j))],
        out_specs=pl.BlockSpec((tm, tn), lambda j, i: (i, j)),
        out_shape=jax.ShapeDtypeStruct((m, n), out_dtype),
        scratch_shapes=[pltpu.VMEM((k, tn), BF16)],
        compiler_params=_cparams(("parallel", "arbitrary")),
        name="proj" if groups > 1 else "matmul",
    )(a, w)


def _out_proj_kernel(a_ref, b_ref, r_ref, nw_ref, *o_refs):
    y = r_ref[...] + _dot(a_ref[...], b_ref[...])
    yn = y * lax.rsqrt(jnp.mean(y * y, axis=-1, keepdims=True) + NORM_EPS) * nw_ref[...]
    if len(o_refs) == 2:
        o_refs[0][...] = y
    o_refs[-1][...] = yn.astype(o_refs[-1].dtype)


def _out_proj(a, w, res, nw, *, tm, keep_sum, norm_dtype):
    m, k = a.shape
    n = w.shape[1]
    row_spec = pl.BlockSpec((tm, n), lambda i: (i, 0))
    out_specs = [row_spec, row_spec] if keep_sum else [row_spec]
    out_shape = [jax.ShapeDtypeStruct((m, n), norm_dtype)]
    if keep_sum:
        out_shape = [jax.ShapeDtypeStruct((m, n), F32)] + out_shape
    return pl.pallas_call(
        _out_proj_kernel,
        grid=(m // tm,),
        in_specs=[pl.BlockSpec((tm, k), lambda i: (i, 0)),
                  pl.BlockSpec((k, n), lambda i: (0, 0)),
                  row_spec,
                  pl.BlockSpec((1, n), lambda i: (0, 0))],
        out_specs=out_specs,
        out_shape=out_shape,
        compiler_params=_cparams(("parallel",)),
        name="out_proj_sum_norm" if keep_sum else "out_proj_norm",
    )(a, w, res, nw)


WKV_GROUP = 8
WKV_STREAMS = 2


def _wkv_kernel(r_ref, k_ref, v_ref, g_ref, lw_ref, ic_ref, kk_ref, ka_ref, rk_ref, lnw_ref, lnb_ref,
                o_ref, rm_sc, y0_sc, g_sc, wc_sc, y_sc, *, group):
    seq = r_ref.shape[0]
    streams = r_ref.shape[1] // LANES
    ngroups = seq // (CHUNK * group)
    two = 2 * CHUNK
    inst = [(u, st) for u in range(group) for st in range(streams)]
    ninst = len(inst)

    def lanes(st):
        return slice(st * LANES, (st + 1) * LANES)

    lane1 = lax.broadcasted_iota(jnp.int32, (1, LANES), 1)
    m0 = (lane1 < HEAD).astype(F32)
    m1 = 1.0 - m0
    row2 = lax.broadcasted_iota(jnp.int32, (two, two), 0)
    col2 = lax.broadcasted_iota(jnp.int32, (two, two), 1)
    eye2 = (row2 == col2).astype(F32)
    same_head = (row2 < HEAD) == (col2 < HEAD)
    ones_bd = same_head.astype(BF16)
    row1 = lax.broadcasted_iota(jnp.int32, (CHUNK, LANES), 0)
    col1 = lax.broadcasted_iota(jnp.int32, (CHUNK, LANES), 1) % CHUNK
    strict = row1 > col1
    incl = row1 >= col1
    eye1 = (row1 == col1).astype(F32)
    tril_c = incl[:, :CHUNK].astype(BF16)
    zeros_b = jnp.zeros((CHUNK, LANES), BF16)

    kk_p = [kk_ref[:, lanes(st)] for st in range(streams)]
    ka_p = [ka_ref[:, lanes(st)] for st in range(streams)]

    def segsum(z, parts):
        return _mm_exact_rhs(z, ones_bd, parts)

    m0b = m0.astype(BF16)
    m1b = m1.astype(BF16)

    def bdiag(z):
        zb = z.astype(BF16)
        return jnp.concatenate([zb * m0b, zb * m1b], axis=0)

    def chunk_rows(c):
        return pl.ds(pl.multiple_of(c * CHUNK, CHUNK), CHUNK)

    def phase_a(gi):
        us = range(ninst)
        rows = [chunk_rows(gi * group + u) for u, _ in inst]
        cols = [lanes(st) for _, st in inst]
        k = [k_ref[rows[i], cols[i]] for i in us]
        lw = [lw_ref[rows[i], cols[i]] for i in us]
        kkv = [k[i] * kk_p[inst[i][1]] for i in us]
        n2 = [segsum(kkv[u] * kkv[u], 2) for u in us]
        lin = [_mm_exact_lhs(tril_c, lw[u], 2) for u in us]
        yield
        a_n, r_n, v_b, bk_n, wcb, a_all = [], [], [], [], [], []
        for u in us:
            ic = ic_ref[rows[u], cols[u]]
            kkn = kkv[u] / jnp.maximum(jnp.sqrt(n2[u]), 1e-12)
            kh = k[u] * (1.0 + (ic - 1.0) * ka_p[inst[u][1]])
            e_in = jnp.exp(lin[u])
            e_ex = jnp.exp(lin[u] - lw[u])
            e_neg = jnp.exp(-lin[u])
            b_f = kkn * ic * e_neg
            k_f = kh * e_neg
            a_f = -kkn * e_ex
            a_n.append(a_f)
            r_n.append(r_ref[rows[u], cols[u]] * e_in)
            v_b.append(bdiag(v_ref[rows[u], cols[u]]))
            a_all.append(_dot_nt(jnp.concatenate([a_f, r_n[u]], axis=0).astype(BF16),
                                 jnp.concatenate([bdiag(b_f), bdiag(k_f)], axis=0)))
            bk_n.append(jnp.concatenate([b_f, k_f], axis=0).astype(BF16))
            wcol = jnp.sum(eye2 * e_in[CHUNK - 1:CHUNK, :], axis=1, keepdims=True)
            wcb.append(jnp.broadcast_to(wcol, (two, LANES)))
        yield
        t_inv, pw, a_rb, av = [], [], [], []
        for u in us:
            a_ab = jnp.where(strict, a_all[u][:CHUNK, :LANES], 0.0)
            a_ak = jnp.where(strict, a_all[u][:CHUNK, LANES:], 0.0)
            a_rk = jnp.where(incl, a_all[u][CHUNK:, LANES:], 0.0)
            a_rb.append(jnp.where(incl, a_all[u][CHUNK:, :LANES], 0.0).astype(BF16))
            av.append(_dot(jnp.concatenate([a_ak, a_rk], axis=0).astype(BF16), v_b[u]))
            t_inv.append(eye1 + a_ab)
            pw.append(a_ab)
        yield
        pw = [_dot(pw[u].astype(BF16), bdiag(pw[u])) for u in us]
        yield
        for _ in range(4):
            both = [_dot(jnp.concatenate([t_inv[u], pw[u]], axis=0).astype(BF16), bdiag(pw[u])) for u in us]
            t_inv = [t_inv[u] + both[u][:CHUNK] for u in us]
            pw = [both[u][CHUNK:] for u in us]
            yield
        t_inv = [t_inv[u] + _dot(t_inv[u].astype(BF16), bdiag(pw[u])) for u in us]
        yield
        pq = [_dot(t_inv[u].astype(BF16),
                   jnp.concatenate([bdiag(a_n[u]), bdiag(av[u][:CHUNK])], axis=1)) for u in us]
        yield
        rq = [_dot(a_rb[u], jnp.concatenate([bdiag(pq[u][:, :LANES]), bdiag(pq[u][:, LANES:])], axis=1))
              for u in us]
        mg = [_dot_tn(bk_n[u], jnp.concatenate([pq[u].astype(BF16),
                                                jnp.concatenate([zeros_b, v_b[u][:CHUNK] + v_b[u][CHUNK:]], axis=1)],
                                               axis=0)) for u in us]
        yield
        for u in us:
            m_bd = jnp.where(same_head, mg[u][:, :LANES], 0.0) * wcb[u]
            rm_sc[u] = jnp.concatenate([r_n[u] + rq[u][:, :LANES], m_bd], axis=0).astype(BF16)
            g_sc[u] = jnp.where(same_head, mg[u][:, LANES:], 0.0) * wcb[u]
            y0_sc[u] = rq[u][:, LANES:] + av[u][CHUNK:]
            wc_sc[u] = wcb[u]

    def state_step(gi, i, hs):
        u, st = inst[i]
        res = _dot(rm_sc[i], hs[st].astype(BF16))
        y_sc[chunk_rows(gi * group + u), lanes(st)] = res[:CHUNK] + y0_sc[i]
        return tuple(wc_sc[i] * hs[st] + res[CHUNK:] + g_sc[i] if t == st else hs[t] for t in range(streams))

    inv_n = 1.0 / HEAD
    grows = group * CHUNK

    def finish(gi):
        rows = pl.ds(pl.multiple_of(gi * grows, grows), grows)
        for st in range(streams):
            sl = lanes(st)
            y = y_sc[rows, sl]
            mean = segsum(y, 1) * inv_n
            yc = y - mean
            var = segsum(yc * yc, 1) * inv_n
            gn = yc * lax.rsqrt(var + GN_EPS) * lnw_ref[:, sl] + lnb_ref[:, sl]
            kh = k_ref[rows, sl] * (1.0 + (ic_ref[rows, sl] - 1.0) * ka_p[st])
            bonus = segsum(r_ref[rows, sl] * kh * rk_ref[:, sl], 1) * v_ref[rows, sl]
            g = g_ref[rows, sl]
            o_ref[rows, sl] = ((gn + bonus) * (g / (1.0 + jnp.exp(-g)))).astype(o_ref.dtype)

    def run(a_gi, b_gi, c_gi, hs):
        done = 0
        if a_gi is None:
            for i in range(ninst):
                hs = state_step(b_gi, i, hs)
        else:
            per_slot = -(-ninst // 9)
            for i, _ in enumerate(phase_a(a_gi)):
                for _ in range(per_slot if (b_gi is not None and i >= 1) else 0):
                    if done < ninst:
                        hs = state_step(b_gi, done, hs)
                        done += 1
            assert b_gi is None or done == ninst
        if c_gi is not None:
            finish(c_gi)
        return hs

    run(0, None, None, None)
    hs = run(1, 0, None, tuple(jnp.zeros((two, LANES), F32) for _ in range(streams)))
    if ngroups > 2:
        hs = lax.fori_loop(1, ngroups - 1, lambda gi, hs: run(gi + 1, gi, gi - 1, hs), hs)
    run(None, ngroups - 1, ngroups - 2, hs)
    finish(ngroups - 1)


def _wkv(proj, lw, ic, kk, ka, rk, lnw, lnb, *, batch, seq):
    m, n4 = proj.shape
    w = n4 // 4
    group = WKV_GROUP
    streams = WKV_STREAMS if (w // LANES) % WKV_STREAMS == 0 else 1
    wide = streams * LANES
    nblk = w // wide
    two = 2 * CHUNK
    ninst = group * streams
    assert seq % (CHUNK * group) == 0 and seq // (CHUNK * group) >= 2

    def col_spec(p):
        return pl.BlockSpec((seq, wide), lambda b, hp: (b, p * nblk + hp))

    par_spec = pl.BlockSpec((1, wide), lambda b, hp: (0, hp))
    act_spec = pl.BlockSpec((seq, wide), lambda b, hp: (b, hp))
    return pl.pallas_call(
        functools.partial(_wkv_kernel, group=group),
        grid=(batch, nblk),
        in_specs=[col_spec(0), col_spec(1), col_spec(2), col_spec(3), act_spec, act_spec,
                  par_spec, par_spec, par_spec, par_spec, par_spec],
        out_specs=act_spec,
        out_shape=jax.ShapeDtypeStruct((m, w), BF16),
        scratch_shapes=[pltpu.VMEM((ninst, CHUNK + two, LANES), BF16),
                        pltpu.VMEM((ninst, CHUNK, LANES), F32),
                        pltpu.VMEM((ninst, two, LANES), F32),
                        pltpu.VMEM((ninst, two, LANES), F32),
                        pltpu.VMEM((seq, wide), F32)],
        compiler_params=_cparams(("parallel", "parallel")),
        name="wkv7",
    )(proj, proj, proj, proj, lw, ic, kk, ka, rk, lnw, lnb)


def _attn_kernel(q_ref, k_ref, v_ref, g_ref, lam_ref, sw_ref, o_ref,
                 z_a, z_b, p_a, p_b, acc_sc, ka1_sc, ka2_sc, q1_sc, q2_sc, *, tq, n_heads, lambda_init):
    seq = q_ref.shape[0]
    nq = seq // tq
    tk = tq
    half = 256
    ng = 2 * tq // LANES
    h = pl.program_id(1)
    lane = lax.broadcasted_iota(jnp.int32, (1, LANES), 1)
    slope = jnp.exp2(jnp.full((1, LANES), -8.0 / n_heads, F32) * (h + 1).astype(F32)) * LOG2E
    s_hi = slope.astype(BF16).astype(F32)
    s_lo = slope - s_hi
    s_pat = jnp.where(lane % 2 == 0, s_hi, s_lo)

    row_s = lax.broadcasted_iota(jnp.int32, (seq, LANES), 0) % tk
    lane_s = lax.broadcasted_iota(jnp.int32, (seq, LANES), 1)
    kidx = jnp.where(lane_s % 4 < 2, row_s % half, (row_s // half) * half).astype(F32)
    ks_all = k_ref[...].astype(F32)
    ka1_sc[...] = jnp.where(lane_s < HEAD, ks_all, jnp.where(lane_s < HEAD + 4, kidx, 0.0)).astype(BF16)
    ka2_sc[...] = jnp.where(lane_s >= HEAD, ks_all, jnp.where(lane_s < 4, kidx, 0.0)).astype(BF16)

    tri = (lax.broadcasted_iota(jnp.int32, (LANES, LANES), 0) <= lax.broadcasted_iota(jnp.int32, (LANES, LANES), 1))
    lam = lam_ref[...]
    lam_full = (jnp.exp(jnp.sum(lam[0:1] * lam[1:2], axis=1, keepdims=True))
                - jnp.exp(jnp.sum(lam[2:3] * lam[3:4], axis=1, keepdims=True)) + lambda_init)
    sw_scaled = sw_ref[...] * (1.0 - lambda_init)

    def load_q(qi):
        qf = q_ref[qi * tq:(qi + 1) * tq, :].astype(F32) * (LOG2E / math.sqrt(HEAD))
        q1_sc[...] = jnp.where(lane < HEAD, qf, jnp.where(lane < HEAD + 4, s_pat, 0.0)).astype(BF16)
        q2_sc[...] = jnp.where(lane >= HEAD, qf, jnp.where(lane < 4, s_pat, 0.0)).astype(BF16)

    def scores_to(j, z_ref, diag):
        hq = ng // 4
        for ka_sc, q_sc, g0 in ((ka1_sc, q1_sc, 0), (ka2_sc, q2_sc, ng // 2)):
            if not diag:
                z = _dot_nt(ka_sc[j * tk:(j + 1) * tk, :], q_sc[...])
                for g in range(ng // 2):
                    z_ref[g0 + g] = z[:, g * LANES:(g + 1) * LANES]
            else:
                top = _dot_nt(ka_sc[j * tk:j * tk + tk // 2, :], q_sc[...])
                bot = _dot_nt(ka_sc[j * tk + tk // 2:(j + 1) * tk, :], q_sc[tq // 2:, :])
                for g in range(ng // 2):
                    z_ref[g0 + g, 0:tk // 2, :] = top[:, g * LANES:(g + 1) * LANES]
                for g in range(hq):
                    z_ref[g0 + hq + g, tk // 2:, :] = bot[:, g * LANES:(g + 1) * LANES]

    def accumulate(j, p_ref, alpha, first, diag):
        hq = ng // 4
        if not diag:
            p_all = jnp.concatenate([p_ref[g] for g in range(ng)], axis=1)
            pv = _dot_tn(v_ref[j * tk:(j + 1) * tk, :], p_all)
            parts = [pv[:, g * LANES:(g + 1) * LANES] for g in range(ng)]
        else:
            late = [g for g in range(ng) if g % (ng // 2) >= hq]
            p_top = jnp.concatenate([p_ref[g, 0:tk // 2, :] for g in range(ng)], axis=1)
            p_bot = jnp.concatenate([p_ref[g, tk // 2:, :] for g in late], axis=1)
            pv_top = _dot_tn(v_ref[j * tk:j * tk + tk // 2, :], p_top)
            pv_bot = _dot_tn(v_ref[j * tk + tk // 2:(j + 1) * tk, :], p_bot)
            parts = [pv_top[:, g * LANES:(g + 1) * LANES] for g in range(ng)]
            for n, g in enumerate(late):
                parts[g] = parts[g] + pv_bot[:, n * LANES:(n + 1) * LANES]
        for g in range(ng):
            sl = slice(g * LANES, (g + 1) * LANES)
            acc_sc[g] = parts[g] if first else alpha[:, sl] * acc_sc[g] + parts[g]

    def softmax(qi, j, z_ref, p_ref, m, l):
        c_blk = slope * float((j - qi) * tq)
        ms, ls, alphas = [], [], []
        for g in range(ng):
            sl = slice(g * LANES, (g + 1) * LANES)
            nrows = tk
            if j == qi:
                nrows = (g % (ng // 2)) * LANES + LANES
                fill_to = tk // 2 if nrows <= tk // 2 else tk
                if nrows < fill_to:
                    p_ref[g, nrows:fill_to, :] = jnp.zeros((fill_to - nrows, LANES), BF16)
            diag_tile = jnp.where(tri, z_ref[g, nrows - LANES:nrows, :], NEG_BIG) if j == qi else None
            nplain = nrows - LANES if j == qi else nrows
            mx = diag_tile.max(axis=0, keepdims=True) if diag_tile is not None else None
            if nplain:
                mp = jnp.max(z_ref[g, 0:nplain, :], axis=0, keepdims=True)
                mx = mp if mx is None else jnp.maximum(mx, mp)
            mg = m[:, sl]
            mn = jnp.maximum(mg, mx + c_blk)
            alpha = jnp.exp2(mg - mn)
            shift = mn - c_blk
            lsum = None
            for r0 in range(0, nrows, LANES):
                zt = diag_tile if (j == qi and r0 == nplain) else z_ref[g, r0:r0 + LANES, :]
                pt = jnp.exp2(zt - shift)
                part = jnp.sum(pt, axis=0, keepdims=True)
                lsum = part if lsum is None else lsum + part
                p_ref[g, r0:r0 + LANES, :] = pt.astype(BF16)
            ls.append(alpha * l[:, sl] + lsum)
            ms.append(mn)
            alphas.append(alpha)
        return jnp.concatenate(ms, axis=1), jnp.concatenate(ls, axis=1), jnp.concatenate(alphas, axis=1)

    def finalize(qi, l):
        on = jnp.concatenate([acc_sc[g] for g in range(ng)], axis=1) / l
        ot = on[:, :tq] - lam_full * on[:, tq:]
        ot = ot * lax.rsqrt(jnp.mean(ot * ot, axis=0, keepdims=True) + SUBLN_EPS)
        g = g_ref[qi * tq:(qi + 1) * tq, :].astype(F32)
        o_ref[qi * tq:(qi + 1) * tq, :] = (ot.T * sw_scaled * (g / (1.0 + jnp.exp(-g)))).astype(o_ref.dtype)

    items = [(qi, j) for qi in range(nq) for j in range(qi + 1)]
    zs, ps = (z_a, z_b), (p_a, p_b)
    load_q(0)
    scores_to(0, z_a, True)
    m = l = alpha = None
    for n, (qi, j) in enumerate(items):
        if n > 0:
            pqi, pj = items[n - 1]
            accumulate(pj, ps[(n - 1) % 2], alpha, pj == 0, pj == pqi)
            if pqi != qi:
                finalize(pqi, l)
        if j == 0:
            m = jnp.full((1, 2 * tq), NEG_BIG, F32)
            l = jnp.zeros((1, 2 * tq), F32)
        if n + 1 < len(items):
            nqi, nj = items[n + 1]
            if nqi != qi:
                load_q(nqi)
            scores_to(nj, zs[(n + 1) % 2], nj == nqi)
        m, l, alpha = softmax(qi, j, zs[n % 2], ps[n % 2], m, l)
    pqi, pj = items[-1]
    accumulate(pj, ps[(len(items) - 1) % 2], alpha, pj == 0, pj == pqi)
    finalize(pqi, l)


def _diff_attn(qkvg, lam, sw, *, batch, seq, tq, lambda_init):
    m, n4 = qkvg.shape
    w = n4 // 4
    nh = w // LANES
    ngl = 2 * tq // LANES

    def col_spec(p):
        return pl.BlockSpec((seq, LANES), lambda b, h: (b, p * nh + h))

    return pl.pallas_call(
        functools.partial(_attn_kernel, tq=tq, n_heads=nh, lambda_init=lambda_init),
        grid=(batch, nh),
        in_specs=[col_spec(0), col_spec(1), col_spec(2), col_spec(3),
                  pl.BlockSpec(lam.shape, lambda b, h: (0, 0)),
                  pl.BlockSpec((1, LANES), lambda b, h: (0, 0))],
        out_specs=pl.BlockSpec((seq, LANES), lambda b, h: (b, h)),
        out_shape=jax.ShapeDtypeStruct((m, w), BF16),
        scratch_shapes=[pltpu.VMEM((ngl, tq, LANES), F32), pltpu.VMEM((ngl, tq, LANES), F32),
                        pltpu.VMEM((ngl, tq, LANES), BF16), pltpu.VMEM((ngl, tq, LANES), BF16),
                        pltpu.VMEM((ngl, LANES, LANES), F32),
                        pltpu.VMEM((seq, LANES), BF16), pltpu.VMEM((seq, LANES), BF16),
                        pltpu.VMEM((tq, LANES), BF16), pltpu.VMEM((tq, LANES), BF16)],
        compiler_params=_cparams(("parallel", "parallel")),
        name="diff_attn",
    )(qkvg, qkvg, qkvg, qkvg, lam, sw)


def kernel(x, norm_w, rwkv_mu, rwkv_w_in, rwkv_w0, rwkv_w1, rwkv_w2, rwkv_a0, rwkv_a1, rwkv_a2, rwkv_k_k, rwkv_k_a, rwkv_r_k, rwkv_ln_w, rwkv_ln_b, rwkv_w_out, diff_w_in, diff_lambda, diff_subln_w, diff_w_out, final_norm_w):
    batch, seq, d = x.shape
    m = batch * seq
    assert d % LANES == 0 and seq % CHUNK == 0
    assert norm_w.shape[0] == 2 and rwkv_mu.shape[0] == 1 and diff_w_in.shape[0] == 1
    tm_big = min(1024, m)
    tn_big = min(1024, d)
    tm_prep = min(256, seq)
    tm_last = min(512, m)
    tq = min(512, seq)

    x2 = x.reshape(m, d)
    row = lambda a: a.reshape(1, -1)
    lora = rwkv_w1.shape[-1]
    pad_c = lambda a: jnp.pad(a, ((0, 0), (0, LORA_PAD - lora))).astype(BF16)
    pad_r = lambda a: jnp.pad(a, ((0, LORA_PAD - lora), (0, 0))).astype(BF16)

    xs, lw, ic = _rwkv_prep(x2, row(norm_w[0]), rwkv_mu[0], row(rwkv_w0[0]), pad_c(rwkv_w1[0]), pad_r(rwkv_w2[0]),
                            row(rwkv_a0[0]), pad_c(rwkv_a1[0]), pad_r(rwkv_a2[0]),
                            batch=batch, seq=seq, tm=tm_prep)
    proj = _matmul_wcast(xs, rwkv_w_in[0], tm=tm_big, tn=tn_big, out_dtype=F32, groups=4)
    yg = _wkv(proj, lw, ic, row(rwkv_k_k[0]), row(rwkv_k_a[0]), row(rwkv_r_k[0]), row(rwkv_ln_w[0]),
              row(rwkv_ln_b[0]), batch=batch, seq=seq)
    x1, h1 = _out_proj(yg, rwkv_w_out[0].astype(BF16), x2, row(norm_w[1]), tm=tm_last, keep_sum=True,
                       norm_dtype=BF16)

    lambda_init = 0.8 - 0.6 * math.exp(-0.3 * 1)
    qkvg = _matmul_wcast(h1, diff_w_in[0], tm=tm_big, tn=tn_big, out_dtype=BF16)
    att = _diff_attn(qkvg, diff_lambda[0], row(diff_subln_w[0]), batch=batch, seq=seq, tq=tq,
                     lambda_init=lambda_init)
    (out,) = _out_proj(att, diff_w_out[0].astype(BF16), x1, row(final_norm_w), tm=tm_last, keep_sum=False,
                       norm_dtype=F32)
    return out.reshape(batch, seq, d)
```

```python
import functools
import math

import jax
import jax.numpy as jnp
from jax import lax
from jax.experimental import pallas as pl
from jax.experimental.pallas import tpu as pltpu

F32 = jnp.float32
BF16 = jnp.bfloat16

NORM_EPS = 1e-6
GN_EPS = 64e-5
SUBLN_EPS = 1e-5
HEAD = 64
LANES = 128
CHUNK = 64
LORA_PAD = 128
VMEM_LIMIT = 48 * 1024 * 1024
NEG_BIG = -1e30
LOG2E = 1.0 / math.log(2.0)
SLAB_PAD = 16


def _cparams(sem):
    return pltpu.CompilerParams(dimension_semantics=sem, vmem_limit_bytes=VMEM_LIMIT)


def _dot(a, b):
    return jnp.dot(a, b, preferred_element_type=F32)


def _dot_nt(a, b):
    return lax.dot_general(a, b, (((1,), (1,)), ((), ())), preferred_element_type=F32)


def _dot_tn(a, b):
    return lax.dot_general(a, b, (((0,), (0,)), ((), ())), preferred_element_type=F32)


def _split2(x):
    hi = x.astype(BF16)
    lo = (x - hi.astype(F32)).astype(BF16)
    return hi, lo


def _mm(a, b, passes, nt=False):
    d = _dot_nt if nt else _dot
    if passes == 1:
        return d(a.astype(BF16), b.astype(BF16))
    ah, al = _split2(a)
    bh, bl = _split2(b)
    return d(ah, bh) + (d(al, bh) + d(ah, bl))


def _mm_exact_rhs(a, e_bf16, parts):
    acc = None
    rem = a
    for _ in range(parts):
        piece = rem.astype(BF16)
        term = _dot(piece, e_bf16)
        acc = term if acc is None else acc + term
        rem = rem - piece.astype(F32)
    return acc


def _mm_exact_lhs(e_bf16, a, parts):
    acc = None
    rem = a
    for _ in range(parts):
        piece = rem.astype(BF16)
        term = _dot(e_bf16, piece)
        acc = term if acc is None else acc + term
        rem = rem - piece.astype(F32)
    return acc


def _rwkv_prep_kernel(x_ref, nw_ref, mu_ref, w0_ref, w1_ref, w2_ref, a0_ref, a1_ref, a2_ref,
                      xs_ref, lw_ref, ic_ref, carry_ref):
    t = pl.program_id(1)
    x = x_ref[...]
    tm = x.shape[0]
    h = x * lax.rsqrt(jnp.mean(x * x, axis=-1, keepdims=True) + NORM_EPS) * nw_ref[...]

    @pl.when(t == 0)
    def _():
        carry_ref[...] = jnp.zeros_like(carry_ref)

    prev_last = carry_ref[0:1, :]
    rolled = pltpu.roll(h, 1, axis=0)
    row = lax.broadcasted_iota(jnp.int32, h.shape, 0)
    h_prev = jnp.where(row == 0, prev_last, rolled)
    carry_ref[0:1, :] = h[tm - 1:tm, :]
    d = h_prev - h
    mu = mu_ref[...]
    xs_ref[0] = (h + d * mu[0:1]).astype(BF16)
    xs_ref[1] = (h + d * mu[2:3]).astype(BF16)
    xs_ref[2] = (h + d * mu[3:4]).astype(BF16)
    xs_ref[3] = (h + d * mu[5:6]).astype(BF16)
    xw = (h + d * mu[1:2]).astype(BF16)
    xa = (h + d * mu[4:5]).astype(BF16)
    zw = w0_ref[...] + _dot(jnp.tanh(_dot(xw, w1_ref[...])).astype(BF16), w2_ref[...])
    lw_ref[...] = (-math.exp(-0.5)) / (1.0 + jnp.exp(-zw))
    za = a0_ref[...] + _dot(_dot(xa, a1_ref[...]).astype(BF16), a2_ref[...])
    ic_ref[...] = 1.0 / (1.0 + jnp.exp(-za))


def _rwkv_prep(x2, nw, mu, w0, w1p, w2p, a0, a1p, a2p, *, batch, seq, tm):
    m, d = x2.shape
    nt = seq // tm
    row_spec = pl.BlockSpec((tm, d), lambda b, t: (b * nt + t, 0))
    vec_spec = pl.BlockSpec((1, d), lambda b, t: (0, 0))

    def full(a):
        return pl.BlockSpec(a.shape, lambda b, t: (0,) * a.ndim)

    return pl.pallas_call(
        _rwkv_prep_kernel,
        grid=(batch, nt),
        in_specs=[row_spec, vec_spec, full(mu), vec_spec, full(w1p), full(w2p), vec_spec, full(a1p), full(a2p)],
        out_specs=[pl.BlockSpec((4, tm, d), lambda b, t: (0, b * nt + t, 0)), row_spec, row_spec],
        out_shape=[jax.ShapeDtypeStruct((4, m, d), BF16),
                   jax.ShapeDtypeStruct((m, d), F32),
                   jax.ShapeDtypeStruct((m, d), F32)],
        scratch_shapes=[pltpu.VMEM((8, d), F32)],
        compiler_params=_cparams(("parallel", "arbitrary")),
        name="rwkv_prep",
    )(x2, nw, mu, w0, w1p, w2p, a0, a1p, a2p)


def _mm_wcast_kernel(a_ref, b_ref, o_ref, w_sc):
    @pl.when(pl.program_id(1) == 0)
    def _():
        w_sc[...] = b_ref[...].astype(BF16)

    o_ref[...] = _dot(a_ref[...], w_sc[...]).astype(o_ref.dtype)


def _matmul_wcast(a, w, *, tm, tn, out_dtype, groups=1):
    m, k = a.shape[-2:]
    n = w.shape[1]
    per_group = (n // groups) // tn
    if groups > 1:
        a_spec = pl.BlockSpec((None, tm, k), lambda j, i: (j // per_group, i, 0))
    else:
        a_spec = pl.BlockSpec((tm, k), lambda j, i: (i, 0))
    return pl.pallas_call(
        _mm_wcast_kernel,
        grid=(n // tn, m // tm),
        in_specs=[a_spec, pl.BlockSpec((k, tn), lambda j, i: (0, j))],
        out_specs=pl.BlockSpec((tm, tn), lambda j, i: (i, j)),
        out_shape=jax.ShapeDtypeStruct((m, n), out_dtype),
        scratch_shapes=[pltpu.VMEM((k, tn), BF16)],
        compiler_params=_cparams(("parallel", "arbitrary")),
        name="proj" if groups > 1 else "matmul",
    )(a, w)


def _out_proj_kernel(a_ref, b_ref, r_ref, nw_ref, *o_refs):
    y = r_ref[...] + _dot(a_ref[...], b_ref[...])
    yn = y * lax.rsqrt(jnp.mean(y * y, axis=-1, keepdims=True) + NORM_EPS) * nw_ref[...]
    if len(o_refs) == 2:
        o_refs[0][...] = y
    o_refs[-1][...] = yn.astype(o_refs[-1].dtype)


def _out_proj(a, w, res, nw, *, tm, keep_sum, norm_dtype):
    m, k = a.shape
    n = w.shape[1]
    row_spec = pl.BlockSpec((tm, n), lambda i: (i, 0))
    out_specs = [row_spec, row_spec] if keep_sum else [row_spec]
    out_shape = [jax.ShapeDtypeStruct((m, n), norm_dtype)]
    if keep_sum:
        out_shape = [jax.ShapeDtypeStruct((m, n), F32)] + out_shape
    return pl.pallas_call(
        _out_proj_kernel,
        grid=(m // tm,),
        in_specs=[pl.BlockSpec((tm, k), lambda i: (i, 0)),
                  pl.BlockSpec((k, n), lambda i: (0, 0)),
                  row_spec,
                  pl.BlockSpec((1, n), lambda i: (0, 0))],
        out_specs=out_specs,
        out_shape=out_shape,
        compiler_params=_cparams(("parallel",)),
        name="out_proj_sum_norm" if keep_sum else "out_proj_norm",
    )(a, w, res, nw)


WKV_GROUP = 8
WKV_STREAMS = 2


def _wkv_kernel(r_ref, k_ref, v_ref, g_ref, lw_ref, ic_ref, kk_ref, ka_ref, rk_ref, lnw_ref, lnb_ref,
                o_ref, rm_sc, y0_sc, g_sc, wc_sc, y_sc, *, group):
    seq = r_ref.shape[0]
    streams = r_ref.shape[1] // LANES
    ngroups = seq // (CHUNK * group)
    two = 2 * CHUNK
    inst = [(u, st) for u in range(group) for st in range(streams)]
    ninst = len(inst)

    def lanes(st):
        return slice(st * LANES, (st + 1) * LANES)

    lane1 = lax.broadcasted_iota(jnp.int32, (1, LANES), 1)
    m0 = (lane1 < HEAD).astype(F32)
    m1 = 1.0 - m0
    row2 = lax.broadcasted_iota(jnp.int32, (two, two), 0)
    col2 = lax.broadcasted_iota(jnp.int32, (two, two), 1)
    eye2 = (row2 == col2).astype(F32)
    same_head = (row2 < HEAD) == (col2 < HEAD)
    ones_bd = same_head.astype(BF16)
    row1 = lax.broadcasted_iota(jnp.int32, (CHUNK, LANES), 0)
    col1 = lax.broadcasted_iota(jnp.int32, (CHUNK, LANES), 1) % CHUNK
    strict = row1 > col1
    incl = row1 >= col1
    eye1 = (row1 == col1).astype(F32)
    tril_c = incl[:, :CHUNK].astype(BF16)
    zeros_b = jnp.zeros((CHUNK, LANES), BF16)

    kk_p = [kk_ref[:, lanes(st)] for st in range(streams)]
    ka_p = [ka_ref[:, lanes(st)] for st in range(streams)]

    def segsum(z, parts):
        return _mm_exact_rhs(z, ones_bd, parts)

    m0b = m0.astype(BF16)
    m1b = m1.astype(BF16)

    def bdiag(z):
        zb = z.astype(BF16)
        return jnp.concatenate([zb * m0b, zb * m1b], axis=0)

    def chunk_rows(c):
        return pl.ds(pl.multiple_of(c * CHUNK, CHUNK), CHUNK)

    def phase_a(gi):
        us = range(ninst)
        rows = [chunk_rows(gi * group + u) for u, _ in inst]
        cols = [lanes(st) for _, st in inst]
        k = [k_ref[rows[i], cols[i]] for i in us]
        lw = [lw_ref[rows[i], cols[i]] for i in us]
        kkv = [k[i] * kk_p[inst[i][1]] for i in us]
        n2 = [segsum(kkv[u] * kkv[u], 2) for u in us]
        lin = [_mm_exact_lhs(tril_c, lw[u], 2) for u in us]
        yield
        a_n, r_n, v_b, bk_n, wcb, a_all = [], [], [], [], [], []
        for u in us:
            ic = ic_ref[rows[u], cols[u]]
            kkn = kkv[u] / jnp.maximum(jnp.sqrt(n2[u]), 1e-12)
            kh = k[u] * (1.0 + (ic - 1.0) * ka_p[inst[u][1]])
            e_in = jnp.exp(lin[u])
            e_ex = jnp.exp(lin[u] - lw[u])
            e_neg = jnp.exp(-lin[u])
            b_f = kkn * ic * e_neg
            k_f = kh * e_neg
            a_f = -kkn * e_ex
            a_n.append(a_f)
            r_n.append(r_ref[rows[u], cols[u]] * e_in)
            v_b.append(bdiag(v_ref[rows[u], cols[u]]))
            a_all.append(_dot_nt(jnp.concatenate([a_f, r_n[u]], axis=0).astype(BF16),
                                 jnp.concatenate([bdiag(b_f), bdiag(k_f)], axis=0)))
            bk_n.append(jnp.concatenate([b_f, k_f], axis=0).astype(BF16))
            wcol = jnp.sum(eye2 * e_in[CHUNK - 1:CHUNK, :], axis=1, keepdims=True)
            wcb.append(jnp.broadcast_to(wcol, (two, LANES)))
        yield
        t_inv, pw, a_rb, av = [], [], [], []
        for u in us:
            a_ab = jnp.where(strict, a_all[u][:CHUNK, :LANES], 0.0)
            a_ak = jnp.where(strict, a_all[u][:CHUNK, LANES:], 0.0)
            a_rk = jnp.where(incl, a_all[u][CHUNK:, LANES:], 0.0)
            a_rb.append(jnp.where(incl, a_all[u][CHUNK:, :LANES], 0.0).astype(BF16))
            av.append(_dot(jnp.concatenate([a_ak, a_rk], axis=0).astype(BF16), v_b[u]))
            t_inv.append(eye1 + a_ab)
            pw.append(a_ab)
        yield
        pw = [_dot(pw[u].astype(BF16), bdiag(pw[u])) for u in us]
        yield
        for _ in range(4):
            both = [_dot(jnp.concatenate([t_inv[u], pw[u]], axis=0).astype(BF16), bdiag(pw[u])) for u in us]
            t_inv = [t_inv[u] + both[u][:CHUNK] for u in us]
            pw = [both[u][CHUNK:] for u in us]
            yield
        t_inv = [t_inv[u] + _dot(t_inv[u].astype(BF16), bdiag(pw[u])) for u in us]
        yield
        pq = [_dot(t_inv[u].astype(BF16),
                   jnp.concatenate([bdiag(a_n[u]), bdiag(av[u][:CHUNK])], axis=1)) for u in us]
        yield
        rq = [_dot(a_rb[u], jnp.concatenate([bdiag(pq[u][:, :LANES]), bdiag(pq[u][:, LANES:])], axis=1))
              for u in us]
        mg = [_dot_tn(bk_n[u], jnp.concatenate([pq[u].astype(BF16),
                                                jnp.concatenate([zeros_b, v_b[u][:CHUNK] + v_b[u][CHUNK:]], axis=1)],
                                               axis=0)) for u in us]
        yield
        for u in us:
            m_bd = jnp.where(same_head, mg[u][:, :LANES], 0.0) * wcb[u]
            rm_sc[u] = jnp.concatenate([r_n[u] + rq[u][:, :LANES], m_bd], axis=0).astype(BF16)
            g_sc[u] = jnp.where(same_head, mg[u][:, LANES:], 0.0) * wcb[u]
            y0_sc[u] = rq[u][:, LANES:] + av[u][CHUNK:]
            wc_sc[u] = wcb[u]

    def state_step(gi, i, hs):
        u, st = inst[i]
        res = _dot(rm_sc[i], hs[st].astype(BF16))
        y_sc[chunk_rows(gi * group + u), lanes(st)] = res[:CHUNK] + y0_sc[i]
        return tuple(wc_sc[i] * hs[st] + res[CHUNK:] + g_sc[i] if t == st else hs[t] for t in range(streams))

    inv_n = 1.0 / HEAD
    grows = group * CHUNK

    def finish(gi):
        rows = pl.ds(pl.multiple_of(gi * grows, grows), grows)
        for st in range(streams):
            sl = lanes(st)
            y = y_sc[rows, sl]
            mean = segsum(y, 1) * inv_n
            yc = y - mean
            var = segsum(yc * yc, 1) * inv_n
            gn = yc * lax.rsqrt(var + GN_EPS) * lnw_ref[:, sl] + lnb_ref[:, sl]
            kh = k_ref[rows, sl] * (1.0 + (ic_ref[rows, sl] - 1.0) * ka_p[st])
            bonus = segsum(r_ref[rows, sl] * kh * rk_ref[:, sl], 1) * v_ref[rows, sl]
            g = g_ref[rows, sl]
            o_ref[rows, sl] = ((gn + bonus) * (g / (1.0 + jnp.exp(-g)))).astype(o_ref.dtype)

    def run(a_gi, b_gi, c_gi, hs):
        done = 0
        if a_gi is None:
            for i in range(ninst):
                hs = state_step(b_gi, i, hs)
        else:
            per_slot = -(-ninst // 9)
            for i, _ in enumerate(phase_a(a_gi)):
                for _ in range(per_slot if (b_gi is not None and i >= 1) else 0):
                    if done < ninst:
                        hs = state_step(b_gi, done, hs)
                        done += 1
            assert b_gi is None or done == ninst
        if c_gi is not None:
            finish(c_gi)
        return hs

    run(0, None, None, None)
    hs = run(1, 0, None, tuple(jnp.zeros((two, LANES), F32) for _ in range(streams)))
    if ngroups > 2:
        hs = lax.fori_loop(1, ngroups - 1, lambda gi, hs: run(gi + 1, gi, gi - 1, hs), hs)
    run(None, ngroups - 1, ngroups - 2, hs)
    finish(ngroups - 1)


def _wkv(proj, lw, ic, kk, ka, rk, lnw, lnb, *, batch, seq):
    m, n4 = proj.shape
    w = n4 // 4
    group = WKV_GROUP
    streams = WKV_STREAMS if (w // LANES) % WKV_STREAMS == 0 else 1
    wide = streams * LANES
    nblk = w // wide
    two = 2 * CHUNK
    ninst = group * streams
    assert seq % (CHUNK * group) == 0 and seq // (CHUNK * group) >= 2

    def col_spec(p):
        return pl.BlockSpec((seq, wide), lambda b, hp: (b, p * nblk + hp))

    par_spec = pl.BlockSpec((1, wide), lambda b, hp: (0, hp))
    act_spec = pl.BlockSpec((seq, wide), lambda b, hp: (b, hp))
    return pl.pallas_call(
        functools.partial(_wkv_kernel, group=group),
        grid=(batch, nblk),
        in_specs=[col_spec(0), col_spec(1), col_spec(2), col_spec(3), act_spec, act_spec,
                  par_spec, par_spec, par_spec, par_spec, par_spec],
        out_specs=act_spec,
        out_shape=jax.ShapeDtypeStruct((m, w), BF16),
        scratch_shapes=[pltpu.VMEM((ninst, CHUNK + two, LANES), BF16),
                        pltpu.VMEM((ninst, CHUNK, LANES), F32),
                        pltpu.VMEM((ninst, two, LANES), F32),
                        pltpu.VMEM((ninst, two, LANES), F32),
                        pltpu.VMEM((seq, wide), F32)],
        compiler_params=_cparams(("parallel", "parallel")),
        name="wkv7",
    )(proj, proj, proj, proj, lw, ic, kk, ka, rk, lnw, lnb)


def _attn_kernel(q_ref, k_ref, v_ref, g_ref, lam_ref, sw_ref, o_ref,
                 z_a, z_b, p_a, p_b, acc_sc, ka1_sc, ka2_sc, q1_sc, q2_sc, *, tq, n_heads, lambda_init):
    seq = q_ref.shape[0]
    nq = seq // tq
    tk = tq
    half = 256
    ng = 2 * tq // LANES
    h = pl.program_id(1)
    lane = lax.broadcasted_iota(jnp.int32, (1, LANES), 1)
    slope = jnp.exp2(jnp.full((1, LANES), -8.0 / n_heads, F32) * (h + 1).astype(F32)) * LOG2E
    s_hi = slope.astype(BF16).astype(F32)
    s_lo = slope - s_hi
    s_pat = jnp.where(lane % 2 == 0, s_hi, s_lo)

    row_b = lax.broadcasted_iota(jnp.int32, (tk, LANES), 0)
    lane_b = lax.broadcasted_iota(jnp.int32, (tk, LANES), 1)
    kidx = jnp.where(lane_b % 4 < 2, row_b % half, (row_b // half) * half).astype(F32)
    aug1 = jnp.where((lane_b >= HEAD) & (lane_b < HEAD + 4), kidx, 0.0).astype(BF16)
    aug2 = jnp.where(lane_b < 4, kidx, 0.0).astype(BF16)
    lo_b = (lane_b < HEAD).astype(F32).astype(BF16)
    hi_b = (lane_b >= HEAD).astype(F32).astype(BF16)
    for j in range(seq // tk):
        ks = k_ref[j * tk:(j + 1) * tk, :]
        ka1_sc[j * tk:(j + 1) * tk, :] = ks * lo_b + aug1
        ka2_sc[j * tk:(j + 1) * tk, :] = ks * hi_b + aug2

    tri = (lax.broadcasted_iota(jnp.int32, (LANES, LANES), 0) <= lax.broadcasted_iota(jnp.int32, (LANES, LANES), 1))
    lam = lam_ref[...]
    lam_full = (jnp.exp(jnp.sum(lam[0:1] * lam[1:2], axis=1, keepdims=True))
                - jnp.exp(jnp.sum(lam[2:3] * lam[3:4], axis=1, keepdims=True)) + lambda_init)
    sw_scaled = sw_ref[...] * (1.0 - lambda_init)

    def load_q(qi):
        qf = q_ref[qi * tq:(qi + 1) * tq, :].astype(F32) * (LOG2E / math.sqrt(HEAD))
        q1_sc[...] = jnp.where(lane < HEAD, qf, jnp.where(lane < HEAD + 4, s_pat, 0.0)).astype(BF16)
        q2_sc[...] = jnp.where(lane >= HEAD, qf, jnp.where(lane < 4, s_pat, 0.0)).astype(BF16)

    def scores_to(j, z_ref, diag):
        hq = ng // 4
        for ka_sc, q_sc, g0 in ((ka1_sc, q1_sc, 0), (ka2_sc, q2_sc, ng // 2)):
            if not diag:
                z = _dot_nt(ka_sc[j * tk:(j + 1) * tk, :], q_sc[...])
                for g in range(ng // 2):
                    z_ref[g0 + g, 0:tk, :] = z[:, g * LANES:(g + 1) * LANES]
            else:
                top = _dot_nt(ka_sc[j * tk:j * tk + tk // 2, :], q_sc[...])
                bot = _dot_nt(ka_sc[j * tk + tk // 2:(j + 1) * tk, :], q_sc[tq // 2:, :])
                for g in range(ng // 2):
                    z_ref[g0 + g, 0:tk // 2, :] = top[:, g * LANES:(g + 1) * LANES]
                for g in range(hq):
                    z_ref[g0 + hq + g, tk // 2:tk, :] = bot[:, g * LANES:(g + 1) * LANES]

    def accumulate(j, p_ref, alpha, first, diag):
        hq = ng // 4
        if not diag:
            p_all = jnp.concatenate([p_ref[g, 0:tk, :] for g in range(ng)], axis=1)
            pv = _dot_tn(v_ref[j * tk:(j + 1) * tk, :], p_all)
            parts = [pv[:, g * LANES:(g + 1) * LANES] for g in range(ng)]
        else:
            late = [g for g in range(ng) if g % (ng // 2) >= hq]
            p_top = jnp.concatenate([p_ref[g, 0:tk // 2, :] for g in range(ng)], axis=1)
            p_bot = jnp.concatenate([p_ref[g, tk // 2:tk, :] for g in late], axis=1)
            pv_top = _dot_tn(v_ref[j * tk:j * tk + tk // 2, :], p_top)
            pv_bot = _dot_tn(v_ref[j * tk + tk // 2:(j + 1) * tk, :], p_bot)
            parts = [pv_top[:, g * LANES:(g + 1) * LANES] for g in range(ng)]
            for n, g in enumerate(late):
                parts[g] = parts[g] + pv_bot[:, n * LANES:(n + 1) * LANES]
        for g in range(ng):
            sl = slice(g * LANES, (g + 1) * LANES)
            acc_sc[g, 0:LANES, :] = parts[g] if first else alpha[:, sl] * acc_sc[g, 0:LANES, :] + parts[g]

    def softmax(qi, j, z_ref, p_ref, m, l):
        c_blk = slope * float((j - qi) * tq)
        ms, ls, alphas = [], [], []
        for g in range(ng):
            sl = slice(g * LANES, (g + 1) * LANES)
            nrows = tk
            if j == qi:
                nrows = (g % (ng // 2)) * LANES + LANES
                fill_to = tk // 2 if nrows <= tk // 2 else tk
                if nrows < fill_to:
                    p_ref[g, nrows:fill_to, :] = jnp.zeros((fill_to - nrows, LANES), BF16)
            diag_tile = jnp.where(tri, z_ref[g, nrows - LANES:nrows, :], NEG_BIG) if j == qi else None
            nplain = nrows - LANES if j == qi else nrows
            mx = diag_tile.max(axis=0, keepdims=True) if diag_tile is not None else None
            if nplain:
                mp = jnp.max(z_ref[g, 0:nplain, :], axis=0, keepdims=True)
                mx = mp if mx is None else jnp.maximum(mx, mp)
            mg = m[:, sl]
            mn = jnp.maximum(mg, mx + c_blk)
            alpha = jnp.exp2(mg - mn)
            shift = mn - c_blk
            lsum = None
            for r0 in range(0, nrows, LANES):
                zt = diag_tile if (j == qi and r0 == nplain) else z_ref[g, r0:r0 + LANES, :]
                pt = jnp.exp2(zt - shift)
                part = jnp.sum(pt, axis=0, keepdims=True)
                lsum = part if lsum is None else lsum + part
                p_ref[g, r0:r0 + LANES, :] = pt.astype(BF16)
            ls.append(alpha * l[:, sl] + lsum)
            ms.append(mn)
            alphas.append(alpha)
        return jnp.concatenate(ms, axis=1), jnp.concatenate(ls, axis=1), jnp.concatenate(alphas, axis=1)

    def finalize(qi, l):
        on = jnp.concatenate([acc_sc[g, 0:LANES, :] for g in range(ng)], axis=1) / l
        ot = on[:, :tq] - lam_full * on[:, tq:]
        ot = ot * lax.rsqrt(jnp.mean(ot * ot, axis=0, keepdims=True) + SUBLN_EPS)
        g = g_ref[qi * tq:(qi + 1) * tq, :].astype(F32)
        o_ref[qi * tq:(qi + 1) * tq, :] = (ot.T * sw_scaled * (g / (1.0 + jnp.exp(-g)))).astype(o_ref.dtype)

    items = [(qi, j) for qi in range(nq) for j in range(qi + 1)]
    zs, ps = (z_a, z_b), (p_a, p_b)
    load_q(0)
    scores_to(0, z_a, True)
    m = l = alpha = None
    for n, (qi, j) in enumerate(items):
        if n > 0:
            pqi, pj = items[n - 1]
            accumulate(pj, ps[(n - 1) % 2], alpha, pj == 0, pj == pqi)
            if pqi != qi:
                finalize(pqi, l)
        if j == 0:
            m = jnp.full((1, 2 * tq), NEG_BIG, F32)
            l = jnp.zeros((1, 2 * tq), F32)
        if n + 1 < len(items):
            nqi, nj = items[n + 1]
            if nqi != qi:
                load_q(nqi)
            scores_to(nj, zs[(n + 1) % 2], nj == nqi)
        m, l, alpha = softmax(qi, j, zs[n % 2], ps[n % 2], m, l)
    pqi, pj = items[-1]
    accumulate(pj, ps[(len(items) - 1) % 2], alpha, pj == 0, pj == pqi)
    finalize(pqi, l)


def _diff_attn(qkvg, lam, sw, *, batch, seq, tq, lambda_init):
    m, n4 = qkvg.shape
    w = n4 // 4
    nh = w // LANES
    ngl = 2 * tq // LANES

    def col_spec(p):
        return pl.BlockSpec((seq, LANES), lambda b, h: (b, p * nh + h))

    return pl.pallas_call(
        functools.partial(_attn_kernel, tq=tq, n_heads=nh, lambda_init=lambda_init),
        grid=(batch, nh),
        in_specs=[col_spec(0), col_spec(1), col_spec(2), col_spec(3),
                  pl.BlockSpec(lam.shape, lambda b, h: (0, 0)),
                  pl.BlockSpec((1, LANES), lambda b, h: (0, 0))],
        out_specs=pl.BlockSpec((seq, LANES), lambda b, h: (b, h)),
        out_shape=jax.ShapeDtypeStruct((m, w), BF16),
        scratch_shapes=[pltpu.VMEM((ngl, tq + SLAB_PAD, LANES), F32), pltpu.VMEM((ngl, tq + SLAB_PAD, LANES), F32),
                        pltpu.VMEM((ngl, tq + SLAB_PAD, LANES), BF16), pltpu.VMEM((ngl, tq + SLAB_PAD, LANES), BF16),
                        pltpu.VMEM((ngl, LANES + SLAB_PAD, LANES), F32),
                        pltpu.VMEM((seq, LANES), BF16), pltpu.VMEM((seq, LANES), BF16),
                        pltpu.VMEM((tq, LANES), BF16), pltpu.VMEM((tq, LANES), BF16)],
        compiler_params=_cparams(("parallel", "parallel")),
        name="diff_attn",
    )(qkvg, qkvg, qkvg, qkvg, lam, sw)


def kernel(x, norm_w, rwkv_mu, rwkv_w_in, rwkv_w0, rwkv_w1, rwkv_w2, rwkv_a0, rwkv_a1, rwkv_a2, rwkv_k_k, rwkv_k_a, rwkv_r_k, rwkv_ln_w, rwkv_ln_b, rwkv_w_out, diff_w_in, diff_lambda, diff_subln_w, diff_w_out, final_norm_w):
    batch, seq, d = x.shape
    m = batch * seq
    assert d % LANES == 0 and seq % CHUNK == 0
    assert norm_w.shape[0] == 2 and rwkv_mu.shape[0] == 1 and diff_w_in.shape[0] == 1
    tm_big = min(1024, m)
    tn_big = min(1024, d)
    tm_prep = min(256, seq)
    tm_last = min(512, m)
    tq = min(512, seq)

    x2 = x.reshape(m, d)
    row = lambda a: a.reshape(1, -1)
    lora = rwkv_w1.shape[-1]
    pad_c = lambda a: jnp.pad(a, ((0, 0), (0, LORA_PAD - lora))).astype(BF16)
    pad_r = lambda a: jnp.pad(a, ((0, LORA_PAD - lora), (0, 0))).astype(BF16)

    xs, lw, ic = _rwkv_prep(x2, row(norm_w[0]), rwkv_mu[0], row(rwkv_w0[0]), pad_c(rwkv_w1[0]), pad_r(rwkv_w2[0]),
                            row(rwkv_a0[0]), pad_c(rwkv_a1[0]), pad_r(rwkv_a2[0]),
                            batch=batch, seq=seq, tm=tm_prep)
    proj = _matmul_wcast(xs, rwkv_w_in[0], tm=tm_big, tn=tn_big, out_dtype=F32, groups=4)
    yg = _wkv(proj, lw, ic, row(rwkv_k_k[0]), row(rwkv_k_a[0]), row(rwkv_r_k[0]), row(rwkv_ln_w[0]),
              row(rwkv_ln_b[0]), batch=batch, seq=seq)
    x1, h1 = _out_proj(yg, rwkv_w_out[0].astype(BF16), x2, row(norm_w[1]), tm=tm_last, keep_sum=True,
                       norm_dtype=BF16)

    lambda_init = 0.8 - 0.6 * math.exp(-0.3 * 1)
    qkvg = _matmul_wcast(h1, diff_w_in[0], tm=tm_big, tn=tn_big, out_dtype=BF16)
    att = _diff_attn(qkvg, diff_lambda[0], row(diff_subln_w[0]), batch=batch, seq=seq, tq=tq,
                     lambda_init=lambda_init)
    (out,) = _out_proj(att, diff_w_out[0].astype(BF16), x1, row(final_norm_w), tm=tm_last, keep_sum=False,
                       norm_dtype=F32)
    return out.reshape(batch, seq, d)
```

```python
import functools
import math

import jax
import jax.numpy as jnp
from jax import lax
from jax.experimental import pallas as pl
from jax.experimental.pallas import tpu as pltpu

F32 = jnp.float32
BF16 = jnp.bfloat16

NORM_EPS = 1e-6
GN_EPS = 64e-5
SUBLN_EPS = 1e-5
HEAD = 64
LANES = 128
CHUNK = 64
LORA_PAD = 128
VMEM_LIMIT = 48 * 1024 * 1024
NEG_BIG = -1e30
LOG2E = 1.0 / math.log(2.0)
SLAB_PAD = 16


def _cparams(sem):
    return pltpu.CompilerParams(dimension_semantics=sem, vmem_limit_bytes=VMEM_LIMIT)


def _dot(a, b):
    return jnp.dot(a, b, preferred_element_type=F32)


def _dot_nt(a, b):
    return lax.dot_general(a, b, (((1,), (1,)), ((), ())), preferred_element_type=F32)


def _dot_tn(a, b):
    return lax.dot_general(a, b, (((0,), (0,)), ((), ())), preferred_element_type=F32)


def _split2(x):
    hi = x.astype(BF16)
    lo = (x - hi.astype(F32)).astype(BF16)
    return hi, lo


def _mm(a, b, passes, nt=False):
    d = _dot_nt if nt else _dot
    if passes == 1:
        return d(a.astype(BF16), b.astype(BF16))
    ah, al = _split2(a)
    bh, bl = _split2(b)
    return d(ah, bh) + (d(al, bh) + d(ah, bl))


def _mm_exact_rhs(a, e_bf16, parts):
    acc = None
    rem = a
    for _ in range(parts):
        piece = rem.astype(BF16)
        term = _dot(piece, e_bf16)
        acc = term if acc is None else acc + term
        rem = rem - piece.astype(F32)
    return acc


def _mm_exact_lhs(e_bf16, a, parts):
    acc = None
    rem = a
    for _ in range(parts):
        piece = rem.astype(BF16)
        term = _dot(e_bf16, piece)
        acc = term if acc is None else acc + term
        rem = rem - piece.astype(F32)
    return acc


def _rwkv_prep_kernel(x_ref, nw_ref, mu_ref, w0_ref, wl1_ref, w2_ref, a0_ref, a2_ref,
                      xs_ref, lw_ref, ic_ref, carry_ref):
    t = pl.program_id(1)
    x = x_ref[...]
    tm = x.shape[0]
    h = x * lax.rsqrt(jnp.mean(x * x, axis=-1, keepdims=True) + NORM_EPS) * nw_ref[...]

    @pl.when(t == 0)
    def _():
        carry_ref[...] = jnp.zeros_like(carry_ref)

    prev_last = carry_ref[0:1, :]
    rolled = pltpu.roll(h, 1, axis=0)
    row = lax.broadcasted_iota(jnp.int32, h.shape, 0)
    h_prev = jnp.where(row == 0, prev_last, rolled)
    carry_ref[0:1, :] = h[tm - 1:tm, :]
    d = h_prev - h
    mu = mu_ref[...]
    for p in range(4):
        xs_ref[p] = (h + d * mu[p:p + 1]).astype(BF16)
    t1 = _dot(jnp.concatenate([h, d], axis=1).astype(BF16), wl1_ref[...])
    zw = w0_ref[...] + _dot(jnp.tanh(t1[:, :LORA_PAD]).astype(BF16), w2_ref[...])
    lw_ref[...] = (-math.exp(-0.5)) / (1.0 + jnp.exp2(zw * (-LOG2E)))
    za = a0_ref[...] + _dot(t1[:, LORA_PAD:].astype(BF16), a2_ref[...])
    ic_ref[...] = 1.0 / (1.0 + jnp.exp2(za * (-LOG2E)))


def _rwkv_prep(x2, nw, mu, w0, wl1, w2p, a0, a2p, *, batch, seq, tm):
    m, d = x2.shape
    nt = seq // tm
    row_spec = pl.BlockSpec((tm, d), lambda b, t: (b * nt + t, 0))
    vec_spec = pl.BlockSpec((1, d), lambda b, t: (0, 0))

    def full(a):
        return pl.BlockSpec(a.shape, lambda b, t: (0,) * a.ndim)

    return pl.pallas_call(
        _rwkv_prep_kernel,
        grid=(batch, nt),
        in_specs=[row_spec, vec_spec, full(mu), vec_spec, full(wl1), full(w2p), vec_spec, full(a2p)],
        out_specs=[pl.BlockSpec((4, tm, d), lambda b, t: (0, b * nt + t, 0)), row_spec, row_spec],
        out_shape=[jax.ShapeDtypeStruct((4, m, d), BF16),
                   jax.ShapeDtypeStruct((m, d), F32),
                   jax.ShapeDtypeStruct((m, d), F32)],
        scratch_shapes=[pltpu.VMEM((8, d), F32)],
        compiler_params=_cparams(("parallel", "arbitrary")),
        name="rwkv_prep",
    )(x2, nw, mu, w0, wl1, w2p, a0, a2p)


def _mm_wcast_kernel(a_ref, b_ref, o_ref, w_sc):
    @pl.when(pl.program_id(1) == 0)
    def _():
        w_sc[...] = b_ref[...].astype(BF16)

    o_ref[...] = _dot(a_ref[...], w_sc[...]).astype(o_ref.dtype)


def _matmul_wcast(a, w, *, tm, tn, out_dtype, groups=1):
    m, k = a.shape[-2:]
    n = w.shape[1]
    per_group = (n // groups) // tn
    if groups > 1:
        a_spec = pl.BlockSpec((None, tm, k), lambda j, i: (j // per_group, i, 0))
    else:
        a_spec = pl.BlockSpec((tm, k), lambda j, i: (i, 0))
    return pl.pallas_call(
        _mm_wcast_kernel,
        grid=(n // tn, m // tm),
        in_specs=[a_spec, pl.BlockSpec((k, tn), lambda j, i: (0, j))],
        out_specs=pl.BlockSpec((tm, tn), lambda j, i: (i, j)),
        out_shape=jax.ShapeDtypeStruct((m, n), out_dtype),
        scratch_shapes=[pltpu.VMEM((k, tn), BF16)],
        compiler_params=_cparams(("parallel", "arbitrary")),
        name="proj" if groups > 1 else "matmul",
    )(a, w)


def _out_proj_kernel(a_ref, b_ref, r_ref, nw_ref, *o_refs):
    y = r_ref[...] + _dot(a_ref[...], b_ref[...])
    yn = y * lax.rsqrt(jnp.mean(y * y, axis=-1, keepdims=True) + NORM_EPS) * nw_ref[...]
    if len(o_refs) == 2:
        o_refs[0][...] = y
    o_refs[-1][...] = yn.astype(o_refs[-1].dtype)


def _out_proj(a, w, res, nw, *, tm, keep_sum, norm_dtype):
    m, k = a.shape
    n = w.shape[1]
    row_spec = pl.BlockSpec((tm, n), lambda i: (i, 0))
    out_specs = [row_spec, row_spec] if keep_sum else [row_spec]
    out_shape = [jax.ShapeDtypeStruct((m, n), norm_dtype)]
    if keep_sum:
        out_shape = [jax.ShapeDtypeStruct((m, n), F32)] + out_shape
    return pl.pallas_call(
        _out_proj_kernel,
        grid=(m // tm,),
        in_specs=[pl.BlockSpec((tm, k), lambda i: (i, 0)),
                  pl.BlockSpec((k, n), lambda i: (0, 0)),
                  row_spec,
                  pl.BlockSpec((1, n), lambda i: (0, 0))],
        out_specs=out_specs,
        out_shape=out_shape,
        compiler_params=_cparams(("parallel",)),
        name="out_proj_sum_norm" if keep_sum else "out_proj_norm",
    )(a, w, res, nw)


WKV_GROUP = 8
WKV_STREAMS = 2


def _wkv_kernel(r_ref, k_ref, v_ref, g_ref, lw_ref, ic_ref, kk_ref, ka_ref, rk_ref, lnw_ref, lnb_ref,
                o_ref, rm_sc, y0_sc, g_sc, wc_sc, y_sc, *, group):
    seq = r_ref.shape[0]
    streams = r_ref.shape[1] // LANES
    ngroups = seq // (CHUNK * group)
    two = 2 * CHUNK
    inst = [(u, st) for u in range(group) for st in range(streams)]
    ninst = len(inst)

    def lanes(st):
        return slice(st * LANES, (st + 1) * LANES)

    lane1 = lax.broadcasted_iota(jnp.int32, (1, LANES), 1)
    m0 = (lane1 < HEAD).astype(F32)
    m1 = 1.0 - m0
    row2 = lax.broadcasted_iota(jnp.int32, (two, two), 0)
    col2 = lax.broadcasted_iota(jnp.int32, (two, two), 1)
    eye2 = (row2 == col2).astype(F32)
    same_head = (row2 < HEAD) == (col2 < HEAD)
    ones_bd = same_head.astype(BF16)
    row1 = lax.broadcasted_iota(jnp.int32, (CHUNK, LANES), 0)
    col1 = lax.broadcasted_iota(jnp.int32, (CHUNK, LANES), 1) % CHUNK
    strict = row1 > col1
    incl = row1 >= col1
    eye1 = (row1 == col1).astype(F32)
    tril_c = incl[:, :CHUNK].astype(BF16)
    zeros_b = jnp.zeros((CHUNK, LANES), BF16)

    kk_p = [kk_ref[:, lanes(st)] for st in range(streams)]
    ka_p = [ka_ref[:, lanes(st)] for st in range(streams)]

    def segsum(z, parts):
        return _mm_exact_rhs(z, ones_bd, parts)

    m0b = m0.astype(BF16)
    m1b = m1.astype(BF16)

    def bdiag(z):
        zb = z.astype(BF16)
        return jnp.concatenate([zb * m0b, zb * m1b], axis=0)

    def chunk_rows(c):
        return pl.ds(pl.multiple_of(c * CHUNK, CHUNK), CHUNK)

    def phase_a(gi):
        us = range(ninst)
        rows = [chunk_rows(gi * group + u) for u, _ in inst]
        cols = [lanes(st) for _, st in inst]
        k = [k_ref[rows[i], cols[i]] for i in us]
        lw = [lw_ref[rows[i], cols[i]] for i in us]
        kkv = [k[i] * kk_p[inst[i][1]] for i in us]
        n2 = [segsum(kkv[u] * kkv[u], 2) for u in us]
        lin = [_mm_exact_lhs(tril_c, lw[u], 2) for u in us]
        yield
        a_n, r_n, v_b, bk_n, wcb, a_all = [], [], [], [], [], []
        for u in us:
            ic = ic_ref[rows[u], cols[u]]
            kkn = kkv[u] / jnp.maximum(jnp.sqrt(n2[u]), 1e-12)
            kh = k[u] * (1.0 + (ic - 1.0) * ka_p[inst[u][1]])
            e_in = jnp.exp(lin[u])
            e_ex = jnp.exp(lin[u] - lw[u])
            e_neg = jnp.exp(-lin[u])
            b_f = kkn * ic * e_neg
            k_f = kh * e_neg
            a_f = -kkn * e_ex
            a_n.append(a_f)
            r_n.append(r_ref[rows[u], cols[u]] * e_in)
            v_b.append(bdiag(v_ref[rows[u], cols[u]]))
            a_all.append(_dot_nt(jnp.concatenate([a_f, r_n[u]], axis=0).astype(BF16),
                                 jnp.concatenate([bdiag(b_f), bdiag(k_f)], axis=0)))
            bk_n.append(jnp.concatenate([b_f, k_f], axis=0).astype(BF16))
            wcol = jnp.sum(eye2 * e_in[CHUNK - 1:CHUNK, :], axis=1, keepdims=True)
            wcb.append(jnp.broadcast_to(wcol, (two, LANES)))
        yield
        t_inv, pw, a_rb, av = [], [], [], []
        for u in us:
            a_ab = jnp.where(strict, a_all[u][:CHUNK, :LANES], 0.0)
            a_ak = jnp.where(strict, a_all[u][:CHUNK, LANES:], 0.0)
            a_rk = jnp.where(incl, a_all[u][CHUNK:, LANES:], 0.0)
            a_rb.append(jnp.where(incl, a_all[u][CHUNK:, :LANES], 0.0).astype(BF16))
            av.append(_dot(jnp.concatenate([a_ak, a_rk], axis=0).astype(BF16), v_b[u]))
            t_inv.append(eye1 + a_ab)
            pw.append(a_ab)
        yield
        pw = [_dot(pw[u].astype(BF16), bdiag(pw[u])) for u in us]
        yield
        for _ in range(4):
            both = [_dot(jnp.concatenate([t_inv[u], pw[u]], axis=0).astype(BF16), bdiag(pw[u])) for u in us]
            t_inv = [t_inv[u] + both[u][:CHUNK] for u in us]
            pw = [both[u][CHUNK:] for u in us]
            yield
        t_inv = [t_inv[u] + _dot(t_inv[u].astype(BF16), bdiag(pw[u])) for u in us]
        yield
        pq = [_dot(t_inv[u].astype(BF16),
                   jnp.concatenate([bdiag(a_n[u]), bdiag(av[u][:CHUNK])], axis=1)) for u in us]
        yield
        rq = [_dot(a_rb[u], jnp.concatenate([bdiag(pq[u][:, :LANES]), bdiag(pq[u][:, LANES:])], axis=1))
              for u in us]
        mg = [_dot_tn(bk_n[u], jnp.concatenate([pq[u].astype(BF16),
                                                jnp.concatenate([zeros_b, v_b[u][:CHUNK] + v_b[u][CHUNK:]], axis=1)],
                                               axis=0)) for u in us]
        yield
        for u in us:
            m_bd = jnp.where(same_head, mg[u][:, :LANES], 0.0) * wcb[u]
            rm_sc[u] = jnp.concatenate([r_n[u] + rq[u][:, :LANES], m_bd], axis=0).astype(BF16)
            g_sc[u] = jnp.where(same_head, mg[u][:, LANES:], 0.0) * wcb[u]
            y0_sc[u] = rq[u][:, LANES:] + av[u][CHUNK:]
            wc_sc[u] = wcb[u]

    def state_step(gi, i, hs):
        u, st = inst[i]
        res = _dot(rm_sc[i], hs[st].astype(BF16))
        y_sc[chunk_rows(gi * group + u), lanes(st)] = res[:CHUNK] + y0_sc[i]
        return tuple(wc_sc[i] * hs[st] + res[CHUNK:] + g_sc[i] if t == st else hs[t] for t in range(streams))

    inv_n = 1.0 / HEAD
    grows = group * CHUNK

    def finish(gi):
        rows = pl.ds(pl.multiple_of(gi * grows, grows), grows)
        for st in range(streams):
            sl = lanes(st)
            y = y_sc[rows, sl]
            mean = segsum(y, 1) * inv_n
            yc = y - mean
            var = segsum(yc * yc, 1) * inv_n
            gn = yc * lax.rsqrt(var + GN_EPS) * lnw_ref[:, sl] + lnb_ref[:, sl]
            kh = k_ref[rows, sl] * (1.0 + (ic_ref[rows, sl] - 1.0) * ka_p[st])
            bonus = segsum(r_ref[rows, sl] * kh * rk_ref[:, sl], 1) * v_ref[rows, sl]
            g = g_ref[rows, sl]
            o_ref[rows, sl] = ((gn + bonus) * (g / (1.0 + jnp.exp(-g)))).astype(o_ref.dtype)

    def run(a_gi, b_gi, c_gi, hs):
        done = 0
        if a_gi is None:
            for i in range(ninst):
                hs = state_step(b_gi, i, hs)
        else:
            per_slot = -(-ninst // 9)
            for i, _ in enumerate(phase_a(a_gi)):
                for _ in range(per_slot if (b_gi is not None and i >= 1) else 0):
                    if done < ninst:
                        hs = state_step(b_gi, done, hs)
                        done += 1
            assert b_gi is None or done == ninst
        if c_gi is not None:
            finish(c_gi)
        return hs

    run(0, None, None, None)
    hs = run(1, 0, None, tuple(jnp.zeros((two, LANES), F32) for _ in range(streams)))
    if ngroups > 2:
        hs = lax.fori_loop(1, ngroups - 1, lambda gi, hs: run(gi + 1, gi, gi - 1, hs), hs)
    run(None, ngroups - 1, ngroups - 2, hs)
    finish(ngroups - 1)


def _wkv(proj, lw, ic, kk, ka, rk, lnw, lnb, *, batch, seq):
    m, n4 = proj.shape
    w = n4 // 4
    group = WKV_GROUP
    streams = WKV_STREAMS if (w // LANES) % WKV_STREAMS == 0 else 1
    wide = streams * LANES
    nblk = w // wide
    two = 2 * CHUNK
    ninst = group * streams
    assert seq % (CHUNK * group) == 0 and seq // (CHUNK * group) >= 2

    def col_spec(p):
        return pl.BlockSpec((seq, wide), lambda b, hp: (b, p * nblk + hp))

    par_spec = pl.BlockSpec((1, wide), lambda b, hp: (0, hp))
    act_spec = pl.BlockSpec((seq, wide), lambda b, hp: (b, hp))
    return pl.pallas_call(
        functools.partial(_wkv_kernel, group=group),
        grid=(batch, nblk),
        in_specs=[col_spec(0), col_spec(1), col_spec(2), col_spec(3), act_spec, act_spec,
                  par_spec, par_spec, par_spec, par_spec, par_spec],
        out_specs=act_spec,
        out_shape=jax.ShapeDtypeStruct((m, w), BF16),
        scratch_shapes=[pltpu.VMEM((ninst, CHUNK + two, LANES), BF16),
                        pltpu.VMEM((ninst, CHUNK, LANES), F32),
                        pltpu.VMEM((ninst, two, LANES), F32),
                        pltpu.VMEM((ninst, two, LANES), F32),
                        pltpu.VMEM((seq, wide), F32)],
        compiler_params=_cparams(("parallel", "parallel")),
        name="wkv7",
    )(proj, proj, proj, proj, lw, ic, kk, ka, rk, lnw, lnb)


def _attn_kernel(q_ref, k_ref, v_ref, g_ref, lam_ref, sw_ref, o_ref,
                 z_a, z_b, p_a, p_b, acc_sc, ka1_sc, ka2_sc, q1_sc, q2_sc, *, tq, n_heads, lambda_init):
    seq = q_ref.shape[0]
    nq = seq // tq
    tk = tq
    half = 256
    ng = 2 * tq // LANES
    h = pl.program_id(1)
    lane = lax.broadcasted_iota(jnp.int32, (1, LANES), 1)
    slope = jnp.exp2(jnp.full((1, LANES), -8.0 / n_heads, F32) * (h + 1).astype(F32)) * LOG2E
    s_hi = slope.astype(BF16).astype(F32)
    s_lo = slope - s_hi
    s_pat = jnp.where(lane % 2 == 0, s_hi, s_lo)

    row_b = lax.broadcasted_iota(jnp.int32, (tk, LANES), 0)
    lane_b = lax.broadcasted_iota(jnp.int32, (tk, LANES), 1)
    kidx = jnp.where(lane_b % 4 < 2, row_b % half, (row_b // half) * half).astype(F32)
    aug1 = jnp.where((lane_b >= HEAD) & (lane_b < HEAD + 4), kidx, 0.0).astype(BF16)
    aug2 = jnp.where(lane_b < 4, kidx, 0.0).astype(BF16)
    lo_b = (lane_b < HEAD).astype(F32).astype(BF16)
    hi_b = (lane_b >= HEAD).astype(F32).astype(BF16)
    for j in range(seq // tk):
        ks = k_ref[j * tk:(j + 1) * tk, :]
        ka1_sc[j * tk:(j + 1) * tk, :] = ks * lo_b + aug1
        ka2_sc[j * tk:(j + 1) * tk, :] = ks * hi_b + aug2

    tri = (lax.broadcasted_iota(jnp.int32, (LANES, LANES), 0) <= lax.broadcasted_iota(jnp.int32, (LANES, LANES), 1))
    lam = lam_ref[...]
    lam_full = (jnp.exp(jnp.sum(lam[0:1] * lam[1:2], axis=1, keepdims=True))
                - jnp.exp(jnp.sum(lam[2:3] * lam[3:4], axis=1, keepdims=True)) + lambda_init)
    sw_scaled = sw_ref[...] * (1.0 - lambda_init)

    def load_q(qi):
        qf = q_ref[qi * tq:(qi + 1) * tq, :].astype(F32) * (LOG2E / math.sqrt(HEAD))
        q1_sc[...] = jnp.where(lane < HEAD, qf, jnp.where(lane < HEAD + 4, s_pat, 0.0)).astype(BF16)
        q2_sc[...] = jnp.where(lane >= HEAD, qf, jnp.where(lane < 4, s_pat, 0.0)).astype(BF16)

    def scores_to(j, z_ref, diag):
        hq = ng // 4
        for ka_sc, q_sc, g0 in ((ka1_sc, q1_sc, 0), (ka2_sc, q2_sc, ng // 2)):
            if not diag:
                z = _dot_nt(ka_sc[j * tk:(j + 1) * tk, :], q_sc[...])
                for g in range(ng // 2):
                    z_ref[g0 + g, 0:tk, :] = z[:, g * LANES:(g + 1) * LANES]
            else:
                top = _dot_nt(ka_sc[j * tk:j * tk + tk // 2, :], q_sc[...])
                bot = _dot_nt(ka_sc[j * tk + tk // 2:(j + 1) * tk, :], q_sc[tq // 2:, :])
                for g in range(ng // 2):
                    z_ref[g0 + g, 0:tk // 2, :] = top[:, g * LANES:(g + 1) * LANES]
                for g in range(hq):
                    z_ref[g0 + hq + g, tk // 2:tk, :] = bot[:, g * LANES:(g + 1) * LANES]

    def accumulate(j, p_ref, alpha, first, diag):
        hq = ng // 4
        if not diag:
            p_all = jnp.concatenate([p_ref[g, 0:tk, :] for g in range(ng)], axis=1)
            pv = _dot_tn(v_ref[j * tk:(j + 1) * tk, :], p_all)
            parts = [pv[:, g * LANES:(g + 1) * LANES] for g in range(ng)]
        else:
            late = [g for g in range(ng) if g % (ng // 2) >= hq]
            p_top = jnp.concatenate([p_ref[g, 0:tk // 2, :] for g in range(ng)], axis=1)
            p_bot = jnp.concatenate([p_ref[g, tk // 2:tk, :] for g in late], axis=1)
            pv_top = _dot_tn(v_ref[j * tk:j * tk + tk // 2, :], p_top)
            pv_bot = _dot_tn(v_ref[j * tk + tk // 2:(j + 1) * tk, :], p_bot)
            parts = [pv_top[:, g * LANES:(g + 1) * LANES] for g in range(ng)]
            for n, g in enumerate(late):
                parts[g] = parts[g] + pv_bot[:, n * LANES:(n + 1) * LANES]
        for g in range(ng):
            sl = slice(g * LANES, (g + 1) * LANES)
            acc_sc[g, 0:LANES, :] = parts[g] if first else alpha[:, sl] * acc_sc[g, 0:LANES, :] + parts[g]

    def softmax(qi, j, z_ref, p_ref, m, l):
        c_blk = slope * float((j - qi) * tq)
        ms, ls, alphas = [], [], []
        for g in range(ng):
            sl = slice(g * LANES, (g + 1) * LANES)
            nrows = tk
            if j == qi:
                nrows = (g % (ng // 2)) * LANES + LANES
                fill_to = tk // 2 if nrows <= tk // 2 else tk
                if nrows < fill_to:
                    p_ref[g, nrows:fill_to, :] = jnp.zeros((fill_to - nrows, LANES), BF16)
            diag_tile = jnp.where(tri, z_ref[g, nrows - LANES:nrows, :], NEG_BIG) if j == qi else None
            nplain = nrows - LANES if j == qi else nrows
            mx = diag_tile.max(axis=0, keepdims=True) if diag_tile is not None else None
            if nplain:
                mp = jnp.max(z_ref[g, 0:nplain, :], axis=0, keepdims=True)
                mx = mp if mx is None else jnp.maximum(mx, mp)
            mg = m[:, sl]
            mn = jnp.maximum(mg, mx + c_blk)
            alpha = jnp.exp2(mg - mn)
            shift = mn - c_blk
            lsum = None
            for r0 in range(0, nrows, LANES):
                zt = diag_tile if (j == qi and r0 == nplain) else z_ref[g, r0:r0 + LANES, :]
                pt = jnp.exp2(zt - shift)
                part = jnp.sum(pt, axis=0, keepdims=True)
                lsum = part if lsum is None else lsum + part
                p_ref[g, r0:r0 + LANES, :] = pt.astype(BF16)
            ls.append(alpha * l[:, sl] + lsum)
            ms.append(mn)
            alphas.append(alpha)
        return jnp.concatenate(ms, axis=1), jnp.concatenate(ls, axis=1), jnp.concatenate(alphas, axis=1)

    def finalize(qi, l):
        on = jnp.concatenate([acc_sc[g, 0:LANES, :] for g in range(ng)], axis=1) / l
        ot = on[:, :tq] - lam_full * on[:, tq:]
        ot = ot * lax.rsqrt(jnp.mean(ot * ot, axis=0, keepdims=True) + SUBLN_EPS)
        g = g_ref[qi * tq:(qi + 1) * tq, :].astype(F32)
        o_ref[qi * tq:(qi + 1) * tq, :] = (ot.T * sw_scaled * (g / (1.0 + jnp.exp(-g)))).astype(o_ref.dtype)

    items = [(qi, j) for qi in range(nq) for j in range(qi + 1)]
    zs, ps = (z_a, z_b), (p_a, p_b)
    load_q(0)
    scores_to(0, z_a, True)
    m = l = alpha = None
    for n, (qi, j) in enumerate(items):
        if n > 0:
            pqi, pj = items[n - 1]
            accumulate(pj, ps[(n - 1) % 2], alpha, pj == 0, pj == pqi)
            if pqi != qi:
                finalize(pqi, l)
        if j == 0:
            m = jnp.full((1, 2 * tq), NEG_BIG, F32)
            l = jnp.zeros((1, 2 * tq), F32)
        if n + 1 < len(items):
            nqi, nj = items[n + 1]
            if nqi != qi:
                load_q(nqi)
            scores_to(nj, zs[(n + 1) % 2], nj == nqi)
        m, l, alpha = softmax(qi, j, zs[n % 2], ps[n % 2], m, l)
    pqi, pj = items[-1]
    accumulate(pj, ps[(len(items) - 1) % 2], alpha, pj == 0, pj == pqi)
    finalize(pqi, l)


def _diff_attn(qkvg, lam, sw, *, batch, seq, tq, lambda_init):
    m, n4 = qkvg.shape
    w = n4 // 4
    nh = w // LANES
    ngl = 2 * tq // LANES

    def col_spec(p):
        return pl.BlockSpec((seq, LANES), lambda b, h: (b, p * nh + h))

    return pl.pallas_call(
        functools.partial(_attn_kernel, tq=tq, n_heads=nh, lambda_init=lambda_init),
        grid=(batch, nh),
        in_specs=[col_spec(0), col_spec(1), col_spec(2), col_spec(3),
                  pl.BlockSpec(lam.shape, lambda b, h: (0, 0)),
                  pl.BlockSpec((1, LANES), lambda b, h: (0, 0))],
        out_specs=pl.BlockSpec((seq, LANES), lambda b, h: (b, h)),
        out_shape=jax.ShapeDtypeStruct((m, w), BF16),
        scratch_shapes=[pltpu.VMEM((ngl, tq + SLAB_PAD, LANES), F32), pltpu.VMEM((ngl, tq + SLAB_PAD, LANES), F32),
                        pltpu.VMEM((ngl, tq + SLAB_PAD, LANES), BF16), pltpu.VMEM((ngl, tq + SLAB_PAD, LANES), BF16),
                        pltpu.VMEM((ngl, LANES + SLAB_PAD, LANES), F32),
                        pltpu.VMEM((seq, LANES), BF16), pltpu.VMEM((seq, LANES), BF16),
                        pltpu.VMEM((tq, LANES), BF16), pltpu.VMEM((tq, LANES), BF16)],
        compiler_params=_cparams(("parallel", "parallel")),
        name="diff_attn",
    )(qkvg, qkvg, qkvg, qkvg, lam, sw)


def kernel(x, norm_w, rwkv_mu, rwkv_w_in, rwkv_w0, rwkv_w1, rwkv_w2, rwkv_a0, rwkv_a1, rwkv_a2, rwkv_k_k, rwkv_k_a, rwkv_r_k, rwkv_ln_w, rwkv_ln_b, rwkv_w_out, diff_w_in, diff_lambda, diff_subln_w, diff_w_out, final_norm_w):
    batch, seq, d = x.shape
    m = batch * seq
    assert d % LANES == 0 and seq % CHUNK == 0
    assert norm_w.shape[0] == 2 and rwkv_mu.shape[0] == 1 and diff_w_in.shape[0] == 1
    tm_big = min(1024, m)
    tn_big = min(1024, d)
    tm_prep = min(512, seq)
    tm_last = min(512, m)
    tq = min(512, seq)

    x2 = x.reshape(m, d)
    row = lambda a: a.reshape(1, -1)
    lora = rwkv_w1.shape[-1]
    pad_c = lambda a: jnp.pad(a, ((0, 0), (0, LORA_PAD - lora)))
    pad_r = lambda a: jnp.pad(a, ((0, LORA_PAD - lora), (0, 0))).astype(BF16)

    mu = rwkv_mu[0]
    mu_rkvg = jnp.stack([mu[0], mu[2], mu[3], mu[5]])
    lora_in = jnp.concatenate([pad_c(rwkv_w1[0]), pad_c(rwkv_a1[0])], axis=1)
    lora_shift = jnp.concatenate([pad_c(mu[1][:, None] * rwkv_w1[0]), pad_c(mu[4][:, None] * rwkv_a1[0])], axis=1)
    wl1 = jnp.concatenate([lora_in, lora_shift], axis=0).astype(BF16)
    xs, lw, ic = _rwkv_prep(x2, row(norm_w[0]), mu_rkvg, row(rwkv_w0[0]), wl1, pad_r(rwkv_w2[0]),
                            row(rwkv_a0[0]), pad_r(rwkv_a2[0]), batch=batch, seq=seq, tm=tm_prep)
    proj = _matmul_wcast(xs, rwkv_w_in[0], tm=tm_big, tn=tn_big, out_dtype=F32, groups=4)
    yg = _wkv(proj, lw, ic, row(rwkv_k_k[0]), row(rwkv_k_a[0]), row(rwkv_r_k[0]), row(rwkv_ln_w[0]),
              row(rwkv_ln_b[0]), batch=batch, seq=seq)
    x1, h1 = _out_proj(yg, rwkv_w_out[0].astype(BF16), x2, row(norm_w[1]), tm=tm_last, keep_sum=True,
                       norm_dtype=BF16)

    lambda_init = 0.8 - 0.6 * math.exp(-0.3 * 1)
    qkvg = _matmul_wcast(h1, diff_w_in[0], tm=tm_big, tn=tn_big, out_dtype=BF16)
    att = _diff_attn(qkvg, diff_lambda[0], row(diff_subln_w[0]), batch=batch, seq=seq, tq=tq,
                     lambda_init=lambda_init)
    (out,) = _out_proj(att, diff_w_out[0].astype(BF16), x1, row(final_norm_w), tm=tm_last, keep_sum=False,
                       norm_dtype=F32)
    return out.reshape(batch, seq, d)
```

```python
import functools
import math

import jax
import jax.numpy as jnp
from jax import lax
from jax.experimental import pallas as pl
from jax.experimental.pallas import tpu as pltpu

F32 = jnp.float32
BF16 = jnp.bfloat16

NORM_EPS = 1e-6
GN_EPS = 64e-5
SUBLN_EPS = 1e-5
HEAD = 64
LANES = 128
CHUNK = 64
LORA_PAD = 128
VMEM_LIMIT = 48 * 1024 * 1024
NEG_BIG = -1e30
LOG2E = 1.0 / math.log(2.0)
BF16_EXACT_INT = 256

MM_TILE = 1024
ROW_TILE = 512
ATTN_BLOCK = 512


def _cparams(sem):
    return pltpu.CompilerParams(dimension_semantics=sem, vmem_limit_bytes=VMEM_LIMIT)


def _dot(a, b):
    return jnp.dot(a, b, preferred_element_type=F32)


def _dot_nt(a, b):
    return lax.dot_general(a, b, (((1,), (1,)), ((), ())), preferred_element_type=F32)


def _dot_tn(a, b):
    return lax.dot_general(a, b, (((0,), (0,)), ((), ())), preferred_element_type=F32)


def _mm_exact_rhs(a, e_bf16, parts):
    acc = None
    rem = a
    for _ in range(parts):
        piece = rem.astype(BF16)
        term = _dot(piece, e_bf16)
        acc = term if acc is None else acc + term
        rem = rem - piece.astype(F32)
    return acc


def _mm_exact_lhs(e_bf16, a, parts):
    acc = None
    rem = a
    for _ in range(parts):
        piece = rem.astype(BF16)
        term = _dot(e_bf16, piece)
        acc = term if acc is None else acc + term
        rem = rem - piece.astype(F32)
    return acc


def _rwkv_prep_kernel(x_ref, nw_ref, mu_ref, w0_ref, wl1_ref, w2_ref, a0_ref, a2_ref,
                      xs_ref, lw_ref, ic_ref, carry_ref):
    t = pl.program_id(1)
    x = x_ref[...]
    tm = x.shape[0]
    h = x * lax.rsqrt(jnp.mean(x * x, axis=-1, keepdims=True) + NORM_EPS) * nw_ref[...]

    @pl.when(t == 0)
    def _():
        carry_ref[...] = jnp.zeros_like(carry_ref)

    prev_last = carry_ref[0:1, :]
    rolled = pltpu.roll(h, 1, axis=0)
    row = lax.broadcasted_iota(jnp.int32, h.shape, 0)
    h_prev = jnp.where(row == 0, prev_last, rolled)
    carry_ref[0:1, :] = h[tm - 1:tm, :]
    d = h_prev - h
    mu = mu_ref[...]
    for p in range(4):
        xs_ref[p] = (h + d * mu[p:p + 1]).astype(BF16)
    t1 = _dot(jnp.concatenate([h, d], axis=1).astype(BF16), wl1_ref[...])
    zw = w0_ref[...] + _dot(jnp.tanh(t1[:, :LORA_PAD]).astype(BF16), w2_ref[...])
    lw_ref[...] = (-math.exp(-0.5)) / (1.0 + jnp.exp2(zw * (-LOG2E)))
    za = a0_ref[...] + _dot(t1[:, LORA_PAD:].astype(BF16), a2_ref[...])
    ic_ref[...] = 1.0 / (1.0 + jnp.exp2(za * (-LOG2E)))


def _rwkv_prep(x2, nw, mu, w0, wl1, w2p, a0, a2p, *, batch, seq, tm):
    m, d = x2.shape
    nt = seq // tm
    row_spec = pl.BlockSpec((tm, d), lambda b, t: (b * nt + t, 0))
    vec_spec = pl.BlockSpec((1, d), lambda b, t: (0, 0))

    def full(a):
        return pl.BlockSpec(a.shape, lambda b, t: (0,) * a.ndim)

    return pl.pallas_call(
        _rwkv_prep_kernel,
        grid=(batch, nt),
        in_specs=[row_spec, vec_spec, full(mu), vec_spec, full(wl1), full(w2p), vec_spec, full(a2p)],
        out_specs=[pl.BlockSpec((4, tm, d), lambda b, t: (0, b * nt + t, 0)), row_spec, row_spec],
        out_shape=[jax.ShapeDtypeStruct((4, m, d), BF16),
                   jax.ShapeDtypeStruct((m, d), F32),
                   jax.ShapeDtypeStruct((m, d), F32)],
        scratch_shapes=[pltpu.VMEM((8, d), F32)],
        compiler_params=_cparams(("parallel", "arbitrary")),
        name="rwkv_prep",
    )(x2, nw, mu, w0, wl1, w2p, a0, a2p)


def _mm_wcast_kernel(a_ref, b_ref, o_ref, w_sc):
    @pl.when(pl.program_id(1) == 0)
    def _():
        w_sc[...] = b_ref[...].astype(BF16)

    o_ref[...] = _dot(a_ref[...], w_sc[...]).astype(o_ref.dtype)


def _matmul_wcast(a, w, *, tm, tn, out_dtype, groups=1):
    m, k = a.shape[-2:]
    n = w.shape[1]
    per_group = (n // groups) // tn
    if groups > 1:
        a_spec = pl.BlockSpec((None, tm, k), lambda j, i: (j // per_group, i, 0))
    else:
        a_spec = pl.BlockSpec((tm, k), lambda j, i: (i, 0))
    return pl.pallas_call(
        _mm_wcast_kernel,
        grid=(n // tn, m // tm),
        in_specs=[a_spec, pl.BlockSpec((k, tn), lambda j, i: (0, j))],
        out_specs=pl.BlockSpec((tm, tn), lambda j, i: (i, j)),
        out_shape=jax.ShapeDtypeStruct((m, n), out_dtype),
        scratch_shapes=[pltpu.VMEM((k, tn), BF16)],
        compiler_params=_cparams(("parallel", "arbitrary")),
        name="proj" if groups > 1 else "matmul",
    )(a, w)


def _out_proj_kernel(a_ref, b_ref, r_ref, nw_ref, *o_refs):
    y = r_ref[...] + _dot(a_ref[...], b_ref[...])
    yn = y * lax.rsqrt(jnp.mean(y * y, axis=-1, keepdims=True) + NORM_EPS) * nw_ref[...]
    if len(o_refs) == 2:
        o_refs[0][...] = y
    o_refs[-1][...] = yn.astype(o_refs[-1].dtype)


def _out_proj(a, w, res, nw, *, tm, keep_sum, norm_dtype):
    m, k = a.shape
    n = w.shape[1]
    row_spec = pl.BlockSpec((tm, n), lambda i: (i, 0))
    out_specs = [row_spec, row_spec] if keep_sum else [row_spec]
    out_shape = [jax.ShapeDtypeStruct((m, n), norm_dtype)]
    if keep_sum:
        out_shape = [jax.ShapeDtypeStruct((m, n), F32)] + out_shape
    return pl.pallas_call(
        _out_proj_kernel,
        grid=(m // tm,),
        in_specs=[pl.BlockSpec((tm, k), lambda i: (i, 0)),
                  pl.BlockSpec((k, n), lambda i: (0, 0)),
                  row_spec,
                  pl.BlockSpec((1, n), lambda i: (0, 0))],
        out_specs=out_specs,
        out_shape=out_shape,
        compiler_params=_cparams(("parallel",)),
        name="out_proj_sum_norm" if keep_sum else "out_proj_norm",
    )(a, w, res, nw)


WKV_GROUP = 8
WKV_STREAMS = 2
WKV_STAGE_GAPS = 10


def _wkv_kernel(r_ref, k_ref, v_ref, g_ref, lw_ref, ic_ref, kk_ref, ka_ref, rk_ref, lnw_ref, lnb_ref,
                o_ref, rm_sc, y0_sc, g_sc, wc_sc, y_sc, *, group):
    seq = r_ref.shape[0]
    streams = r_ref.shape[1] // LANES
    ngroups = seq // (CHUNK * group)
    two = 2 * CHUNK
    inst = [(u, st) for u in range(group) for st in range(streams)]
    ninst = len(inst)

    def lanes(st):
        return slice(st * LANES, (st + 1) * LANES)

    lane1 = lax.broadcasted_iota(jnp.int32, (1, LANES), 1)
    m0 = (lane1 < HEAD).astype(F32)
    m1 = 1.0 - m0
    row2 = lax.broadcasted_iota(jnp.int32, (two, two), 0)
    col2 = lax.broadcasted_iota(jnp.int32, (two, two), 1)
    eye2 = (row2 == col2).astype(F32)
    same_head = (row2 < HEAD) == (col2 < HEAD)
    ones_bd = same_head.astype(BF16)
    row1 = lax.broadcasted_iota(jnp.int32, (CHUNK, LANES), 0)
    col1 = lax.broadcasted_iota(jnp.int32, (CHUNK, LANES), 1) % CHUNK
    strict = row1 > col1
    incl = row1 >= col1
    eye1 = (row1 == col1).astype(F32)
    tril_c = incl[:, :CHUNK].astype(BF16)
    zeros_b = jnp.zeros((CHUNK, LANES), BF16)

    kk_p = [kk_ref[:, lanes(st)] for st in range(streams)]
    ka_p = [ka_ref[:, lanes(st)] for st in range(streams)]

    def segsum(z, parts):
        return _mm_exact_rhs(z, ones_bd, parts)

    m0b = m0.astype(BF16)
    m1b = m1.astype(BF16)

    def bdiag(z):
        zb = z.astype(BF16)
        return jnp.concatenate([zb * m0b, zb * m1b], axis=0)

    def chunk_rows(c):
        return pl.ds(pl.multiple_of(c * CHUNK, CHUNK), CHUNK)

    def phase_a(gi):
        us = range(ninst)
        rows = [chunk_rows(gi * group + u) for u, _ in inst]
        cols = [lanes(st) for _, st in inst]
        k = [k_ref[rows[i], cols[i]] for i in us]
        lw = [lw_ref[rows[i], cols[i]] for i in us]
        kkv = [k[i] * kk_p[inst[i][1]] for i in us]
        n2 = [segsum(kkv[u] * kkv[u], 2) for u in us]
        lin = [_mm_exact_lhs(tril_c, lw[u], 2) for u in us]
        yield
        a_n, r_n, v_b, bk_n, wcb, a_all = [], [], [], [], [], []
        for u in us:
            ic = ic_ref[rows[u], cols[u]]
            kkn = kkv[u] / jnp.maximum(jnp.sqrt(n2[u]), 1e-12)
            kh = k[u] * (1.0 + (ic - 1.0) * ka_p[inst[u][1]])
            e_in = jnp.exp(lin[u])
            e_ex = jnp.exp(lin[u] - lw[u])
            e_neg = jnp.exp(-lin[u])
            b_f = kkn * ic * e_neg
            k_f = kh * e_neg
            a_f = -kkn * e_ex
            a_n.append(a_f)
            r_n.append(r_ref[rows[u], cols[u]] * e_in)
            v_b.append(bdiag(v_ref[rows[u], cols[u]]))
            a_all.append(_dot_nt(jnp.concatenate([a_f, r_n[u]], axis=0).astype(BF16),
                                 jnp.concatenate([bdiag(b_f), bdiag(k_f)], axis=0)))
            bk_n.append(jnp.concatenate([b_f, k_f], axis=0).astype(BF16))
            wcol = jnp.sum(eye2 * e_in[CHUNK - 1:CHUNK, :], axis=1, keepdims=True)
            wcb.append(jnp.broadcast_to(wcol, (two, LANES)))
        yield
        t_inv, pw, a_rb, av = [], [], [], []
        for u in us:
            a_ab = jnp.where(strict, a_all[u][:CHUNK, :LANES], 0.0)
            a_ak = jnp.where(strict, a_all[u][:CHUNK, LANES:], 0.0)
            a_rk = jnp.where(incl, a_all[u][CHUNK:, LANES:], 0.0)
            a_rb.append(jnp.where(incl, a_all[u][CHUNK:, :LANES], 0.0).astype(BF16))
            av.append(_dot(jnp.concatenate([a_ak, a_rk], axis=0).astype(BF16), v_b[u]))
            t_inv.append(eye1 + a_ab)
            pw.append(a_ab)
        yield
        pw = [_dot(pw[u].astype(BF16), bdiag(pw[u])) for u in us]
        yield
        for _ in range(4):
            both = [_dot(jnp.concatenate([t_inv[u], pw[u]], axis=0).astype(BF16), bdiag(pw[u])) for u in us]
            t_inv = [t_inv[u] + both[u][:CHUNK] for u in us]
            pw = [both[u][CHUNK:] for u in us]
            yield
        t_inv = [t_inv[u] + _dot(t_inv[u].astype(BF16), bdiag(pw[u])) for u in us]
        yield
        pq = [_dot(t_inv[u].astype(BF16),
                   jnp.concatenate([bdiag(a_n[u]), bdiag(av[u][:CHUNK])], axis=1)) for u in us]
        yield
        rq = [_dot(a_rb[u], jnp.concatenate([bdiag(pq[u][:, :LANES]), bdiag(pq[u][:, LANES:])], axis=1))
              for u in us]
        mg = [_dot_tn(bk_n[u], jnp.concatenate([pq[u].astype(BF16),
                                                jnp.concatenate([zeros_b, v_b[u][:CHUNK] + v_b[u][CHUNK:]], axis=1)],
                                               axis=0)) for u in us]
        yield
        for u in us:
            m_bd = jnp.where(same_head, mg[u][:, :LANES], 0.0) * wcb[u]
            rm_sc[u] = jnp.concatenate([r_n[u] + rq[u][:, :LANES], m_bd], axis=0).astype(BF16)
            g_sc[u] = jnp.where(same_head, mg[u][:, LANES:], 0.0) * wcb[u]
            y0_sc[u] = rq[u][:, LANES:] + av[u][CHUNK:]
            wc_sc[u] = wcb[u]

    def state_step(gi, i, hs):
        u, st = inst[i]
        res = _dot(rm_sc[i], hs[st].astype(BF16))
        y_sc[chunk_rows(gi * group + u), lanes(st)] = res[:CHUNK] + y0_sc[i]
        return tuple(wc_sc[i] * hs[st] + res[CHUNK:] + g_sc[i] if t == st else hs[t] for t in range(streams))

    inv_n = 1.0 / HEAD
    grows = group * CHUNK

    def finish(gi):
        rows = pl.ds(pl.multiple_of(gi * grows, grows), grows)
        for st in range(streams):
            sl = lanes(st)
            y = y_sc[rows, sl]
            mean = segsum(y, 1) * inv_n
            yc = y - mean
            var = segsum(yc * yc, 1) * inv_n
            gn = yc * lax.rsqrt(var + GN_EPS) * lnw_ref[:, sl] + lnb_ref[:, sl]
            kh = k_ref[rows, sl] * (1.0 + (ic_ref[rows, sl] - 1.0) * ka_p[st])
            bonus = segsum(r_ref[rows, sl] * kh * rk_ref[:, sl], 1) * v_ref[rows, sl]
            g = g_ref[rows, sl]
            o_ref[rows, sl] = ((gn + bonus) * (g / (1.0 + jnp.exp(-g)))).astype(o_ref.dtype)

    def run(a_gi, b_gi, c_gi, hs):
        done = 0
        if a_gi is None:
            for i in range(ninst):
                hs = state_step(b_gi, i, hs)
        else:
            per_slot = -(-ninst // WKV_STAGE_GAPS)
            nyield = 0
            for i, _ in enumerate(phase_a(a_gi)):
                nyield += 1
                for _ in range(per_slot if (b_gi is not None and i >= 1) else 0):
                    if done < ninst:
                        hs = state_step(b_gi, done, hs)
                        done += 1
            assert nyield == WKV_STAGE_GAPS + 1 and (b_gi is None or done == ninst)
        if c_gi is not None:
            finish(c_gi)
        return hs

    run(0, None, None, None)
    hs = run(1, 0, None, tuple(jnp.zeros((two, LANES), F32) for _ in range(streams)))
    if ngroups > 2:
        hs = lax.fori_loop(1, ngroups - 1, lambda gi, hs: run(gi + 1, gi, gi - 1, hs), hs)
    run(None, ngroups - 1, ngroups - 2, hs)
    finish(ngroups - 1)


def _wkv(proj, lw, ic, kk, ka, rk, lnw, lnb, *, batch, seq):
    m, n4 = proj.shape
    w = n4 // 4
    group = WKV_GROUP
    streams = WKV_STREAMS if (w // LANES) % WKV_STREAMS == 0 else 1
    wide = streams * LANES
    nblk = w // wide
    two = 2 * CHUNK
    ninst = group * streams
    assert seq % (CHUNK * group) == 0 and seq // (CHUNK * group) >= 2

    def col_spec(p):
        return pl.BlockSpec((seq, wide), lambda b, hp: (b, p * nblk + hp))

    par_spec = pl.BlockSpec((1, wide), lambda b, hp: (0, hp))
    act_spec = pl.BlockSpec((seq, wide), lambda b, hp: (b, hp))
    return pl.pallas_call(
        functools.partial(_wkv_kernel, group=group),
        grid=(batch, nblk),
        in_specs=[col_spec(0), col_spec(1), col_spec(2), col_spec(3), act_spec, act_spec,
                  par_spec, par_spec, par_spec, par_spec, par_spec],
        out_specs=act_spec,
        out_shape=jax.ShapeDtypeStruct((m, w), BF16),
        scratch_shapes=[pltpu.VMEM((ninst, CHUNK + two, LANES), BF16),
                        pltpu.VMEM((ninst, CHUNK, LANES), F32),
                        pltpu.VMEM((ninst, two, LANES), F32),
                        pltpu.VMEM((ninst, two, LANES), F32),
                        pltpu.VMEM((seq, wide), F32)],
        compiler_params=_cparams(("parallel", "parallel")),
        name="wkv7",
    )(proj, proj, proj, proj, lw, ic, kk, ka, rk, lnw, lnb)


def _attn_kernel(q_ref, k_ref, v_ref, g_ref, lam_ref, sw_ref, o_ref,
                 z_a, z_b, p_a, p_b, acc_sc, ka1_sc, ka2_sc, q1_sc, q2_sc, *, tq, n_heads, lambda_init):
    seq = q_ref.shape[0]
    nq = seq // tq
    tk = tq
    half = BF16_EXACT_INT
    assert tk <= 2 * half
    ng = 2 * tq // LANES
    h = pl.program_id(1)
    lane = lax.broadcasted_iota(jnp.int32, (1, LANES), 1)
    slope = jnp.exp2(jnp.full((1, LANES), -8.0 / n_heads, F32) * (h + 1).astype(F32)) * LOG2E
    s_hi = slope.astype(BF16).astype(F32)
    s_lo = slope - s_hi
    s_pat = jnp.where(lane % 2 == 0, s_hi, s_lo)

    row_b = lax.broadcasted_iota(jnp.int32, (tk, LANES), 0)
    lane_b = lax.broadcasted_iota(jnp.int32, (tk, LANES), 1)
    kidx = jnp.where(lane_b % 4 < 2, row_b % half, (row_b // half) * half).astype(F32)
    aug1 = jnp.where((lane_b >= HEAD) & (lane_b < HEAD + 4), kidx, 0.0).astype(BF16)
    aug2 = jnp.where(lane_b < 4, kidx, 0.0).astype(BF16)
    lo_b = (lane_b < HEAD).astype(F32).astype(BF16)
    hi_b = (lane_b >= HEAD).astype(F32).astype(BF16)
    for j in range(seq // tk):
        ks = k_ref[j * tk:(j + 1) * tk, :]
        ka1_sc[j * tk:(j + 1) * tk, :] = ks * lo_b + aug1
        ka2_sc[j * tk:(j + 1) * tk, :] = ks * hi_b + aug2

    tri = (lax.broadcasted_iota(jnp.int32, (LANES, LANES), 0) <= lax.broadcasted_iota(jnp.int32, (LANES, LANES), 1))
    lam = lam_ref[...]
    lam_full = (jnp.exp(jnp.sum(lam[0:1] * lam[1:2], axis=1, keepdims=True))
                - jnp.exp(jnp.sum(lam[2:3] * lam[3:4], axis=1, keepdims=True)) + lambda_init)
    sw_scaled = sw_ref[...] * (1.0 - lambda_init)

    def load_q(qi):
        qf = q_ref[qi * tq:(qi + 1) * tq, :].astype(F32) * (LOG2E / math.sqrt(HEAD))
        q1_sc[...] = jnp.where(lane < HEAD, qf, jnp.where(lane < HEAD + 4, s_pat, 0.0)).astype(BF16)
        q2_sc[...] = jnp.where(lane >= HEAD, qf, jnp.where(lane < 4, s_pat, 0.0)).astype(BF16)

    def scores_to(j, z_ref, diag):
        hq = ng // 4
        for ka_sc, q_sc, g0 in ((ka1_sc, q1_sc, 0), (ka2_sc, q2_sc, ng // 2)):
            if not diag:
                z = _dot_nt(ka_sc[j * tk:(j + 1) * tk, :], q_sc[...])
                for g in range(ng // 2):
                    z_ref[g0 + g] = z[:, g * LANES:(g + 1) * LANES]
            else:
                top = _dot_nt(ka_sc[j * tk:j * tk + tk // 2, :], q_sc[...])
                bot = _dot_nt(ka_sc[j * tk + tk // 2:(j + 1) * tk, :], q_sc[tq // 2:, :])
                for g in range(ng // 2):
                    z_ref[g0 + g, 0:tk // 2, :] = top[:, g * LANES:(g + 1) * LANES]
                for g in range(hq):
                    z_ref[g0 + hq + g, tk // 2:tk, :] = bot[:, g * LANES:(g + 1) * LANES]

    def accumulate(j, p_ref, alpha, first, diag):
        hq = ng // 4
        if not diag:
            p_all = jnp.concatenate([p_ref[g] for g in range(ng)], axis=1)
            pv = _dot_tn(v_ref[j * tk:(j + 1) * tk, :], p_all)
            parts = [pv[:, g * LANES:(g + 1) * LANES] for g in range(ng)]
        else:
            late = [g for g in range(ng) if g % (ng // 2) >= hq]
            p_top = jnp.concatenate([p_ref[g, 0:tk // 2, :] for g in range(ng)], axis=1)
            p_bot = jnp.concatenate([p_ref[g, tk // 2:tk, :] for g in late], axis=1)
            pv_top = _dot_tn(v_ref[j * tk:j * tk + tk // 2, :], p_top)
            pv_bot = _dot_tn(v_ref[j * tk + tk // 2:(j + 1) * tk, :], p_bot)
            parts = [pv_top[:, g * LANES:(g + 1) * LANES] for g in range(ng)]
            for n, g in enumerate(late):
                parts[g] = parts[g] + pv_bot[:, n * LANES:(n + 1) * LANES]
        for g in range(ng):
            sl = slice(g * LANES, (g + 1) * LANES)
            acc_sc[g] = parts[g] if first else alpha[:, sl] * acc_sc[g] + parts[g]

    def softmax(qi, j, z_ref, p_ref, m, l):
        c_blk = slope * float((j - qi) * tq)
        ms, ls, alphas = [], [], []
        for g in range(ng):
            sl = slice(g * LANES, (g + 1) * LANES)
            nrows = tk
            if j == qi:
                nrows = (g % (ng // 2)) * LANES + LANES
                fill_to = tk // 2 if nrows <= tk // 2 else tk
                if nrows < fill_to:
                    p_ref[g, nrows:fill_to, :] = jnp.zeros((fill_to - nrows, LANES), BF16)
            diag_tile = jnp.where(tri, z_ref[g, nrows - LANES:nrows, :], NEG_BIG) if j == qi else None
            nplain = nrows - LANES if j == qi else nrows
            mx = diag_tile.max(axis=0, keepdims=True) if diag_tile is not None else None
            if nplain:
                mp = jnp.max(z_ref[g, 0:nplain, :], axis=0, keepdims=True)
                mx = mp if mx is None else jnp.maximum(mx, mp)
            mg = m[:, sl]
            mn = jnp.maximum(mg, mx + c_blk)
            alpha = jnp.exp2(mg - mn)
            shift = mn - c_blk
            lsum = None
            for r0 in range(0, nrows, LANES):
                zt = diag_tile if (j == qi and r0 == nplain) else z_ref[g, r0:r0 + LANES, :]
                pt = jnp.exp2(zt - shift)
                part = jnp.sum(pt, axis=0, keepdims=True)
                lsum = part if lsum is None else lsum + part
                p_ref[g, r0:r0 + LANES, :] = pt.astype(BF16)
            ls.append(alpha * l[:, sl] + lsum)
            ms.append(mn)
            alphas.append(alpha)
        return jnp.concatenate(ms, axis=1), jnp.concatenate(ls, axis=1), jnp.concatenate(alphas, axis=1)

    def finalize(qi, l):
        on = jnp.concatenate([acc_sc[g] for g in range(ng)], axis=1) / l
        ot = on[:, :tq] - lam_full * on[:, tq:]
        ot = ot * lax.rsqrt(jnp.mean(ot * ot, axis=0, keepdims=True) + SUBLN_EPS)
        g = g_ref[qi * tq:(qi + 1) * tq, :].astype(F32)
        o_ref[qi * tq:(qi + 1) * tq, :] = (ot.T * sw_scaled * (g / (1.0 + jnp.exp(-g)))).astype(o_ref.dtype)

    items = [(qi, j) for qi in range(nq) for j in range(qi + 1)]
    zs, ps = (z_a, z_b), (p_a, p_b)
    load_q(0)
    scores_to(0, z_a, True)
    m = l = alpha = None
    for n, (qi, j) in enumerate(items):
        if n > 0:
            pqi, pj = items[n - 1]
            accumulate(pj, ps[(n - 1) % 2], alpha, pj == 0, pj == pqi)
            if pqi != qi:
                finalize(pqi, l)
        if j == 0:
            m = jnp.full((1, 2 * tq), NEG_BIG, F32)
            l = jnp.zeros((1, 2 * tq), F32)
        if n + 1 < len(items):
            nqi, nj = items[n + 1]
            if nqi != qi:
                load_q(nqi)
            scores_to(nj, zs[(n + 1) % 2], nj == nqi)
        m, l, alpha = softmax(qi, j, zs[n % 2], ps[n % 2], m, l)
    pqi, pj = items[-1]
    accumulate(pj, ps[(len(items) - 1) % 2], alpha, pj == 0, pj == pqi)
    finalize(pqi, l)


def _diff_attn(qkvg, lam, sw, *, batch, seq, tq, lambda_init):
    m, n4 = qkvg.shape
    w = n4 // 4
    nh = w // LANES
    ngl = 2 * tq // LANES

    def col_spec(p):
        return pl.BlockSpec((seq, LANES), lambda b, h: (b, p * nh + h))

    return pl.pallas_call(
        functools.partial(_attn_kernel, tq=tq, n_heads=nh, lambda_init=lambda_init),
        grid=(batch, nh),
        in_specs=[col_spec(0), col_spec(1), col_spec(2), col_spec(3),
                  pl.BlockSpec(lam.shape, lambda b, h: (0, 0)),
                  pl.BlockSpec((1, LANES), lambda b, h: (0, 0))],
        out_specs=pl.BlockSpec((seq, LANES), lambda b, h: (b, h)),
        out_shape=jax.ShapeDtypeStruct((m, w), BF16),
        scratch_shapes=[pltpu.VMEM((ngl, tq, LANES), F32), pltpu.VMEM((ngl, tq, LANES), F32),
                        pltpu.VMEM((ngl, tq, LANES), BF16), pltpu.VMEM((ngl, tq, LANES), BF16),
                        pltpu.VMEM((ngl, LANES, LANES), F32),
                        pltpu.VMEM((seq, LANES), BF16), pltpu.VMEM((seq, LANES), BF16),
                        pltpu.VMEM((tq, LANES), BF16), pltpu.VMEM((tq, LANES), BF16)],
        compiler_params=_cparams(("parallel", "parallel")),
        name="diff_attn",
    )(qkvg, qkvg, qkvg, qkvg, lam, sw)


def kernel(x, norm_w, rwkv_mu, rwkv_w_in, rwkv_w0, rwkv_w1, rwkv_w2, rwkv_a0, rwkv_a1, rwkv_a2, rwkv_k_k, rwkv_k_a, rwkv_r_k, rwkv_ln_w, rwkv_ln_b, rwkv_w_out, diff_w_in, diff_lambda, diff_subln_w, diff_w_out, final_norm_w):
    batch, seq, d = x.shape
    m = batch * seq
    assert d % LANES == 0 and seq % CHUNK == 0
    assert norm_w.shape[0] == 2 and rwkv_mu.shape[0] == 1 and diff_w_in.shape[0] == 1
    tm_big = min(MM_TILE, m)
    tn_big = min(MM_TILE, d)
    tm_prep = min(ROW_TILE, seq)
    tm_last = min(ROW_TILE, m)
    tq = min(ATTN_BLOCK, seq)

    x2 = x.reshape(m, d)
    row = lambda a: a.reshape(1, -1)
    lora = rwkv_w1.shape[-1]
    pad_c = lambda a: jnp.pad(a, ((0, 0), (0, LORA_PAD - lora)))
    pad_r = lambda a: jnp.pad(a, ((0, LORA_PAD - lora), (0, 0))).astype(BF16)

    mu = rwkv_mu[0]
    mu_rkvg = jnp.stack([mu[0], mu[2], mu[3], mu[5]])
    lora_in = jnp.concatenate([pad_c(rwkv_w1[0]), pad_c(rwkv_a1[0])], axis=1)
    lora_shift = jnp.concatenate([pad_c(mu[1][:, None] * rwkv_w1[0]), pad_c(mu[4][:, None] * rwkv_a1[0])], axis=1)
    wl1 = jnp.concatenate([lora_in, lora_shift], axis=0).astype(BF16)
    xs, lw, ic = _rwkv_prep(x2, row(norm_w[0]), mu_rkvg, row(rwkv_w0[0]), wl1, pad_r(rwkv_w2[0]),
                            row(rwkv_a0[0]), pad_r(rwkv_a2[0]), batch=batch, seq=seq, tm=tm_prep)
    proj = _matmul_wcast(xs, rwkv_w_in[0], tm=tm_big, tn=tn_big, out_dtype=F32, groups=4)
    yg = _wkv(proj, lw, ic, row(rwkv_k_k[0]), row(rwkv_k_a[0]), row(rwkv_r_k[0]), row(rwkv_ln_w[0]),
              row(rwkv_ln_b[0]), batch=batch, seq=seq)
    x1, h1 = _out_proj(yg, rwkv_w_out[0].astype(BF16), x2, row(norm_w[1]), tm=tm_last, keep_sum=True,
                       norm_dtype=BF16)

    lambda_init = 0.8 - 0.6 * math.exp(-0.3 * 1)
    qkvg = _matmul_wcast(h1, diff_w_in[0], tm=tm_big, tn=tn_big, out_dtype=BF16)
    att = _diff_attn(qkvg, diff_lambda[0], row(diff_subln_w[0]), batch=batch, seq=seq, tq=tq,
                     lambda_init=lambda_init)
    (out,) = _out_proj(att, diff_w_out[0].astype(BF16), x1, row(final_norm_w), tm=tm_last, keep_sum=False,
                       norm_dtype=F32)
    return out.reshape(batch, seq, d)
```

```python
import functools
import math

import jax
import jax.numpy as jnp
from jax import lax
from jax.experimental import pallas as pl
from jax.experimental.pallas import tpu as pltpu

F32 = jnp.float32
BF16 = jnp.bfloat16

NORM_EPS = 1e-6
GN_EPS = 64e-5
SUBLN_EPS = 1e-5
HEAD = 64
LANES = 128
CHUNK = 64
LORA_PAD = 128
VMEM_LIMIT = 48 * 1024 * 1024
NEG_BIG = -1e30
LOG2E = 1.0 / math.log(2.0)
BF16_EXACT_INT = 256

MM_TILE = 1024
MM_ROWS_BF16_OUT = 2048
MM_VMEM_LIMIT_BF16_OUT = 56 * 1024 * 1024
ROW_TILE = 512
ATTN_BLOCK = 512


def _cparams(sem, vmem_limit=VMEM_LIMIT):
    return pltpu.CompilerParams(dimension_semantics=sem, vmem_limit_bytes=vmem_limit)


def _dot(a, b):
    return jnp.dot(a, b, preferred_element_type=F32)


def _dot_nt(a, b):
    return lax.dot_general(a, b, (((1,), (1,)), ((), ())), preferred_element_type=F32)


def _dot_tn(a, b):
    return lax.dot_general(a, b, (((0,), (0,)), ((), ())), preferred_element_type=F32)


def _mm_exact_rhs(a, e_bf16, parts):
    acc = None
    rem = a
    for _ in range(parts):
        piece = rem.astype(BF16)
        term = _dot(piece, e_bf16)
        acc = term if acc is None else acc + term
        rem = rem - piece.astype(F32)
    return acc


def _mm_exact_lhs(e_bf16, a, parts):
    acc = None
    rem = a
    for _ in range(parts):
        piece = rem.astype(BF16)
        term = _dot(e_bf16, piece)
        acc = term if acc is None else acc + term
        rem = rem - piece.astype(F32)
    return acc


def _rwkv_prep_kernel(x_ref, nw_ref, mu_ref, w0_ref, wl1_ref, w2_ref, a0_ref, a2_ref,
                      xs_ref, lw_ref, ic_ref, carry_ref):
    t = pl.program_id(1)
    x = x_ref[...]
    tm = x.shape[0]
    h = x * lax.rsqrt(jnp.mean(x * x, axis=-1, keepdims=True) + NORM_EPS) * nw_ref[...]

    @pl.when(t == 0)
    def _():
        carry_ref[...] = jnp.zeros_like(carry_ref)

    prev_last = carry_ref[0:1, :]
    rolled = pltpu.roll(h, 1, axis=0)
    row = lax.broadcasted_iota(jnp.int32, h.shape, 0)
    h_prev = jnp.where(row == 0, prev_last, rolled)
    carry_ref[0:1, :] = h[tm - 1:tm, :]
    d = h_prev - h
    mu = mu_ref[...]
    for p in range(4):
        xs_ref[p] = (h + d * mu[p:p + 1]).astype(BF16)
    t1 = _dot(jnp.concatenate([h, d], axis=1).astype(BF16), wl1_ref[...])
    zw = w0_ref[...] + _dot(jnp.tanh(t1[:, :LORA_PAD]).astype(BF16), w2_ref[...])
    lw_ref[...] = (-math.exp(-0.5)) / (1.0 + jnp.exp2(zw * (-LOG2E)))
    za = a0_ref[...] + _dot(t1[:, LORA_PAD:].astype(BF16), a2_ref[...])
    ic_ref[...] = 1.0 / (1.0 + jnp.exp2(za * (-LOG2E)))


def _rwkv_prep(x2, nw, mu, w0, wl1, w2p, a0, a2p, *, batch, seq, tm):
    m, d = x2.shape
    nt = seq // tm
    row_spec = pl.BlockSpec((tm, d), lambda b, t: (b * nt + t, 0))
    vec_spec = pl.BlockSpec((1, d), lambda b, t: (0, 0))

    def full(a):
        return pl.BlockSpec(a.shape, lambda b, t: (0,) * a.ndim)

    return pl.pallas_call(
        _rwkv_prep_kernel,
        grid=(batch, nt),
        in_specs=[row_spec, vec_spec, full(mu), vec_spec, full(wl1), full(w2p), vec_spec, full(a2p)],
        out_specs=[pl.BlockSpec((4, tm, d), lambda b, t: (0, b * nt + t, 0)), row_spec, row_spec],
        out_shape=[jax.ShapeDtypeStruct((4, m, d), BF16),
                   jax.ShapeDtypeStruct((m, d), F32),
                   jax.ShapeDtypeStruct((m, d), F32)],
        scratch_shapes=[pltpu.VMEM((8, d), F32)],
        compiler_params=_cparams(("parallel", "arbitrary")),
        name="rwkv_prep",
    )(x2, nw, mu, w0, wl1, w2p, a0, a2p)


def _mm_wcast_kernel(a_ref, b_ref, o_ref, w_sc):
    @pl.when(pl.program_id(1) == 0)
    def _():
        w_sc[...] = b_ref[...].astype(BF16)

    o_ref[...] = _dot(a_ref[...], w_sc[...]).astype(o_ref.dtype)


def _matmul_wcast(a, w, *, tm, tn, out_dtype, groups=1, vmem_limit=VMEM_LIMIT):
    m, k = a.shape[-2:]
    n = w.shape[1]
    per_group = (n // groups) // tn
    if groups > 1:
        a_spec = pl.BlockSpec((None, tm, k), lambda j, i: (j // per_group, i, 0))
    else:
        a_spec = pl.BlockSpec((tm, k), lambda j, i: (i, 0))
    return pl.pallas_call(
        _mm_wcast_kernel,
        grid=(n // tn, m // tm),
        in_specs=[a_spec, pl.BlockSpec((k, tn), lambda j, i: (0, j))],
        out_specs=pl.BlockSpec((tm, tn), lambda j, i: (i, j)),
        out_shape=jax.ShapeDtypeStruct((m, n), out_dtype),
        scratch_shapes=[pltpu.VMEM((k, tn), BF16)],
        compiler_params=_cparams(("parallel", "arbitrary"), vmem_limit),
        name="proj" if groups > 1 else "matmul",
    )(a, w)


def _out_proj_kernel(a_ref, b_ref, r_ref, nw_ref, *o_refs):
    y = r_ref[...] + _dot(a_ref[...], b_ref[...])
    yn = y * lax.rsqrt(jnp.mean(y * y, axis=-1, keepdims=True) + NORM_EPS) * nw_ref[...]
    if len(o_refs) == 2:
        o_refs[0][...] = y
    o_refs[-1][...] = yn.astype(o_refs[-1].dtype)


def _out_proj(a, w, res, nw, *, tm, keep_sum, norm_dtype):
    m, k = a.shape
    n = w.shape[1]
    row_spec = pl.BlockSpec((tm, n), lambda i: (i, 0))
    out_specs = [row_spec, row_spec] if keep_sum else [row_spec]
    out_shape = [jax.ShapeDtypeStruct((m, n), norm_dtype)]
    if keep_sum:
        out_shape = [jax.ShapeDtypeStruct((m, n), F32)] + out_shape
    return pl.pallas_call(
        _out_proj_kernel,
        grid=(m // tm,),
        in_specs=[pl.BlockSpec((tm, k), lambda i: (i, 0)),
                  pl.BlockSpec((k, n), lambda i: (0, 0)),
                  row_spec,
                  pl.BlockSpec((1, n), lambda i: (0, 0))],
        out_specs=out_specs,
        out_shape=out_shape,
        compiler_params=_cparams(("parallel",)),
        name="out_proj_sum_norm" if keep_sum else "out_proj_norm",
    )(a, w, res, nw)


WKV_GROUP = 8
WKV_STREAMS = 2
WKV_STAGE_GAPS = 10


def _wkv_kernel(r_ref, k_ref, v_ref, g_ref, lw_ref, ic_ref, kk_ref, ka_ref, rk_ref, lnw_ref, lnb_ref,
                o_ref, rm_sc, y0_sc, g_sc, wc_sc, y_sc, *, group):
    seq = r_ref.shape[0]
    streams = r_ref.shape[1] // LANES
    ngroups = seq // (CHUNK * group)
    two = 2 * CHUNK
    inst = [(u, st) for u in range(group) for st in range(streams)]
    ninst = len(inst)

    def lanes(st):
        return slice(st * LANES, (st + 1) * LANES)

    lane1 = lax.broadcasted_iota(jnp.int32, (1, LANES), 1)
    m0 = (lane1 < HEAD).astype(F32)
    m1 = 1.0 - m0
    row2 = lax.broadcasted_iota(jnp.int32, (two, two), 0)
    col2 = lax.broadcasted_iota(jnp.int32, (two, two), 1)
    eye2 = (row2 == col2).astype(F32)
    same_head = (row2 < HEAD) == (col2 < HEAD)
    ones_bd = same_head.astype(BF16)
    row1 = lax.broadcasted_iota(jnp.int32, (CHUNK, LANES), 0)
    col1 = lax.broadcasted_iota(jnp.int32, (CHUNK, LANES), 1) % CHUNK
    strict = row1 > col1
    incl = row1 >= col1
    eye1 = (row1 == col1).astype(F32)
    tril_c = incl[:, :CHUNK].astype(BF16)
    zeros_b = jnp.zeros((CHUNK, LANES), BF16)

    kk_p = [kk_ref[:, lanes(st)] for st in range(streams)]
    ka_p = [ka_ref[:, lanes(st)] for st in range(streams)]

    def segsum(z, parts):
        return _mm_exact_rhs(z, ones_bd, parts)

    m0b = m0.astype(BF16)
    m1b = m1.astype(BF16)

    def bdiag(z):
        zb = z.astype(BF16)
        return jnp.concatenate([zb * m0b, zb * m1b], axis=0)

    def chunk_rows(c):
        return pl.ds(pl.multiple_of(c * CHUNK, CHUNK), CHUNK)

    def phase_a(gi):
        us = range(ninst)
        rows = [chunk_rows(gi * group + u) for u, _ in inst]
        cols = [lanes(st) for _, st in inst]
        k = [k_ref[rows[i], cols[i]] for i in us]
        lw = [lw_ref[rows[i], cols[i]] for i in us]
        kkv = [k[i] * kk_p[inst[i][1]] for i in us]
        n2 = [segsum(kkv[u] * kkv[u], 2) for u in us]
        lin = [_mm_exact_lhs(tril_c, lw[u], 2) for u in us]
        yield
        a_n, r_n, v_b, bk_n, wcb, a_all = [], [], [], [], [], []
        for u in us:
            ic = ic_ref[rows[u], cols[u]]
            kkn = kkv[u] / jnp.maximum(jnp.sqrt(n2[u]), 1e-12)
            kh = k[u] * (1.0 + (ic - 1.0) * ka_p[inst[u][1]])
            e_in = jnp.exp(lin[u])
            e_ex = jnp.exp(lin[u] - lw[u])
            e_neg = jnp.exp(-lin[u])
            b_f = kkn * ic * e_neg
            k_f = kh * e_neg
            a_f = -kkn * e_ex
            a_n.append(a_f)
            r_n.append(r_ref[rows[u], cols[u]] * e_in)
            v_b.append(bdiag(v_ref[rows[u], cols[u]]))
            a_all.append(_dot_nt(jnp.concatenate([a_f, r_n[u]], axis=0).astype(BF16),
                                 jnp.concatenate([bdiag(b_f), bdiag(k_f)], axis=0)))
            bk_n.append(jnp.concatenate([b_f, k_f], axis=0).astype(BF16))
            wcol = jnp.sum(eye2 * e_in[CHUNK - 1:CHUNK, :], axis=1, keepdims=True)
            wcb.append(jnp.broadcast_to(wcol, (two, LANES)))
        yield
        t_inv, pw, a_rb, av = [], [], [], []
        for u in us:
            a_ab = jnp.where(strict, a_all[u][:CHUNK, :LANES], 0.0)
            a_ak = jnp.where(strict, a_all[u][:CHUNK, LANES:], 0.0)
            a_rk = jnp.where(incl, a_all[u][CHUNK:, LANES:], 0.0)
            a_rb.append(jnp.where(incl, a_all[u][CHUNK:, :LANES], 0.0).astype(BF16))
            av.append(_dot(jnp.concatenate([a_ak, a_rk], axis=0).astype(BF16), v_b[u]))
            t_inv.append(eye1 + a_ab)
            pw.append(a_ab)
        yield
        pw = [_dot(pw[u].astype(BF16), bdiag(pw[u])) for u in us]
        yield
        for _ in range(4):
            both = [_dot(jnp.concatenate([t_inv[u], pw[u]], axis=0).astype(BF16), bdiag(pw[u])) for u in us]
            t_inv = [t_inv[u] + both[u][:CHUNK] for u in us]
            pw = [both[u][CHUNK:] for u in us]
            yield
        t_inv = [t_inv[u] + _dot(t_inv[u].astype(BF16), bdiag(pw[u])) for u in us]
        yield
        pq = [_dot(t_inv[u].astype(BF16),
                   jnp.concatenate([bdiag(a_n[u]), bdiag(av[u][:CHUNK])], axis=1)) for u in us]
        yield
        rq = [_dot(a_rb[u], jnp.concatenate([bdiag(pq[u][:, :LANES]), bdiag(pq[u][:, LANES:])], axis=1))
              for u in us]
        mg = [_dot_tn(bk_n[u], jnp.concatenate([pq[u].astype(BF16),
                                                jnp.concatenate([zeros_b, v_b[u][:CHUNK] + v_b[u][CHUNK:]], axis=1)],
                                               axis=0)) for u in us]
        yield
        for u in us:
            m_bd = jnp.where(same_head, mg[u][:, :LANES], 0.0) * wcb[u]
            rm_sc[u] = jnp.concatenate([r_n[u] + rq[u][:, :LANES], m_bd], axis=0).astype(BF16)
            g_sc[u] = jnp.where(same_head, mg[u][:, LANES:], 0.0) * wcb[u]
            y0_sc[u] = rq[u][:, LANES:] + av[u][CHUNK:]
            wc_sc[u] = wcb[u]

    def state_step(gi, i, hs):
        u, st = inst[i]
        res = _dot(rm_sc[i], hs[st].astype(BF16))
        y_sc[chunk_rows(gi * group + u), lanes(st)] = res[:CHUNK] + y0_sc[i]
        return tuple(wc_sc[i] * hs[st] + res[CHUNK:] + g_sc[i] if t == st else hs[t] for t in range(streams))

    inv_n = 1.0 / HEAD
    grows = group * CHUNK

    def finish(gi):
        rows = pl.ds(pl.multiple_of(gi * grows, grows), grows)
        for st in range(streams):
            sl = lanes(st)
            y = y_sc[rows, sl]
            mean = segsum(y, 1) * inv_n
            yc = y - mean
            var = segsum(yc * yc, 1) * inv_n
            gn = yc * lax.rsqrt(var + GN_EPS) * lnw_ref[:, sl] + lnb_ref[:, sl]
            kh = k_ref[rows, sl] * (1.0 + (ic_ref[rows, sl] - 1.0) * ka_p[st])
            bonus = segsum(r_ref[rows, sl] * kh * rk_ref[:, sl], 1) * v_ref[rows, sl]
            g = g_ref[rows, sl]
            o_ref[rows, sl] = ((gn + bonus) * (g / (1.0 + jnp.exp(-g)))).astype(o_ref.dtype)

    def run(a_gi, b_gi, c_gi, hs):
        done = 0
        if a_gi is None:
            for i in range(ninst):
                hs = state_step(b_gi, i, hs)
        else:
            per_slot = -(-ninst // WKV_STAGE_GAPS)
            nyield = 0
            for i, _ in enumerate(phase_a(a_gi)):
                nyield += 1
                for _ in range(per_slot if (b_gi is not None and i >= 1) else 0):
                    if done < ninst:
                        hs = state_step(b_gi, done, hs)
                        done += 1
            assert nyield == WKV_STAGE_GAPS + 1 and (b_gi is None or done == ninst)
        if c_gi is not None:
            finish(c_gi)
        return hs

    run(0, None, None, None)
    hs = run(1, 0, None, tuple(jnp.zeros((two, LANES), F32) for _ in range(streams)))
    if ngroups > 2:
        hs = lax.fori_loop(1, ngroups - 1, lambda gi, hs: run(gi + 1, gi, gi - 1, hs), hs)
    run(None, ngroups - 1, ngroups - 2, hs)
    finish(ngroups - 1)


def _wkv(proj, lw, ic, kk, ka, rk, lnw, lnb, *, batch, seq):
    m, n4 = proj.shape
    w = n4 // 4
    group = WKV_GROUP
    streams = WKV_STREAMS if (w // LANES) % WKV_STREAMS == 0 else 1
    wide = streams * LANES
    nblk = w // wide
    two = 2 * CHUNK
    ninst = group * streams
    assert seq % (CHUNK * group) == 0 and seq // (CHUNK * group) >= 2

    def col_spec(p):
        return pl.BlockSpec((seq, wide), lambda b, hp: (b, p * nblk + hp))

    par_spec = pl.BlockSpec((1, wide), lambda b, hp: (0, hp))
    act_spec = pl.BlockSpec((seq, wide), lambda b, hp: (b, hp))
    return pl.pallas_call(
        functools.partial(_wkv_kernel, group=group),
        grid=(batch, nblk),
        in_specs=[col_spec(0), col_spec(1), col_spec(2), col_spec(3), act_spec, act_spec,
                  par_spec, par_spec, par_spec, par_spec, par_spec],
        out_specs=act_spec,
        out_shape=jax.ShapeDtypeStruct((m, w), BF16),
        scratch_shapes=[pltpu.VMEM((ninst, CHUNK + two, LANES), BF16),
                        pltpu.VMEM((ninst, CHUNK, LANES), F32),
                        pltpu.VMEM((ninst, two, LANES), F32),
                        pltpu.VMEM((ninst, two, LANES), F32),
                        pltpu.VMEM((seq, wide), F32)],
        compiler_params=_cparams(("parallel", "parallel")),
        name="wkv7",
    )(proj, proj, proj, proj, lw, ic, kk, ka, rk, lnw, lnb)


def _attn_kernel(q_ref, k_ref, v_ref, g_ref, lam_ref, sw_ref, o_ref,
                 z_a, z_b, p_a, p_b, acc_sc, ka1_sc, ka2_sc, q1_sc, q2_sc, *, tq, n_heads, lambda_init):
    seq = q_ref.shape[0]
    nq = seq // tq
    tk = tq
    half = BF16_EXACT_INT
    assert tk <= 2 * half
    ng = 2 * tq // LANES
    h = pl.program_id(1)
    lane = lax.broadcasted_iota(jnp.int32, (1, LANES), 1)
    slope = jnp.exp2(jnp.full((1, LANES), -8.0 / n_heads, F32) * (h + 1).astype(F32)) * LOG2E
    s_hi = slope.astype(BF16).astype(F32)
    s_lo = slope - s_hi
    s_pat = jnp.where(lane % 2 == 0, s_hi, s_lo)

    row_b = lax.broadcasted_iota(jnp.int32, (tk, LANES), 0)
    lane_b = lax.broadcasted_iota(jnp.int32, (tk, LANES), 1)
    kidx = jnp.where(lane_b % 4 < 2, row_b % half, (row_b // half) * half).astype(F32)
    aug1 = jnp.where((lane_b >= HEAD) & (lane_b < HEAD + 4), kidx, 0.0).astype(BF16)
    aug2 = jnp.where(lane_b < 4, kidx, 0.0).astype(BF16)
    lo_b = (lane_b < HEAD).astype(F32).astype(BF16)
    hi_b = (lane_b >= HEAD).astype(F32).astype(BF16)
    for j in range(seq // tk):
        ks = k_ref[j * tk:(j + 1) * tk, :]
        ka1_sc[j * tk:(j + 1) * tk, :] = ks * lo_b + aug1
        ka2_sc[j * tk:(j + 1) * tk, :] = ks * hi_b + aug2

    tri = (lax.broadcasted_iota(jnp.int32, (LANES, LANES), 0) <= lax.broadcasted_iota(jnp.int32, (LANES, LANES), 1))
    lam = lam_ref[...]
    lam_full = (jnp.exp(jnp.sum(lam[0:1] * lam[1:2], axis=1, keepdims=True))
                - jnp.exp(jnp.sum(lam[2:3] * lam[3:4], axis=1, keepdims=True)) + lambda_init)
    sw_scaled = sw_ref[...] * (1.0 - lambda_init)

    def load_q(qi):
        qf = q_ref[qi * tq:(qi + 1) * tq, :].astype(F32) * (LOG2E / math.sqrt(HEAD))
        q1_sc[...] = jnp.where(lane < HEAD, qf, jnp.where(lane < HEAD + 4, s_pat, 0.0)).astype(BF16)
        q2_sc[...] = jnp.where(lane >= HEAD, qf, jnp.where(lane < 4, s_pat, 0.0)).astype(BF16)

    def scores_to(j, z_ref, diag):
        hq = ng // 4
        for ka_sc, q_sc, g0 in ((ka1_sc, q1_sc, 0), (ka2_sc, q2_sc, ng // 2)):
            if not diag:
                z = _dot_nt(ka_sc[j * tk:(j + 1) * tk, :], q_sc[...])
                for g in range(ng // 2):
                    z_ref[g0 + g] = z[:, g * LANES:(g + 1) * LANES]
            else:
                top = _dot_nt(ka_sc[j * tk:j * tk + tk // 2, :], q_sc[...])
                bot = _dot_nt(ka_sc[j * tk + tk // 2:(j + 1) * tk, :], q_sc[tq // 2:, :])
                for g in range(ng // 2):
                    z_ref[g0 + g, 0:tk // 2, :] = top[:, g * LANES:(g + 1) * LANES]
                for g in range(hq):
                    z_ref[g0 + hq + g, tk // 2:tk, :] = bot[:, g * LANES:(g + 1) * LANES]

    def accumulate(j, p_ref, alpha, first, diag):
        hq = ng // 4
        if not diag:
            p_all = jnp.concatenate([p_ref[g] for g in range(ng)], axis=1)
            pv = _dot_tn(v_ref[j * tk:(j + 1) * tk, :], p_all)
            parts = [pv[:, g * LANES:(g + 1) * LANES] for g in range(ng)]
        else:
            late = [g for g in range(ng) if g % (ng // 2) >= hq]
            p_top = jnp.concatenate([p_ref[g, 0:tk // 2, :] for g in range(ng)], axis=1)
            p_bot = jnp.concatenate([p_ref[g, tk // 2:tk, :] for g in late], axis=1)
            pv_top = _dot_tn(v_ref[j * tk:j * tk + tk // 2, :], p_top)
            pv_bot = _dot_tn(v_ref[j * tk + tk // 2:(j + 1) * tk, :], p_bot)
            parts = [pv_top[:, g * LANES:(g + 1) * LANES] for g in range(ng)]
            for n, g in enumerate(late):
                parts[g] = parts[g] + pv_bot[:, n * LANES:(n + 1) * LANES]
        for g in range(ng):
            sl = slice(g * LANES, (g + 1) * LANES)
            acc_sc[g] = parts[g] if first else alpha[:, sl] * acc_sc[g] + parts[g]

    def softmax(qi, j, z_ref, p_ref, m, l):
        c_blk = slope * float((j - qi) * tq)
        ms, ls, alphas = [], [], []
        for g in range(ng):
            sl = slice(g * LANES, (g + 1) * LANES)
            nrows = tk
            if j == qi:
                nrows = (g % (ng // 2)) * LANES + LANES
                fill_to = tk // 2 if nrows <= tk // 2 else tk
                if nrows < fill_to:
                    p_ref[g, nrows:fill_to, :] = jnp.zeros((fill_to - nrows, LANES), BF16)
            diag_tile = jnp.where(tri, z_ref[g, nrows - LANES:nrows, :], NEG_BIG) if j == qi else None
            nplain = nrows - LANES if j == qi else nrows
            mx = diag_tile.max(axis=0, keepdims=True) if diag_tile is not None else None
            if nplain:
                mp = jnp.max(z_ref[g, 0:nplain, :], axis=0, keepdims=True)
                mx = mp if mx is None else jnp.maximum(mx, mp)
            mg = m[:, sl]
            mn = jnp.maximum(mg, mx + c_blk)
            alpha = jnp.exp2(mg - mn)
            shift = mn - c_blk
            lsum = None
            for r0 in range(0, nrows, LANES):
                zt = diag_tile if (j == qi and r0 == nplain) else z_ref[g, r0:r0 + LANES, :]
                pt = jnp.exp2(zt - shift)
                part = jnp.sum(pt, axis=0, keepdims=True)
                lsum = part if lsum is None else lsum + part
                p_ref[g, r0:r0 + LANES, :] = pt.astype(BF16)
            ls.append(alpha * l[:, sl] + lsum)
            ms.append(mn)
            alphas.append(alpha)
        return jnp.concatenate(ms, axis=1), jnp.concatenate(ls, axis=1), jnp.concatenate(alphas, axis=1)

    def finalize(qi, l):
        on = jnp.concatenate([acc_sc[g] for g in range(ng)], axis=1) / l
        ot = on[:, :tq] - lam_full * on[:, tq:]
        ot = ot * lax.rsqrt(jnp.mean(ot * ot, axis=0, keepdims=True) + SUBLN_EPS)
        g = g_ref[qi * tq:(qi + 1) * tq, :].astype(F32)
        o_ref[qi * tq:(qi + 1) * tq, :] = (ot.T * sw_scaled * (g / (1.0 + jnp.exp(-g)))).astype(o_ref.dtype)

    items = [(qi, j) for qi in range(nq) for j in range(qi + 1)]
    zs, ps = (z_a, z_b), (p_a, p_b)
    load_q(0)
    scores_to(0, z_a, True)
    m = l = alpha = None
    for n, (qi, j) in enumerate(items):
        if n > 0:
            pqi, pj = items[n - 1]
            accumulate(pj, ps[(n - 1) % 2], alpha, pj == 0, pj == pqi)
            if pqi != qi:
                finalize(pqi, l)
        if j == 0:
            m = jnp.full((1, 2 * tq), NEG_BIG, F32)
            l = jnp.zeros((1, 2 * tq), F32)
        if n + 1 < len(items):
            nqi, nj = items[n + 1]
            if nqi != qi:
                load_q(nqi)
            scores_to(nj, zs[(n + 1) % 2], nj == nqi)
        m, l, alpha = softmax(qi, j, zs[n % 2], ps[n % 2], m, l)
    pqi, pj = items[-1]
    accumulate(pj, ps[(len(items) - 1) % 2], alpha, pj == 0, pj == pqi)
    finalize(pqi, l)


def _diff_attn(qkvg, lam, sw, *, batch, seq, tq, lambda_init):
    m, n4 = qkvg.shape
    w = n4 // 4
    nh = w // LANES
    ngl = 2 * tq // LANES

    def col_spec(p):
        return pl.BlockSpec((seq, LANES), lambda b, h: (b, p * nh + h))

    return pl.pallas_call(
        functools.partial(_attn_kernel, tq=tq, n_heads=nh, lambda_init=lambda_init),
        grid=(batch, nh),
        in_specs=[col_spec(0), col_spec(1), col_spec(2), col_spec(3),
                  pl.BlockSpec(lam.shape, lambda b, h: (0, 0)),
                  pl.BlockSpec((1, LANES), lambda b, h: (0, 0))],
        out_specs=pl.BlockSpec((seq, LANES), lambda b, h: (b, h)),
        out_shape=jax.ShapeDtypeStruct((m, w), BF16),
        scratch_shapes=[pltpu.VMEM((ngl, tq, LANES), F32), pltpu.VMEM((ngl, tq, LANES), F32),
                        pltpu.VMEM((ngl, tq, LANES), BF16), pltpu.VMEM((ngl, tq, LANES), BF16),
                        pltpu.VMEM((ngl, LANES, LANES), F32),
                        pltpu.VMEM((seq, LANES), BF16), pltpu.VMEM((seq, LANES), BF16),
                        pltpu.VMEM((tq, LANES), BF16), pltpu.VMEM((tq, LANES), BF16)],
        compiler_params=_cparams(("parallel", "parallel")),
        name="diff_attn",
    )(qkvg, qkvg, qkvg, qkvg, lam, sw)


def kernel(x, norm_w, rwkv_mu, rwkv_w_in, rwkv_w0, rwkv_w1, rwkv_w2, rwkv_a0, rwkv_a1, rwkv_a2, rwkv_k_k, rwkv_k_a, rwkv_r_k, rwkv_ln_w, rwkv_ln_b, rwkv_w_out, diff_w_in, diff_lambda, diff_subln_w, diff_w_out, final_norm_w):
    batch, seq, d = x.shape
    m = batch * seq
    assert d % LANES == 0 and seq % CHUNK == 0
    assert norm_w.shape[0] == 2 and rwkv_mu.shape[0] == 1 and diff_w_in.shape[0] == 1
    tm_big = min(MM_TILE, m)
    tn_big = min(MM_TILE, d)
    tm_prep = min(ROW_TILE, seq)
    tm_last = min(ROW_TILE, m)
    tq = min(ATTN_BLOCK, seq)

    x2 = x.reshape(m, d)
    row = lambda a: a.reshape(1, -1)
    lora = rwkv_w1.shape[-1]
    pad_c = lambda a: jnp.pad(a, ((0, 0), (0, LORA_PAD - lora)))
    pad_r = lambda a: jnp.pad(a, ((0, LORA_PAD - lora), (0, 0))).astype(BF16)

    mu = rwkv_mu[0]
    mu_rkvg = jnp.stack([mu[0], mu[2], mu[3], mu[5]])
    lora_in = jnp.concatenate([pad_c(rwkv_w1[0]), pad_c(rwkv_a1[0])], axis=1)
    lora_shift = jnp.concatenate([pad_c(mu[1][:, None] * rwkv_w1[0]), pad_c(mu[4][:, None] * rwkv_a1[0])], axis=1)
    wl1 = jnp.concatenate([lora_in, lora_shift], axis=0).astype(BF16)
    xs, lw, ic = _rwkv_prep(x2, row(norm_w[0]), mu_rkvg, row(rwkv_w0[0]), wl1, pad_r(rwkv_w2[0]),
                            row(rwkv_a0[0]), pad_r(rwkv_a2[0]), batch=batch, seq=seq, tm=tm_prep)
    proj = _matmul_wcast(xs, rwkv_w_in[0], tm=tm_big, tn=tn_big, out_dtype=F32, groups=4)
    yg = _wkv(proj, lw, ic, row(rwkv_k_k[0]), row(rwkv_k_a[0]), row(rwkv_r_k[0]), row(rwkv_ln_w[0]),
              row(rwkv_ln_b[0]), batch=batch, seq=seq)
    x1, h1 = _out_proj(yg, rwkv_w_out[0].astype(BF16), x2, row(norm_w[1]), tm=tm_last, keep_sum=True,
                       norm_dtype=BF16)

    lambda_init = 0.8 - 0.6 * math.exp(-0.3 * 1)
    qkvg = _matmul_wcast(h1, diff_w_in[0], tm=min(MM_ROWS_BF16_OUT, m), tn=tn_big, out_dtype=BF16,
                         vmem_limit=MM_VMEM_LIMIT_BF16_OUT)
    att = _diff_attn(qkvg, diff_lambda[0], row(diff_subln_w[0]), batch=batch, seq=seq, tq=tq,
                     lambda_init=lambda_init)
    (out,) = _out_proj(att, diff_w_out[0].astype(BF16), x1, row(final_norm_w), tm=tm_last, keep_sum=False,
                       norm_dtype=F32)
    return out.reshape(batch, seq, d)
```

```python
import functools
import math

import jax
import jax.numpy as jnp
from jax import lax
from jax.experimental import pallas as pl
from jax.experimental.pallas import tpu as pltpu

F32 = jnp.float32
BF16 = jnp.bfloat16

NORM_EPS = 1e-6
GN_EPS = 64e-5
SUBLN_EPS = 1e-5
HEAD = 64
LANES = 128
CHUNK = 64
LORA_PAD = 128
VMEM_LIMIT = 48 * 1024 * 1024
NEG_BIG = -1e30
LOG2E = 1.0 / math.log(2.0)
BF16_EXACT_INT = 256

MM_TILE = 1024
MM_ROWS_BF16_OUT = 2048
MM_VMEM_LIMIT_BF16_OUT = 56 * 1024 * 1024
OUT_PROJ_VMEM_LIMIT = 56 * 1024 * 1024
ROW_TILE = 512
ATTN_BLOCK = 512


def _cparams(sem, vmem_limit=VMEM_LIMIT):
    return pltpu.CompilerParams(dimension_semantics=sem, vmem_limit_bytes=vmem_limit)


def _dot(a, b):
    return jnp.dot(a, b, preferred_element_type=F32)


def _dot_nt(a, b):
    return lax.dot_general(a, b, (((1,), (1,)), ((), ())), preferred_element_type=F32)


def _dot_tn(a, b):
    return lax.dot_general(a, b, (((0,), (0,)), ((), ())), preferred_element_type=F32)


def _mm_exact_rhs(a, e_bf16, parts):
    acc = None
    rem = a
    for _ in range(parts):
        piece = rem.astype(BF16)
        term = _dot(piece, e_bf16)
        acc = term if acc is None else acc + term
        rem = rem - piece.astype(F32)
    return acc


def _mm_exact_lhs(e_bf16, a, parts):
    acc = None
    rem = a
    for _ in range(parts):
        piece = rem.astype(BF16)
        term = _dot(e_bf16, piece)
        acc = term if acc is None else acc + term
        rem = rem - piece.astype(F32)
    return acc


def _rwkv_prep_kernel(x_ref, nw_ref, mu_ref, w0_ref, wl1_ref, w2_ref, a0_ref, a2_ref,
                      xs_ref, lw_ref, ic_ref, carry_ref):
    t = pl.program_id(1)
    x = x_ref[...]
    tm = x.shape[0]
    h = x * lax.rsqrt(jnp.mean(x * x, axis=-1, keepdims=True) + NORM_EPS) * nw_ref[...]

    @pl.when(t == 0)
    def _():
        carry_ref[...] = jnp.zeros_like(carry_ref)

    prev_last = carry_ref[0:1, :]
    rolled = pltpu.roll(h, 1, axis=0)
    row = lax.broadcasted_iota(jnp.int32, h.shape, 0)
    h_prev = jnp.where(row == 0, prev_last, rolled)
    carry_ref[0:1, :] = h[tm - 1:tm, :]
    d = h_prev - h
    mu = mu_ref[...]
    for p in range(4):
        xs_ref[p] = (h + d * mu[p:p + 1]).astype(BF16)
    t1 = _dot(jnp.concatenate([h, d], axis=1).astype(BF16), wl1_ref[...])
    zw = w0_ref[...] + _dot(jnp.tanh(t1[:, :LORA_PAD]).astype(BF16), w2_ref[...])
    lw_ref[...] = (-math.exp(-0.5)) / (1.0 + jnp.exp2(zw * (-LOG2E)))
    za = a0_ref[...] + _dot(t1[:, LORA_PAD:].astype(BF16), a2_ref[...])
    ic_ref[...] = 1.0 / (1.0 + jnp.exp2(za * (-LOG2E)))


def _rwkv_prep(x2, nw, mu, w0, wl1, w2p, a0, a2p, *, batch, seq, tm):
    m, d = x2.shape
    nt = seq // tm
    row_spec = pl.BlockSpec((tm, d), lambda b, t: (b * nt + t, 0))
    vec_spec = pl.BlockSpec((1, d), lambda b, t: (0, 0))

    def full(a):
        return pl.BlockSpec(a.shape, lambda b, t: (0,) * a.ndim)

    return pl.pallas_call(
        _rwkv_prep_kernel,
        grid=(batch, nt),
        in_specs=[row_spec, vec_spec, full(mu), vec_spec, full(wl1), full(w2p), vec_spec, full(a2p)],
        out_specs=[pl.BlockSpec((4, tm, d), lambda b, t: (0, b * nt + t, 0)), row_spec, row_spec],
        out_shape=[jax.ShapeDtypeStruct((4, m, d), BF16),
                   jax.ShapeDtypeStruct((m, d), F32),
                   jax.ShapeDtypeStruct((m, d), F32)],
        scratch_shapes=[pltpu.VMEM((8, d), F32)],
        compiler_params=_cparams(("parallel", "arbitrary")),
        name="rwkv_prep",
    )(x2, nw, mu, w0, wl1, w2p, a0, a2p)


def _mm_wcast_kernel(a_ref, b_ref, o_ref, w_sc):
    @pl.when(pl.program_id(1) == 0)
    def _():
        w_sc[...] = b_ref[...].astype(BF16)

    o_ref[...] = _dot(a_ref[...], w_sc[...]).astype(o_ref.dtype)


def _matmul_wcast(a, w, *, tm, tn, out_dtype, groups=1, vmem_limit=VMEM_LIMIT):
    m, k = a.shape[-2:]
    n = w.shape[1]
    per_group = (n // groups) // tn
    if groups > 1:
        a_spec = pl.BlockSpec((None, tm, k), lambda j, i: (j // per_group, i, 0))
    else:
        a_spec = pl.BlockSpec((tm, k), lambda j, i: (i, 0))
    return pl.pallas_call(
        _mm_wcast_kernel,
        grid=(n // tn, m // tm),
        in_specs=[a_spec, pl.BlockSpec((k, tn), lambda j, i: (0, j))],
        out_specs=pl.BlockSpec((tm, tn), lambda j, i: (i, j)),
        out_shape=jax.ShapeDtypeStruct((m, n), out_dtype),
        scratch_shapes=[pltpu.VMEM((k, tn), BF16)],
        compiler_params=_cparams(("parallel", "arbitrary"), vmem_limit),
        name="proj" if groups > 1 else "matmul",
    )(a, w)


def _out_proj_kernel(a_ref, b_ref, r_ref, nw_ref, *refs):
    *o_refs, w_sc = refs

    @pl.when(pl.program_id(0) == 0)
    def _():
        w_sc[...] = b_ref[...].astype(BF16)

    y = r_ref[...] + _dot(a_ref[...], w_sc[...])
    yn = y * lax.rsqrt(jnp.mean(y * y, axis=-1, keepdims=True) + NORM_EPS) * nw_ref[...]
    if len(o_refs) == 2:
        o_refs[0][...] = y
    o_refs[-1][...] = yn.astype(o_refs[-1].dtype)


def _out_proj(a, w, res, nw, *, tm, keep_sum, norm_dtype):
    m, k = a.shape
    n = w.shape[1]
    row_spec = pl.BlockSpec((tm, n), lambda i: (i, 0))
    out_specs = [row_spec, row_spec] if keep_sum else [row_spec]
    out_shape = [jax.ShapeDtypeStruct((m, n), norm_dtype)]
    if keep_sum:
        out_shape = [jax.ShapeDtypeStruct((m, n), F32)] + out_shape
    return pl.pallas_call(
        _out_proj_kernel,
        grid=(m // tm,),
        in_specs=[pl.BlockSpec((tm, k), lambda i: (i, 0)),
                  pl.BlockSpec((k, n), lambda i: (0, 0), pipeline_mode=pl.Buffered(1)),
                  row_spec,
                  pl.BlockSpec((1, n), lambda i: (0, 0))],
        out_specs=out_specs,
        out_shape=out_shape,
        scratch_shapes=[pltpu.VMEM((k, n), BF16)],
        compiler_params=_cparams(("arbitrary",), OUT_PROJ_VMEM_LIMIT),
        name="out_proj_sum_norm" if keep_sum else "out_proj_norm",
    )(a, w, res, nw)


WKV_GROUP = 8
WKV_STREAMS = 2
WKV_STAGE_GAPS = 10


def _wkv_kernel(r_ref, k_ref, v_ref, g_ref, lw_ref, ic_ref, kk_ref, ka_ref, rk_ref, lnw_ref, lnb_ref,
                o_ref, rm_sc, y0_sc, g_sc, wc_sc, y_sc, *, group):
    seq = r_ref.shape[0]
    streams = r_ref.shape[1] // LANES
    ngroups = seq // (CHUNK * group)
    two = 2 * CHUNK
    inst = [(u, st) for u in range(group) for st in range(streams)]
    ninst = len(inst)

    def lanes(st):
        return slice(st * LANES, (st + 1) * LANES)

    lane1 = lax.broadcasted_iota(jnp.int32, (1, LANES), 1)
    m0 = (lane1 < HEAD).astype(F32)
    m1 = 1.0 - m0
    row2 = lax.broadcasted_iota(jnp.int32, (two, two), 0)
    col2 = lax.broadcasted_iota(jnp.int32, (two, two), 1)
    eye2 = (row2 == col2).astype(F32)
    same_head = (row2 < HEAD) == (col2 < HEAD)
    ones_bd = same_head.astype(BF16)
    row1 = lax.broadcasted_iota(jnp.int32, (CHUNK, LANES), 0)
    col1 = lax.broadcasted_iota(jnp.int32, (CHUNK, LANES), 1) % CHUNK
    strict = row1 > col1
    incl = row1 >= col1
    eye1 = (row1 == col1).astype(F32)
    tril_c = incl[:, :CHUNK].astype(BF16)
    zeros_b = jnp.zeros((CHUNK, LANES), BF16)

    kk_p = [kk_ref[:, lanes(st)] for st in range(streams)]
    ka_p = [ka_ref[:, lanes(st)] for st in range(streams)]

    def segsum(z, parts):
        return _mm_exact_rhs(z, ones_bd, parts)

    m0b = m0.astype(BF16)
    m1b = m1.astype(BF16)

    def bdiag(z):
        zb = z.astype(BF16)
        return jnp.concatenate([zb * m0b, zb * m1b], axis=0)

    def chunk_rows(c):
        return pl.ds(pl.multiple_of(c * CHUNK, CHUNK), CHUNK)

    def phase_a(gi):
        us = range(ninst)
        rows = [chunk_rows(gi * group + u) for u, _ in inst]
        cols = [lanes(st) for _, st in inst]
        k = [k_ref[rows[i], cols[i]] for i in us]
        lw = [lw_ref[rows[i], cols[i]] for i in us]
        kkv = [k[i] * kk_p[inst[i][1]] for i in us]
        n2 = [segsum(kkv[u] * kkv[u], 2) for u in us]
        lin = [_mm_exact_lhs(tril_c, lw[u], 2) for u in us]
        yield
        a_n, r_n, v_b, bk_n, wcb, a_all = [], [], [], [], [], []
        for u in us:
            ic = ic_ref[rows[u], cols[u]]
            kkn = kkv[u] * lax.rsqrt(jnp.maximum(n2[u], 1e-24))
            kh = k[u] * (1.0 + (ic - 1.0) * ka_p[inst[u][1]])
            e_in = jnp.exp(lin[u])
            e_ex = jnp.exp(lin[u] - lw[u])
            e_neg = jnp.exp(-lin[u])
            b_f = kkn * ic * e_neg
            k_f = kh * e_neg
            a_f = -kkn * e_ex
            a_n.append(a_f)
            r_n.append(r_ref[rows[u], cols[u]] * e_in)
            v_b.append(bdiag(v_ref[rows[u], cols[u]]))
            a_all.append(_dot_nt(jnp.concatenate([a_f, r_n[u]], axis=0).astype(BF16),
                                 jnp.concatenate([bdiag(b_f), bdiag(k_f)], axis=0)))
            bk_n.append(jnp.concatenate([b_f, k_f], axis=0).astype(BF16))
            wcol = jnp.sum(eye2 * e_in[CHUNK - 1:CHUNK, :], axis=1, keepdims=True)
            wcb.append(jnp.broadcast_to(wcol, (two, LANES)))
        yield
        t_inv, pw, a_rb, av = [], [], [], []
        for u in us:
            a_ab = jnp.where(strict, a_all[u][:CHUNK, :LANES], 0.0)
            a_ak = jnp.where(strict, a_all[u][:CHUNK, LANES:], 0.0)
            a_rk = jnp.where(incl, a_all[u][CHUNK:, LANES:], 0.0)
            a_rb.append(jnp.where(incl, a_all[u][CHUNK:, :LANES], 0.0).astype(BF16))
            av.append(_dot(jnp.concatenate([a_ak, a_rk], axis=0).astype(BF16), v_b[u]))
            t_inv.append(eye1 + a_ab)
            pw.append(a_ab)
        yield
        pw = [_dot(pw[u].astype(BF16), bdiag(pw[u])) for u in us]
        yield
        for _ in range(4):
            both = [_dot(jnp.concatenate([t_inv[u], pw[u]], axis=0).astype(BF16), bdiag(pw[u])) for u in us]
            t_inv = [t_inv[u] + both[u][:CHUNK] for u in us]
            pw = [both[u][CHUNK:] for u in us]
            yield
        t_inv = [t_inv[u] + _dot(t_inv[u].astype(BF16), bdiag(pw[u])) for u in us]
        yield
        pq = [_dot(t_inv[u].astype(BF16),
                   jnp.concatenate([bdiag(a_n[u]), bdiag(av[u][:CHUNK])], axis=1)) for u in us]
        yield
        rq = [_dot(a_rb[u], jnp.concatenate([bdiag(pq[u][:, :LANES]), bdiag(pq[u][:, LANES:])], axis=1))
              for u in us]
        mg = [_dot_tn(bk_n[u], jnp.concatenate([pq[u].astype(BF16),
                                                jnp.concatenate([zeros_b, v_b[u][:CHUNK] + v_b[u][CHUNK:]], axis=1)],
                                               axis=0)) for u in us]
        yield
        for u in us:
            m_bd = jnp.where(same_head, mg[u][:, :LANES], 0.0) * wcb[u]
            rm_sc[u] = jnp.concatenate([r_n[u] + rq[u][:, :LANES], m_bd], axis=0).astype(BF16)
            g_sc[u] = jnp.where(same_head, mg[u][:, LANES:], 0.0) * wcb[u]
            y0_sc[u] = rq[u][:, LANES:] + av[u][CHUNK:]
            wc_sc[u] = wcb[u]

    def state_step(gi, i, hs):
        u, st = inst[i]
        res = _dot(rm_sc[i], hs[st].astype(BF16))
        y_sc[chunk_rows(gi * group + u), lanes(st)] = res[:CHUNK] + y0_sc[i]
        return tuple(wc_sc[i] * hs[st] + res[CHUNK:] + g_sc[i] if t == st else hs[t] for t in range(streams))

    inv_n = 1.0 / HEAD
    grows = group * CHUNK

    def finish(gi):
        rows = pl.ds(pl.multiple_of(gi * grows, grows), grows)
        for st in range(streams):
            sl = lanes(st)
            y = y_sc[rows, sl]
            mean = segsum(y, 1) * inv_n
            yc = y - mean
            var = segsum(yc * yc, 1) * inv_n
            gn = yc * lax.rsqrt(var + GN_EPS) * lnw_ref[:, sl] + lnb_ref[:, sl]
            kh = k_ref[rows, sl] * (1.0 + (ic_ref[rows, sl] - 1.0) * ka_p[st])
            bonus = segsum(r_ref[rows, sl] * kh * rk_ref[:, sl], 1) * v_ref[rows, sl]
            g = g_ref[rows, sl]
            o_ref[rows, sl] = ((gn + bonus) * (g / (1.0 + jnp.exp(-g)))).astype(o_ref.dtype)

    def run(a_gi, b_gi, c_gi, hs):
        done = 0
        if a_gi is None:
            for i in range(ninst):
                hs = state_step(b_gi, i, hs)
        else:
            per_slot = -(-ninst // WKV_STAGE_GAPS)
            nyield = 0
            for i, _ in enumerate(phase_a(a_gi)):
                nyield += 1
                for _ in range(per_slot if (b_gi is not None and i >= 1) else 0):
                    if done < ninst:
                        hs = state_step(b_gi, done, hs)
                        done += 1
            assert nyield == WKV_STAGE_GAPS + 1 and (b_gi is None or done == ninst)
        if c_gi is not None:
            finish(c_gi)
        return hs

    run(0, None, None, None)
    hs = run(1, 0, None, tuple(jnp.zeros((two, LANES), F32) for _ in range(streams)))
    if ngroups > 2:
        hs = lax.fori_loop(1, ngroups - 1, lambda gi, hs: run(gi + 1, gi, gi - 1, hs), hs)
    run(None, ngroups - 1, ngroups - 2, hs)
    finish(ngroups - 1)


def _wkv(proj, lw, ic, kk, ka, rk, lnw, lnb, *, batch, seq):
    m, n4 = proj.shape
    w = n4 // 4
    group = WKV_GROUP
    streams = WKV_STREAMS if (w // LANES) % WKV_STREAMS == 0 else 1
    wide = streams * LANES
    nblk = w // wide
    two = 2 * CHUNK
    ninst = group * streams
    assert seq % (CHUNK * group) == 0 and seq // (CHUNK * group) >= 2

    def col_spec(p):
        return pl.BlockSpec((seq, wide), lambda b, hp: (b, p * nblk + hp))

    par_spec = pl.BlockSpec((1, wide), lambda b, hp: (0, hp))
    act_spec = pl.BlockSpec((seq, wide), lambda b, hp: (b, hp))
    return pl.pallas_call(
        functools.partial(_wkv_kernel, group=group),
        grid=(batch, nblk),
        in_specs=[col_spec(0), col_spec(1), col_spec(2), col_spec(3), act_spec, act_spec,
                  par_spec, par_spec, par_spec, par_spec, par_spec],
        out_specs=act_spec,
        out_shape=jax.ShapeDtypeStruct((m, w), BF16),
        scratch_shapes=[pltpu.VMEM((ninst, CHUNK + two, LANES), BF16),
                        pltpu.VMEM((ninst, CHUNK, LANES), F32),
                        pltpu.VMEM((ninst, two, LANES), F32),
                        pltpu.VMEM((ninst, two, LANES), F32),
                        pltpu.VMEM((seq, wide), F32)],
        compiler_params=_cparams(("parallel", "parallel")),
        name="wkv7",
    )(proj, proj, proj, proj, lw, ic, kk, ka, rk, lnw, lnb)


def _attn_kernel(q_ref, k_ref, v_ref, g_ref, lam_ref, sw_ref, o_ref,
                 z_a, z_b, p_a, p_b, acc_sc, ka1_sc, ka2_sc, q1_sc, q2_sc, *, tq, n_heads, lambda_init):
    seq = q_ref.shape[0]
    nq = seq // tq
    tk = tq
    half = BF16_EXACT_INT
    assert tk <= 2 * half
    ng = 2 * tq // LANES
    h = pl.program_id(1)
    lane = lax.broadcasted_iota(jnp.int32, (1, LANES), 1)
    slope = jnp.exp2(jnp.full((1, LANES), -8.0 / n_heads, F32) * (h + 1).astype(F32)) * LOG2E
    s_hi = slope.astype(BF16).astype(F32)
    s_lo = slope - s_hi
    s_pat = jnp.where(lane % 2 == 0, s_hi, s_lo)

    row_b = lax.broadcasted_iota(jnp.int32, (tk, LANES), 0)
    lane_b = lax.broadcasted_iota(jnp.int32, (tk, LANES), 1)
    kidx = jnp.where(lane_b % 4 < 2, row_b % half, (row_b // half) * half).astype(F32)
    aug1 = jnp.where((lane_b >= HEAD) & (lane_b < HEAD + 4), kidx, 0.0).astype(BF16)
    aug2 = jnp.where(lane_b < 4, kidx, 0.0).astype(BF16)
    lo_b = (lane_b < HEAD).astype(F32).astype(BF16)
    hi_b = (lane_b >= HEAD).astype(F32).astype(BF16)
    for j in range(seq // tk):
        ks = k_ref[j * tk:(j + 1) * tk, :]
        ka1_sc[j * tk:(j + 1) * tk, :] = ks * lo_b + aug1
        ka2_sc[j * tk:(j + 1) * tk, :] = ks * hi_b + aug2

    tri = (lax.broadcasted_iota(jnp.int32, (LANES, LANES), 0) <= lax.broadcasted_iota(jnp.int32, (LANES, LANES), 1))
    lam = lam_ref[...]
    lam_full = (jnp.exp(jnp.sum(lam[0:1] * lam[1:2], axis=1, keepdims=True))
                - jnp.exp(jnp.sum(lam[2:3] * lam[3:4], axis=1, keepdims=True)) + lambda_init)
    sw_scaled = sw_ref[...] * (1.0 - lambda_init)

    def load_q(qi):
        qf = q_ref[qi * tq:(qi + 1) * tq, :].astype(F32) * (LOG2E / math.sqrt(HEAD))
        q1_sc[...] = jnp.where(lane < HEAD, qf, jnp.where(lane < HEAD + 4, s_pat, 0.0)).astype(BF16)
        q2_sc[...] = jnp.where(lane >= HEAD, qf, jnp.where(lane < 4, s_pat, 0.0)).astype(BF16)

    def scores_to(j, z_ref, diag):
        hq = ng // 4
        for ka_sc, q_sc, g0 in ((ka1_sc, q1_sc, 0), (ka2_sc, q2_sc, ng // 2)):
            if not diag:
                z = _dot_nt(ka_sc[j * tk:(j + 1) * tk, :], q_sc[...])
                for g in range(ng // 2):
                    z_ref[g0 + g] = z[:, g * LANES:(g + 1) * LANES]
            else:
                top = _dot_nt(ka_sc[j * tk:j * tk + tk // 2, :], q_sc[...])
                bot = _dot_nt(ka_sc[j * tk + tk // 2:(j + 1) * tk, :], q_sc[tq // 2:, :])
                for g in range(ng // 2):
                    z_ref[g0 + g, 0:tk // 2, :] = top[:, g * LANES:(g + 1) * LANES]
                for g in range(hq):
                    z_ref[g0 + hq + g, tk // 2:tk, :] = bot[:, g * LANES:(g + 1) * LANES]

    def accumulate(j, p_ref, alpha, first, diag):
        hq = ng // 4
        if not diag:
            p_all = jnp.concatenate([p_ref[g] for g in range(ng)], axis=1)
            pv = _dot_tn(v_ref[j * tk:(j + 1) * tk, :], p_all)
            parts = [pv[:, g * LANES:(g + 1) * LANES] for g in range(ng)]
        else:
            late = [g for g in range(ng) if g % (ng // 2) >= hq]
            p_top = jnp.concatenate([p_ref[g, 0:tk // 2, :] for g in range(ng)], axis=1)
            p_bot = jnp.concatenate([p_ref[g, tk // 2:tk, :] for g in late], axis=1)
            pv_top = _dot_tn(v_ref[j * tk:j * tk + tk // 2, :], p_top)
            pv_bot = _dot_tn(v_ref[j * tk + tk // 2:(j + 1) * tk, :], p_bot)
            parts = [pv_top[:, g * LANES:(g + 1) * LANES] for g in range(ng)]
            for n, g in enumerate(late):
                parts[g] = parts[g] + pv_bot[:, n * LANES:(n + 1) * LANES]
        for g in range(ng):
            sl = slice(g * LANES, (g + 1) * LANES)
            acc_sc[g] = parts[g] if first else alpha[:, sl] * acc_sc[g] + parts[g]

    def softmax(qi, j, z_ref, p_ref, m, l):
        c_blk = slope * float((j - qi) * tq)
        ms, ls, alphas = [], [], []
        for g in range(ng):
            sl = slice(g * LANES, (g + 1) * LANES)
            nrows = tk
            if j == qi:
                nrows = (g % (ng // 2)) * LANES + LANES
                fill_to = tk // 2 if nrows <= tk // 2 else tk
                if nrows < fill_to:
                    p_ref[g, nrows:fill_to, :] = jnp.zeros((fill_to - nrows, LANES), BF16)
            diag_tile = jnp.where(tri, z_ref[g, nrows - LANES:nrows, :], NEG_BIG) if j == qi else None
            nplain = nrows - LANES if j == qi else nrows
            mx = diag_tile.max(axis=0, keepdims=True) if diag_tile is not None else None
            if nplain:
                mp = jnp.max(z_ref[g, 0:nplain, :], axis=0, keepdims=True)
                mx = mp if mx is None else jnp.maximum(mx, mp)
            mg = m[:, sl]
            mn = jnp.maximum(mg, mx + c_blk)
            alpha = jnp.exp2(mg - mn)
            shift = mn - c_blk
            lsum = None
            for r0 in range(0, nrows, LANES):
                zt = diag_tile if (j == qi and r0 == nplain) else z_ref[g, r0:r0 + LANES, :]
                pt = jnp.exp2(zt - shift)
                part = jnp.sum(pt, axis=0, keepdims=True)
                lsum = part if lsum is None else lsum + part
                p_ref[g, r0:r0 + LANES, :] = pt.astype(BF16)
            ls.append(alpha * l[:, sl] + lsum)
            ms.append(mn)
            alphas.append(alpha)
        return jnp.concatenate(ms, axis=1), jnp.concatenate(ls, axis=1), jnp.concatenate(alphas, axis=1)

    def finalize(qi, l):
        on = jnp.concatenate([acc_sc[g] for g in range(ng)], axis=1) / l
        ot = on[:, :tq] - lam_full * on[:, tq:]
        ot = ot * lax.rsqrt(jnp.mean(ot * ot, axis=0, keepdims=True) + SUBLN_EPS)
        g = g_ref[qi * tq:(qi + 1) * tq, :].astype(F32)
        o_ref[qi * tq:(qi + 1) * tq, :] = (ot.T * sw_scaled * (g / (1.0 + jnp.exp(-g)))).astype(o_ref.dtype)

    items = [(qi, j) for qi in range(nq) for j in range(qi + 1)]
    zs, ps = (z_a, z_b), (p_a, p_b)
    load_q(0)
    scores_to(0, z_a, True)
    m = l = alpha = None
    for n, (qi, j) in enumerate(items):
        if n > 0:
            pqi, pj = items[n - 1]
            accumulate(pj, ps[(n - 1) % 2], alpha, pj == 0, pj == pqi)
            if pqi != qi:
                finalize(pqi, l)
        if j == 0:
            m = jnp.full((1, 2 * tq), NEG_BIG, F32)
            l = jnp.zeros((1, 2 * tq), F32)
        if n + 1 < len(items):
            nqi, nj = items[n + 1]
            if nqi != qi:
                load_q(nqi)
            scores_to(nj, zs[(n + 1) % 2], nj == nqi)
        m, l, alpha = softmax(qi, j, zs[n % 2], ps[n % 2], m, l)
    pqi, pj = items[-1]
    accumulate(pj, ps[(len(items) - 1) % 2], alpha, pj == 0, pj == pqi)
    finalize(pqi, l)


def _diff_attn(qkvg, lam, sw, *, batch, seq, tq, lambda_init):
    m, n4 = qkvg.shape
    w = n4 // 4
    nh = w // LANES
    ngl = 2 * tq // LANES

    def col_spec(p):
        return pl.BlockSpec((seq, LANES), lambda b, h: (b, p * nh + h))

    return pl.pallas_call(
        functools.partial(_attn_kernel, tq=tq, n_heads=nh, lambda_init=lambda_init),
        grid=(batch, nh),
        in_specs=[col_spec(0), col_spec(1), col_spec(2), col_spec(3),
                  pl.BlockSpec(lam.shape, lambda b, h: (0, 0)),
                  pl.BlockSpec((1, LANES), lambda b, h: (0, 0))],
        out_specs=pl.BlockSpec((seq, LANES), lambda b, h: (b, h)),
        out_shape=jax.ShapeDtypeStruct((m, w), BF16),
        scratch_shapes=[pltpu.VMEM((ngl, tq, LANES), F32), pltpu.VMEM((ngl, tq, LANES), F32),
                        pltpu.VMEM((ngl, tq, LANES), BF16), pltpu.VMEM((ngl, tq, LANES), BF16),
                        pltpu.VMEM((ngl, LANES, LANES), F32),
                        pltpu.VMEM((seq, LANES), BF16), pltpu.VMEM((seq, LANES), BF16),
                        pltpu.VMEM((tq, LANES), BF16), pltpu.VMEM((tq, LANES), BF16)],
        compiler_params=_cparams(("parallel", "parallel")),
        name="diff_attn",
    )(qkvg, qkvg, qkvg, qkvg, lam, sw)


def kernel(x, norm_w, rwkv_mu, rwkv_w_in, rwkv_w0, rwkv_w1, rwkv_w2, rwkv_a0, rwkv_a1, rwkv_a2, rwkv_k_k, rwkv_k_a, rwkv_r_k, rwkv_ln_w, rwkv_ln_b, rwkv_w_out, diff_w_in, diff_lambda, diff_subln_w, diff_w_out, final_norm_w):
    batch, seq, d = x.shape
    m = batch * seq
    assert d % LANES == 0 and seq % CHUNK == 0
    assert norm_w.shape[0] == 2 and rwkv_mu.shape[0] == 1 and diff_w_in.shape[0] == 1
    tm_big = min(MM_TILE, m)
    tn_big = min(MM_TILE, d)
    tm_prep = min(ROW_TILE, seq)
    tm_last = min(ROW_TILE, m)
    tq = min(ATTN_BLOCK, seq)

    x2 = x.reshape(m, d)
    row = lambda a: a.reshape(1, -1)
    lora = rwkv_w1.shape[-1]
    pad_c = lambda a: jnp.pad(a, ((0, 0), (0, LORA_PAD - lora)))
    pad_r = lambda a: jnp.pad(a, ((0, LORA_PAD - lora), (0, 0))).astype(BF16)

    mu = rwkv_mu[0]
    mu_rkvg = jnp.stack([mu[0], mu[2], mu[3], mu[5]])
    lora_in = jnp.concatenate([pad_c(rwkv_w1[0]), pad_c(rwkv_a1[0])], axis=1)
    lora_shift = jnp.concatenate([pad_c(mu[1][:, None] * rwkv_w1[0]), pad_c(mu[4][:, None] * rwkv_a1[0])], axis=1)
    wl1 = jnp.concatenate([lora_in, lora_shift], axis=0).astype(BF16)
    xs, lw, ic = _rwkv_prep(x2, row(norm_w[0]), mu_rkvg, row(rwkv_w0[0]), wl1, pad_r(rwkv_w2[0]),
                            row(rwkv_a0[0]), pad_r(rwkv_a2[0]), batch=batch, seq=seq, tm=tm_prep)
    proj = _matmul_wcast(xs, rwkv_w_in[0], tm=tm_big, tn=tn_big, out_dtype=F32, groups=4)
    yg = _wkv(proj, lw, ic, row(rwkv_k_k[0]), row(rwkv_k_a[0]), row(rwkv_r_k[0]), row(rwkv_ln_w[0]),
              row(rwkv_ln_b[0]), batch=batch, seq=seq)
    x1, h1 = _out_proj(yg, rwkv_w_out[0], x2, row(norm_w[1]), tm=tm_last, keep_sum=True,
                       norm_dtype=BF16)

    lambda_init = 0.8 - 0.6 * math.exp(-0.3 * 1)
    qkvg = _matmul_wcast(h1, diff_w_in[0], tm=min(MM_ROWS_BF16_OUT, m), tn=tn_big, out_dtype=BF16,
                         vmem_limit=MM_VMEM_LIMIT_BF16_OUT)
    att = _diff_attn(qkvg, diff_lambda[0], row(diff_subln_w[0]), batch=batch, seq=seq, tq=tq,
                     lambda_init=lambda_init)
    (out,) = _out_proj(att, diff_w_out[0], x1, row(final_norm_w), tm=tm_last, keep_sum=False,
                       norm_dtype=F32)
    return out.reshape(batch, seq, d)
```

```python
import functools
import math

import jax
import jax.numpy as jnp
from jax import lax
from jax.experimental import pallas as pl
from jax.experimental.pallas import tpu as pltpu

F32 = jnp.float32
BF16 = jnp.bfloat16

NORM_EPS = 1e-6
GN_EPS = 64e-5
SUBLN_EPS = 1e-5
HEAD = 64
LANES = 128
CHUNK = 64
LORA_PAD = 128
VMEM_LIMIT = 48 * 1024 * 1024
NEG_BIG = -1e30
LOG2E = 1.0 / math.log(2.0)
BF16_EXACT_INT = 256

MM_TILE = 1024
MM_ROWS_BF16_OUT = 2048
MM_VMEM_LIMIT_BF16_OUT = 56 * 1024 * 1024
OUT_PROJ_VMEM_LIMIT = 56 * 1024 * 1024
ROW_TILE = 512
ATTN_BLOCK = 512


def _cparams(sem, vmem_limit=VMEM_LIMIT):
    return pltpu.CompilerParams(dimension_semantics=sem, vmem_limit_bytes=vmem_limit)


def _dot(a, b):
    return jnp.dot(a, b, preferred_element_type=F32)


def _dot_nt(a, b):
    return lax.dot_general(a, b, (((1,), (1,)), ((), ())), preferred_element_type=F32)


def _dot_tn(a, b):
    return lax.dot_general(a, b, (((0,), (0,)), ((), ())), preferred_element_type=F32)


def _mm_exact_rhs(a, e_bf16, parts):
    acc = None
    rem = a
    for _ in range(parts):
        piece = rem.astype(BF16)
        term = _dot(piece, e_bf16)
        acc = term if acc is None else acc + term
        rem = rem - piece.astype(F32)
    return acc


def _mm_exact_lhs(e_bf16, a, parts):
    acc = None
    rem = a
    for _ in range(parts):
        piece = rem.astype(BF16)
        term = _dot(e_bf16, piece)
        acc = term if acc is None else acc + term
        rem = rem - piece.astype(F32)
    return acc


def _rwkv_prep_kernel(x_ref, nw_ref, mu_ref, w0_ref, wl1_ref, w2_ref, a0_ref, a2_ref,
                      xs_ref, lw_ref, ic_ref, carry_ref):
    t = pl.program_id(1)
    x = x_ref[...]
    tm = x.shape[0]
    h = x * lax.rsqrt(jnp.mean(x * x, axis=-1, keepdims=True) + NORM_EPS) * nw_ref[...]

    @pl.when(t == 0)
    def _():
        carry_ref[...] = jnp.zeros_like(carry_ref)

    prev_last = carry_ref[0:1, :]
    rolled = pltpu.roll(h, 1, axis=0)
    row = lax.broadcasted_iota(jnp.int32, h.shape, 0)
    h_prev = jnp.where(row == 0, prev_last, rolled)
    carry_ref[0:1, :] = h[tm - 1:tm, :]
    d = h_prev - h
    mu = mu_ref[...]
    for p in range(4):
        xs_ref[p] = (h + d * mu[p:p + 1]).astype(BF16)
    t1 = _dot(jnp.concatenate([h, d], axis=1).astype(BF16), wl1_ref[...])
    zw = w0_ref[...] + _dot(jnp.tanh(t1[:, :LORA_PAD]).astype(BF16), w2_ref[...])
    lw_ref[...] = (-math.exp(-0.5)) / (1.0 + jnp.exp2(zw * (-LOG2E)))
    za = a0_ref[...] + _dot(t1[:, LORA_PAD:].astype(BF16), a2_ref[...])
    ic_ref[...] = 1.0 / (1.0 + jnp.exp2(za * (-LOG2E)))


def _rwkv_prep(x2, nw, mu, w0, wl1, w2p, a0, a2p, *, batch, seq, tm):
    m, d = x2.shape
    nt = seq // tm
    row_spec = pl.BlockSpec((tm, d), lambda b, t: (b * nt + t, 0))
    vec_spec = pl.BlockSpec((1, d), lambda b, t: (0, 0))

    def full(a):
        return pl.BlockSpec(a.shape, lambda b, t: (0,) * a.ndim)

    return pl.pallas_call(
        _rwkv_prep_kernel,
        grid=(batch, nt),
        in_specs=[row_spec, vec_spec, full(mu), vec_spec, full(wl1), full(w2p), vec_spec, full(a2p)],
        out_specs=[pl.BlockSpec((4, tm, d), lambda b, t: (0, b * nt + t, 0)), row_spec, row_spec],
        out_shape=[jax.ShapeDtypeStruct((4, m, d), BF16),
                   jax.ShapeDtypeStruct((m, d), F32),
                   jax.ShapeDtypeStruct((m, d), F32)],
        scratch_shapes=[pltpu.VMEM((8, d), F32)],
        compiler_params=_cparams(("parallel", "arbitrary")),
        name="rwkv_prep",
    )(x2, nw, mu, w0, wl1, w2p, a0, a2p)


def _mm_wcast_kernel(a_ref, b_ref, o_ref, w_sc):
    @pl.when(pl.program_id(1) == 0)
    def _():
        w_sc[...] = b_ref[...].astype(BF16)

    o_ref[...] = _dot(a_ref[...], w_sc[...]).astype(o_ref.dtype)


def _matmul_wcast(a, w, *, tm, tn, out_dtype, groups=1, vmem_limit=VMEM_LIMIT):
    m, k = a.shape[-2:]
    n = w.shape[1]
    per_group = (n // groups) // tn
    if groups > 1:
        a_spec = pl.BlockSpec((None, tm, k), lambda j, i: (j // per_group, i, 0))
    else:
        a_spec = pl.BlockSpec((tm, k), lambda j, i: (i, 0))
    return pl.pallas_call(
        _mm_wcast_kernel,
        grid=(n // tn, m // tm),
        in_specs=[a_spec, pl.BlockSpec((k, tn), lambda j, i: (0, j))],
        out_specs=pl.BlockSpec((tm, tn), lambda j, i: (i, j)),
        out_shape=jax.ShapeDtypeStruct((m, n), out_dtype),
        scratch_shapes=[pltpu.VMEM((k, tn), BF16)],
        compiler_params=_cparams(("parallel", "arbitrary"), vmem_limit),
        name="proj" if groups > 1 else "matmul",
    )(a, w)


def _out_proj_kernel(a_ref, b_ref, r_ref, nw_ref, *refs):
    *o_refs, w_sc = refs

    @pl.when(pl.program_id(0) == 0)
    def _():
        w_sc[...] = b_ref[...].astype(BF16)

    y = r_ref[...] + _dot(a_ref[...], w_sc[...])
    yn = y * lax.rsqrt(jnp.mean(y * y, axis=-1, keepdims=True) + NORM_EPS) * nw_ref[...]
    if len(o_refs) == 2:
        o_refs[0][...] = y
    o_refs[-1][...] = yn.astype(o_refs[-1].dtype)


def _out_proj(a, w, res, nw, *, tm, keep_sum, norm_dtype):
    m, k = a.shape
    n = w.shape[1]
    row_spec = pl.BlockSpec((tm, n), lambda i: (i, 0))
    out_specs = [row_spec, row_spec] if keep_sum else [row_spec]
    out_shape = [jax.ShapeDtypeStruct((m, n), norm_dtype)]
    if keep_sum:
        out_shape = [jax.ShapeDtypeStruct((m, n), F32)] + out_shape
    return pl.pallas_call(
        _out_proj_kernel,
        grid=(m // tm,),
        in_specs=[pl.BlockSpec((tm, k), lambda i: (i, 0)),
                  pl.BlockSpec((k, n), lambda i: (0, 0), pipeline_mode=pl.Buffered(1)),
                  row_spec,
                  pl.BlockSpec((1, n), lambda i: (0, 0))],
        out_specs=out_specs,
        out_shape=out_shape,
        scratch_shapes=[pltpu.VMEM((k, n), BF16)],
        compiler_params=_cparams(("arbitrary",), OUT_PROJ_VMEM_LIMIT),
        name="out_proj_sum_norm" if keep_sum else "out_proj_norm",
    )(a, w, res, nw)


WKV_GROUP = 8
WKV_STREAMS = 2
WKV_STAGE_GAPS = 10


def _wkv_kernel(r_ref, k_ref, v_ref, g_ref, lw_ref, ic_ref, kk_ref, ka_ref, rk_ref, lnw_ref, lnb_ref,
                o_ref, rm_sc, y0_sc, g_sc, wc_sc, y_sc, *, group):
    seq = r_ref.shape[0]
    streams = r_ref.shape[1] // LANES
    ngroups = seq // (CHUNK * group)
    two = 2 * CHUNK
    inst = [(u, st) for u in range(group) for st in range(streams)]
    ninst = len(inst)

    def lanes(st):
        return slice(st * LANES, (st + 1) * LANES)

    lane1 = lax.broadcasted_iota(jnp.int32, (1, LANES), 1)
    m0 = (lane1 < HEAD).astype(F32)
    m1 = 1.0 - m0
    row2 = lax.broadcasted_iota(jnp.int32, (two, two), 0)
    col2 = lax.broadcasted_iota(jnp.int32, (two, two), 1)
    eye2 = (row2 == col2).astype(F32)
    same_head = (row2 < HEAD) == (col2 < HEAD)
    ones_bd = same_head.astype(BF16)
    row1 = lax.broadcasted_iota(jnp.int32, (CHUNK, LANES), 0)
    col1 = lax.broadcasted_iota(jnp.int32, (CHUNK, LANES), 1) % CHUNK
    strict = row1 > col1
    incl = row1 >= col1
    eye1 = (row1 == col1).astype(F32)
    tril_c = incl[:, :CHUNK].astype(BF16)
    zeros_b = jnp.zeros((CHUNK, LANES), BF16)

    kk_p = [kk_ref[:, lanes(st)] for st in range(streams)]
    ka_p = [ka_ref[:, lanes(st)] for st in range(streams)]

    def segsum(z, parts):
        return _mm_exact_rhs(z, ones_bd, parts)

    m0b = m0.astype(BF16)
    m1b = m1.astype(BF16)

    def bdiag(z):
        zb = z.astype(BF16)
        return jnp.concatenate([zb * m0b, zb * m1b], axis=0)

    def chunk_rows(c):
        return pl.ds(pl.multiple_of(c * CHUNK, CHUNK), CHUNK)

    def phase_a(gi):
        us = range(ninst)
        rows = [chunk_rows(gi * group + u) for u, _ in inst]
        cols = [lanes(st) for _, st in inst]
        k = [k_ref[rows[i], cols[i]] for i in us]
        lw = [lw_ref[rows[i], cols[i]] for i in us]
        kkv = [k[i] * kk_p[inst[i][1]] for i in us]
        n2 = [segsum(kkv[u] * kkv[u], 1) for u in us]
        lin = [_mm_exact_lhs(tril_c, lw[u], 2) for u in us]
        yield
        a_n, r_n, v_b, bk_n, wcb, a_all = [], [], [], [], [], []
        for u in us:
            ic = ic_ref[rows[u], cols[u]]
            kkn = kkv[u] * lax.rsqrt(jnp.maximum(n2[u], 1e-24))
            kh = k[u] * (1.0 + (ic - 1.0) * ka_p[inst[u][1]])
            e_in = jnp.exp(lin[u])
            e_ex = jnp.where(row1 == 0, 1.0, pltpu.roll(e_in, 1, axis=0))
            e_neg = jnp.exp(-lin[u])
            b_f = kkn * ic * e_neg
            k_f = kh * e_neg
            a_f = -kkn * e_ex
            a_n.append(a_f)
            r_n.append(r_ref[rows[u], cols[u]] * e_in)
            v_b.append(bdiag(v_ref[rows[u], cols[u]]))
            a_all.append(_dot_nt(jnp.concatenate([a_f, r_n[u]], axis=0).astype(BF16),
                                 jnp.concatenate([bdiag(b_f), bdiag(k_f)], axis=0)))
            bk_n.append(jnp.concatenate([b_f, k_f], axis=0).astype(BF16))
            wcol = jnp.sum(eye2 * e_in[CHUNK - 1:CHUNK, :], axis=1, keepdims=True)
            wcb.append(jnp.broadcast_to(wcol, (two, LANES)))
        yield
        t_inv, pw, a_rb, av = [], [], [], []
        for u in us:
            a_ab = jnp.where(strict, a_all[u][:CHUNK, :LANES], 0.0)
            a_ak = jnp.where(strict, a_all[u][:CHUNK, LANES:], 0.0)
            a_rk = jnp.where(incl, a_all[u][CHUNK:, LANES:], 0.0)
            a_rb.append(jnp.where(incl, a_all[u][CHUNK:, :LANES], 0.0).astype(BF16))
            av.append(_dot(jnp.concatenate([a_ak, a_rk], axis=0).astype(BF16), v_b[u]))
            t_inv.append(eye1 + a_ab)
            pw.append(a_ab)
        yield
        pw = [_dot(pw[u].astype(BF16), bdiag(pw[u])) for u in us]
        yield
        for _ in range(4):
            both = [_dot(jnp.concatenate([t_inv[u], pw[u]], axis=0).astype(BF16), bdiag(pw[u])) for u in us]
            t_inv = [t_inv[u] + both[u][:CHUNK] for u in us]
            pw = [both[u][CHUNK:] for u in us]
            yield
        t_inv = [t_inv[u] + _dot(t_inv[u].astype(BF16), bdiag(pw[u])) for u in us]
        yield
        pq = [_dot(t_inv[u].astype(BF16),
                   jnp.concatenate([bdiag(a_n[u]), bdiag(av[u][:CHUNK])], axis=1)) for u in us]
        yield
        rq = [_dot(a_rb[u], jnp.concatenate([bdiag(pq[u][:, :LANES]), bdiag(pq[u][:, LANES:])], axis=1))
              for u in us]
        mg = [_dot_tn(bk_n[u], jnp.concatenate([pq[u].astype(BF16),
                                                jnp.concatenate([zeros_b, v_b[u][:CHUNK] + v_b[u][CHUNK:]], axis=1)],
                                               axis=0)) for u in us]
        yield
        for u in us:
            m_bd = jnp.where(same_head, mg[u][:, :LANES], 0.0) * wcb[u]
            rm_sc[u] = jnp.concatenate([r_n[u] + rq[u][:, :LANES], m_bd], axis=0).astype(BF16)
            g_sc[u] = jnp.where(same_head, mg[u][:, LANES:], 0.0) * wcb[u]
            y0_sc[u] = rq[u][:, LANES:] + av[u][CHUNK:]
            wc_sc[u] = wcb[u]

    def state_step(gi, i, hs):
        u, st = inst[i]
        res = _dot(rm_sc[i], hs[st].astype(BF16))
        y_sc[chunk_rows(gi * group + u), lanes(st)] = res[:CHUNK] + y0_sc[i]
        return tuple(wc_sc[i] * hs[st] + res[CHUNK:] + g_sc[i] if t == st else hs[t] for t in range(streams))

    inv_n = 1.0 / HEAD
    grows = group * CHUNK

    def finish(gi):
        rows = pl.ds(pl.multiple_of(gi * grows, grows), grows)
        for st in range(streams):
            sl = lanes(st)
            y = y_sc[rows, sl]
            mean = segsum(y, 1) * inv_n
            yc = y - mean
            var = segsum(yc * yc, 1) * inv_n
            gn = yc * lax.rsqrt(var + GN_EPS) * lnw_ref[:, sl] + lnb_ref[:, sl]
            kh = k_ref[rows, sl] * (1.0 + (ic_ref[rows, sl] - 1.0) * ka_p[st])
            bonus = segsum(r_ref[rows, sl] * kh * rk_ref[:, sl], 1) * v_ref[rows, sl]
            g = g_ref[rows, sl]
            o_ref[rows, sl] = ((gn + bonus) * (g / (1.0 + jnp.exp2(g * (-LOG2E))))).astype(o_ref.dtype)

    def run(a_gi, b_gi, c_gi, hs):
        done = 0
        if a_gi is None:
            for i in range(ninst):
                hs = state_step(b_gi, i, hs)
        else:
            per_slot = -(-ninst // WKV_STAGE_GAPS)
            nyield = 0
            for i, _ in enumerate(phase_a(a_gi)):
                nyield += 1
                for _ in range(per_slot if (b_gi is not None and i >= 1) else 0):
                    if done < ninst:
                        hs = state_step(b_gi, done, hs)
                        done += 1
            assert nyield == WKV_STAGE_GAPS + 1 and (b_gi is None or done == ninst)
        if c_gi is not None:
            finish(c_gi)
        return hs

    run(0, None, None, None)
    hs = run(1, 0, None, tuple(jnp.zeros((two, LANES), F32) for _ in range(streams)))
    if ngroups > 2:
        hs = lax.fori_loop(1, ngroups - 1, lambda gi, hs: run(gi + 1, gi, gi - 1, hs), hs)
    run(None, ngroups - 1, ngroups - 2, hs)
    finish(ngroups - 1)


def _wkv(proj, lw, ic, kk, ka, rk, lnw, lnb, *, batch, seq):
    m, n4 = proj.shape
    w = n4 // 4
    group = WKV_GROUP
    streams = WKV_STREAMS if (w // LANES) % WKV_STREAMS == 0 else 1
    wide = streams * LANES
    nblk = w // wide
    two = 2 * CHUNK
    ninst = group * streams
    assert seq % (CHUNK * group) == 0 and seq // (CHUNK * group) >= 2

    def col_spec(p):
        return pl.BlockSpec((seq, wide), lambda b, hp: (b, p * nblk + hp))

    par_spec = pl.BlockSpec((1, wide), lambda b, hp: (0, hp))
    act_spec = pl.BlockSpec((seq, wide), lambda b, hp: (b, hp))
    return pl.pallas_call(
        functools.partial(_wkv_kernel, group=group),
        grid=(batch, nblk),
        in_specs=[col_spec(0), col_spec(1), col_spec(2), col_spec(3), act_spec, act_spec,
                  par_spec, par_spec, par_spec, par_spec, par_spec],
        out_specs=act_spec,
        out_shape=jax.ShapeDtypeStruct((m, w), BF16),
        scratch_shapes=[pltpu.VMEM((ninst, CHUNK + two, LANES), BF16),
                        pltpu.VMEM((ninst, CHUNK, LANES), F32),
                        pltpu.VMEM((ninst, two, LANES), F32),
                        pltpu.VMEM((ninst, two, LANES), F32),
                        pltpu.VMEM((seq, wide), F32)],
        compiler_params=_cparams(("parallel", "parallel")),
        name="wkv7",
    )(proj, proj, proj, proj, lw, ic, kk, ka, rk, lnw, lnb)


def _attn_kernel(q_ref, k_ref, v_ref, g_ref, lam_ref, sw_ref, o_ref,
                 z_a, z_b, p_a, p_b, acc_sc, ka1_sc, ka2_sc, q1_sc, q2_sc, *, tq, n_heads, lambda_init):
    seq = q_ref.shape[0]
    nq = seq // tq
    tk = tq
    half = BF16_EXACT_INT
    assert tk <= 2 * half
    ng = 2 * tq // LANES
    h = pl.program_id(1)
    lane = lax.broadcasted_iota(jnp.int32, (1, LANES), 1)
    slope = jnp.exp2(jnp.full((1, LANES), -8.0 / n_heads, F32) * (h + 1).astype(F32)) * LOG2E
    s_hi = slope.astype(BF16).astype(F32)
    s_lo = slope - s_hi
    s_pat = jnp.where(lane % 2 == 0, s_hi, s_lo)

    row_b = lax.broadcasted_iota(jnp.int32, (tk, LANES), 0)
    lane_b = lax.broadcasted_iota(jnp.int32, (tk, LANES), 1)
    kidx = jnp.where(lane_b % 4 < 2, row_b % half, (row_b // half) * half).astype(F32)
    aug1 = jnp.where((lane_b >= HEAD) & (lane_b < HEAD + 4), kidx, 0.0).astype(BF16)
    aug2 = jnp.where(lane_b < 4, kidx, 0.0).astype(BF16)
    lo_b = (lane_b < HEAD).astype(F32).astype(BF16)
    hi_b = (lane_b >= HEAD).astype(F32).astype(BF16)
    for j in range(seq // tk):
        ks = k_ref[j * tk:(j + 1) * tk, :]
        ka1_sc[j * tk:(j + 1) * tk, :] = ks * lo_b + aug1
        ka2_sc[j * tk:(j + 1) * tk, :] = ks * hi_b + aug2

    tri = (lax.broadcasted_iota(jnp.int32, (LANES, LANES), 0) <= lax.broadcasted_iota(jnp.int32, (LANES, LANES), 1))
    lam = lam_ref[...]
    lam_full = (jnp.exp(jnp.sum(lam[0:1] * lam[1:2], axis=1, keepdims=True))
                - jnp.exp(jnp.sum(lam[2:3] * lam[3:4], axis=1, keepdims=True)) + lambda_init)
    sw_scaled = sw_ref[...] * (1.0 - lambda_init)

    def load_q(qi):
        qf = q_ref[qi * tq:(qi + 1) * tq, :].astype(F32) * (LOG2E / math.sqrt(HEAD))
        q1_sc[...] = jnp.where(lane < HEAD, qf, jnp.where(lane < HEAD + 4, s_pat, 0.0)).astype(BF16)
        q2_sc[...] = jnp.where(lane >= HEAD, qf, jnp.where(lane < 4, s_pat, 0.0)).astype(BF16)

    def scores_to(j, z_ref, diag):
        hq = ng // 4
        for ka_sc, q_sc, g0 in ((ka1_sc, q1_sc, 0), (ka2_sc, q2_sc, ng // 2)):
            if not diag:
                z = _dot_nt(ka_sc[j * tk:(j + 1) * tk, :], q_sc[...])
                for g in range(ng // 2):
                    z_ref[g0 + g] = z[:, g * LANES:(g + 1) * LANES]
            else:
                top = _dot_nt(ka_sc[j * tk:j * tk + tk // 2, :], q_sc[...])
                bot = _dot_nt(ka_sc[j * tk + tk // 2:(j + 1) * tk, :], q_sc[tq // 2:, :])
                for g in range(ng // 2):
                    z_ref[g0 + g, 0:tk // 2, :] = top[:, g * LANES:(g + 1) * LANES]
                for g in range(hq):
                    z_ref[g0 + hq + g, tk // 2:tk, :] = bot[:, g * LANES:(g + 1) * LANES]

    def accumulate(j, p_ref, alpha, first, diag):
        hq = ng // 4
        if not diag:
            p_all = jnp.concatenate([p_ref[g] for g in range(ng)], axis=1)
            pv = _dot_tn(v_ref[j * tk:(j + 1) * tk, :], p_all)
            parts = [pv[:, g * LANES:(g + 1) * LANES] for g in range(ng)]
        else:
            late = [g for g in range(ng) if g % (ng // 2) >= hq]
            p_top = jnp.concatenate([p_ref[g, 0:tk // 2, :] for g in range(ng)], axis=1)
            p_bot = jnp.concatenate([p_ref[g, tk // 2:tk, :] for g in late], axis=1)
            pv_top = _dot_tn(v_ref[j * tk:j * tk + tk // 2, :], p_top)
            pv_bot = _dot_tn(v_ref[j * tk + tk // 2:(j + 1) * tk, :], p_bot)
            parts = [pv_top[:, g * LANES:(g + 1) * LANES] for g in range(ng)]
            for n, g in enumerate(late):
                parts[g] = parts[g] + pv_bot[:, n * LANES:(n + 1) * LANES]
        for g in range(ng):
            sl = slice(g * LANES, (g + 1) * LANES)
            acc_sc[g] = parts[g] if first else alpha[:, sl] * acc_sc[g] + parts[g]

    def softmax(qi, j, z_ref, p_ref, m, l):
        c_blk = slope * float((j - qi) * tq)
        ms, ls, alphas = [], [], []
        for g in range(ng):
            sl = slice(g * LANES, (g + 1) * LANES)
            nrows = tk
            if j == qi:
                nrows = (g % (ng // 2)) * LANES + LANES
                fill_to = tk // 2 if nrows <= tk // 2 else tk
                if nrows < fill_to:
                    p_ref[g, nrows:fill_to, :] = jnp.zeros((fill_to - nrows, LANES), BF16)
            diag_tile = jnp.where(tri, z_ref[g, nrows - LANES:nrows, :], NEG_BIG) if j == qi else None
            nplain = nrows - LANES if j == qi else nrows
            mx = diag_tile.max(axis=0, keepdims=True) if diag_tile is not None else None
            if nplain:
                mp = jnp.max(z_ref[g, 0:nplain, :], axis=0, keepdims=True)
                mx = mp if mx is None else jnp.maximum(mx, mp)
            mg = m[:, sl]
            mn = jnp.maximum(mg, mx + c_blk)
            alpha = jnp.exp2(mg - mn)
            shift = mn - c_blk
            lsum = None
            for r0 in range(0, nrows, LANES):
                zt = diag_tile if (j == qi and r0 == nplain) else z_ref[g, r0:r0 + LANES, :]
                pt = jnp.exp2(zt - shift)
                part = jnp.sum(pt, axis=0, keepdims=True)
                lsum = part if lsum is None else lsum + part
                p_ref[g, r0:r0 + LANES, :] = pt.astype(BF16)
            ls.append(alpha * l[:, sl] + lsum)
            ms.append(mn)
            alphas.append(alpha)
        return jnp.concatenate(ms, axis=1), jnp.concatenate(ls, axis=1), jnp.concatenate(alphas, axis=1)

    def finalize(qi, l):
        on = jnp.concatenate([acc_sc[g] for g in range(ng)], axis=1) / l
        ot = on[:, :tq] - lam_full * on[:, tq:]
        ot = ot * lax.rsqrt(jnp.mean(ot * ot, axis=0, keepdims=True) + SUBLN_EPS)
        g = g_ref[qi * tq:(qi + 1) * tq, :].astype(F32)
        o_ref[qi * tq:(qi + 1) * tq, :] = (ot.T * sw_scaled * (g / (1.0 + jnp.exp2(g * (-LOG2E))))).astype(o_ref.dtype)

    items = [(qi, j) for qi in range(nq) for j in range(qi + 1)]
    zs, ps = (z_a, z_b), (p_a, p_b)
    load_q(0)
    scores_to(0, z_a, True)
    m = l = alpha = None
    for n, (qi, j) in enumerate(items):
        if n > 0:
            pqi, pj = items[n - 1]
            accumulate(pj, ps[(n - 1) % 2], alpha, pj == 0, pj == pqi)
            if pqi != qi:
                finalize(pqi, l)
        if j == 0:
            m = jnp.full((1, 2 * tq), NEG_BIG, F32)
            l = jnp.zeros((1, 2 * tq), F32)
        if n + 1 < len(items):
            nqi, nj = items[n + 1]
            if nqi != qi:
                load_q(nqi)
            scores_to(nj, zs[(n + 1) % 2], nj == nqi)
        m, l, alpha = softmax(qi, j, zs[n % 2], ps[n % 2], m, l)
    pqi, pj = items[-1]
    accumulate(pj, ps[(len(items) - 1) % 2], alpha, pj == 0, pj == pqi)
    finalize(pqi, l)


def _diff_attn(qkvg, lam, sw, *, batch, seq, tq, lambda_init):
    m, n4 = qkvg.shape
    w = n4 // 4
    nh = w // LANES
    ngl = 2 * tq // LANES

    def col_spec(p):
        return pl.BlockSpec((seq, LANES), lambda b, h: (b, p * nh + h))

    return pl.pallas_call(
        functools.partial(_attn_kernel, tq=tq, n_heads=nh, lambda_init=lambda_init),
        grid=(batch, nh),
        in_specs=[col_spec(0), col_spec(1), col_spec(2), col_spec(3),
                  pl.BlockSpec(lam.shape, lambda b, h: (0, 0)),
                  pl.BlockSpec((1, LANES), lambda b, h: (0, 0))],
        out_specs=pl.BlockSpec((seq, LANES), lambda b, h: (b, h)),
        out_shape=jax.ShapeDtypeStruct((m, w), BF16),
        scratch_shapes=[pltpu.VMEM((ngl, tq, LANES), F32), pltpu.VMEM((ngl, tq, LANES), F32),
                        pltpu.VMEM((ngl, tq, LANES), BF16), pltpu.VMEM((ngl, tq, LANES), BF16),
                        pltpu.VMEM((ngl, LANES, LANES), F32),
                        pltpu.VMEM((seq, LANES), BF16), pltpu.VMEM((seq, LANES), BF16),
                        pltpu.VMEM((tq, LANES), BF16), pltpu.VMEM((tq, LANES), BF16)],
        compiler_params=_cparams(("parallel", "parallel")),
        name="diff_attn",
    )(qkvg, qkvg, qkvg, qkvg, lam, sw)


def kernel(x, norm_w, rwkv_mu, rwkv_w_in, rwkv_w0, rwkv_w1, rwkv_w2, rwkv_a0, rwkv_a1, rwkv_a2, rwkv_k_k, rwkv_k_a, rwkv_r_k, rwkv_ln_w, rwkv_ln_b, rwkv_w_out, diff_w_in, diff_lambda, diff_subln_w, diff_w_out, final_norm_w):
    batch, seq, d = x.shape
    m = batch * seq
    assert d % LANES == 0 and seq % CHUNK == 0
    assert norm_w.shape[0] == 2 and rwkv_mu.shape[0] == 1 and diff_w_in.shape[0] == 1
    tm_big = min(MM_TILE, m)
    tn_big = min(MM_TILE, d)
    tm_prep = min(ROW_TILE, seq)
    tm_last = min(ROW_TILE, m)
    tq = min(ATTN_BLOCK, seq)

    x2 = x.reshape(m, d)
    row = lambda a: a.reshape(1, -1)
    lora = rwkv_w1.shape[-1]
    pad_c = lambda a: jnp.pad(a, ((0, 0), (0, LORA_PAD - lora)))
    pad_r = lambda a: jnp.pad(a, ((0, LORA_PAD - lora), (0, 0))).astype(BF16)

    mu = rwkv_mu[0]
    mu_rkvg = jnp.stack([mu[0], mu[2], mu[3], mu[5]])
    lora_in = jnp.concatenate([pad_c(rwkv_w1[0]), pad_c(rwkv_a1[0])], axis=1)
    lora_shift = jnp.concatenate([pad_c(mu[1][:, None] * rwkv_w1[0]), pad_c(mu[4][:, None] * rwkv_a1[0])], axis=1)
    wl1 = jnp.concatenate([lora_in, lora_shift], axis=0).astype(BF16)
    xs, lw, ic = _rwkv_prep(x2, row(norm_w[0]), mu_rkvg, row(rwkv_w0[0]), wl1, pad_r(rwkv_w2[0]),
                            row(rwkv_a0[0]), pad_r(rwkv_a2[0]), batch=batch, seq=seq, tm=tm_prep)
    proj = _matmul_wcast(xs, rwkv_w_in[0], tm=tm_big, tn=tn_big, out_dtype=F32, groups=4)
    yg = _wkv(proj, lw, ic, row(rwkv_k_k[0]), row(rwkv_k_a[0]), row(rwkv_r_k[0]), row(rwkv_ln_w[0]),
              row(rwkv_ln_b[0]), batch=batch, seq=seq)
    x1, h1 = _out_proj(yg, rwkv_w_out[0], x2, row(norm_w[1]), tm=tm_last, keep_sum=True,
                       norm_dtype=BF16)

    lambda_init = 0.8 - 0.6 * math.exp(-0.3 * 1)
    qkvg = _matmul_wcast(h1, diff_w_in[0], tm=min(MM_ROWS_BF16_OUT, m), tn=tn_big, out_dtype=BF16,
                         vmem_limit=MM_VMEM_LIMIT_BF16_OUT)
    att = _diff_attn(qkvg, diff_lambda[0], row(diff_subln_w[0]), batch=batch, seq=seq, tq=tq,
                     lambda_init=lambda_init)
    (out,) = _out_proj(att, diff_w_out[0], x1, row(final_norm_w), tm=tm_last, keep_sum=False,
                       norm_dtype=F32)
    return out.reshape(batch, seq, d)
```

```python
import functools
import math

import jax
import jax.numpy as jnp
from jax import lax
from jax.experimental import pallas as pl
from jax.experimental.pallas import tpu as pltpu

F32 = jnp.float32
BF16 = jnp.bfloat16

NORM_EPS = 1e-6
GN_EPS = 64e-5
SUBLN_EPS = 1e-5
HEAD = 64
LANES = 128
CHUNK = 64
LORA_PAD = 128
VMEM_LIMIT = 48 * 1024 * 1024
NEG_BIG = -1e30
LOG2E = 1.0 / math.log(2.0)
BF16_EXACT_INT = 256

MM_TILE = 1024
MM_ROWS_BF16_OUT = 2048
MM_VMEM_LIMIT_BF16_OUT = 56 * 1024 * 1024
OUT_PROJ_VMEM_LIMIT = 56 * 1024 * 1024
ROW_TILE = 512
ATTN_BLOCK = 512


def _cparams(sem, vmem_limit=VMEM_LIMIT):
    return pltpu.CompilerParams(dimension_semantics=sem, vmem_limit_bytes=vmem_limit)


def _dot(a, b):
    return jnp.dot(a, b, preferred_element_type=F32)


def _dot_nt(a, b):
    return lax.dot_general(a, b, (((1,), (1,)), ((), ())), preferred_element_type=F32)


def _dot_tn(a, b):
    return lax.dot_general(a, b, (((0,), (0,)), ((), ())), preferred_element_type=F32)


def _mm_exact_rhs(a, e_bf16, parts):
    acc = None
    rem = a
    for _ in range(parts):
        piece = rem.astype(BF16)
        term = _dot(piece, e_bf16)
        acc = term if acc is None else acc + term
        rem = rem - piece.astype(F32)
    return acc


def _mm_exact_lhs(e_bf16, a, parts):
    acc = None
    rem = a
    for _ in range(parts):
        piece = rem.astype(BF16)
        term = _dot(e_bf16, piece)
        acc = term if acc is None else acc + term
        rem = rem - piece.astype(F32)
    return acc


def _rwkv_prep_kernel(x_ref, nw_ref, mu_ref, w0_ref, wl1_ref, w2_ref, a0_ref, a2_ref,
                      xs_ref, lw_ref, ic_ref, carry_ref):
    t = pl.program_id(1)
    x = x_ref[...]
    tm = x.shape[0]
    h = x * lax.rsqrt(jnp.mean(x * x, axis=-1, keepdims=True) + NORM_EPS) * nw_ref[...]

    @pl.when(t == 0)
    def _():
        carry_ref[...] = jnp.zeros_like(carry_ref)

    prev_last = carry_ref[0:1, :]
    rolled = pltpu.roll(h, 1, axis=0)
    row = lax.broadcasted_iota(jnp.int32, h.shape, 0)
    h_prev = jnp.where(row == 0, prev_last, rolled)
    carry_ref[0:1, :] = h[tm - 1:tm, :]
    d = h_prev - h
    mu = mu_ref[...]
    for p in range(4):
        xs_ref[p] = (h + d * mu[p:p + 1]).astype(BF16)
    t1 = _dot(jnp.concatenate([h, d], axis=1).astype(BF16), wl1_ref[...])
    zw = w0_ref[...] + _dot(jnp.tanh(t1[:, :LORA_PAD]).astype(BF16), w2_ref[...])
    lw_ref[...] = (-math.exp(-0.5)) / (1.0 + jnp.exp2(zw * (-LOG2E)))
    za = a0_ref[...] + _dot(t1[:, LORA_PAD:].astype(BF16), a2_ref[...])
    ic_ref[...] = 1.0 / (1.0 + jnp.exp2(za * (-LOG2E)))


def _rwkv_prep(x2, nw, mu, w0, wl1, w2p, a0, a2p, *, batch, seq, tm):
    m, d = x2.shape
    nt = seq // tm
    row_spec = pl.BlockSpec((tm, d), lambda b, t: (b * nt + t, 0))
    vec_spec = pl.BlockSpec((1, d), lambda b, t: (0, 0))

    def full(a):
        return pl.BlockSpec(a.shape, lambda b, t: (0,) * a.ndim)

    return pl.pallas_call(
        _rwkv_prep_kernel,
        grid=(batch, nt),
        in_specs=[row_spec, vec_spec, full(mu), vec_spec, full(wl1), full(w2p), vec_spec, full(a2p)],
        out_specs=[pl.BlockSpec((4, tm, d), lambda b, t: (0, b * nt + t, 0)), row_spec, row_spec],
        out_shape=[jax.ShapeDtypeStruct((4, m, d), BF16),
                   jax.ShapeDtypeStruct((m, d), F32),
                   jax.ShapeDtypeStruct((m, d), F32)],
        scratch_shapes=[pltpu.VMEM((8, d), F32)],
        compiler_params=_cparams(("parallel", "arbitrary")),
        name="rwkv_prep",
    )(x2, nw, mu, w0, wl1, w2p, a0, a2p)


def _mm_wcast_kernel(a_ref, b_ref, o_ref, w_sc):
    @pl.when(pl.program_id(1) == 0)
    def _():
        w_sc[...] = b_ref[...].astype(BF16)

    o_ref[...] = _dot(a_ref[...], w_sc[...]).astype(o_ref.dtype)


def _matmul_wcast(a, w, *, tm, tn, out_dtype, groups=1, vmem_limit=VMEM_LIMIT):
    m, k = a.shape[-2:]
    n = w.shape[1]
    per_group = (n // groups) // tn
    if groups > 1:
        a_spec = pl.BlockSpec((None, tm, k), lambda j, i: (j // per_group, i, 0))
    else:
        a_spec = pl.BlockSpec((tm, k), lambda j, i: (i, 0))
    return pl.pallas_call(
        _mm_wcast_kernel,
        grid=(n // tn, m // tm),
        in_specs=[a_spec, pl.BlockSpec((k, tn), lambda j, i: (0, j))],
        out_specs=pl.BlockSpec((tm, tn), lambda j, i: (i, j)),
        out_shape=jax.ShapeDtypeStruct((m, n), out_dtype),
        scratch_shapes=[pltpu.VMEM((k, tn), BF16)],
        compiler_params=_cparams(("parallel", "arbitrary"), vmem_limit),
        name="proj" if groups > 1 else "matmul",
    )(a, w)


def _out_proj_kernel(a_ref, b_ref, r_ref, nw_ref, *refs):
    *o_refs, w_sc = refs

    @pl.when(pl.program_id(0) == 0)
    def _():
        w_sc[...] = b_ref[...].astype(BF16)

    y = r_ref[...] + _dot(a_ref[...], w_sc[...])
    yn = y * lax.rsqrt(jnp.mean(y * y, axis=-1, keepdims=True) + NORM_EPS) * nw_ref[...]
    if len(o_refs) == 2:
        o_refs[0][...] = y
    o_refs[-1][...] = yn.astype(o_refs[-1].dtype)


def _out_proj(a, w, res, nw, *, tm, keep_sum, norm_dtype):
    m, k = a.shape
    n = w.shape[1]
    row_spec = pl.BlockSpec((tm, n), lambda i: (i, 0))
    out_specs = [row_spec, row_spec] if keep_sum else [row_spec]
    out_shape = [jax.ShapeDtypeStruct((m, n), norm_dtype)]
    if keep_sum:
        out_shape = [jax.ShapeDtypeStruct((m, n), F32)] + out_shape
    return pl.pallas_call(
        _out_proj_kernel,
        grid=(m // tm,),
        in_specs=[pl.BlockSpec((tm, k), lambda i: (i, 0)),
                  pl.BlockSpec((k, n), lambda i: (0, 0), pipeline_mode=pl.Buffered(1)),
                  row_spec,
                  pl.BlockSpec((1, n), lambda i: (0, 0))],
        out_specs=out_specs,
        out_shape=out_shape,
        scratch_shapes=[pltpu.VMEM((k, n), BF16)],
        compiler_params=_cparams(("arbitrary",), OUT_PROJ_VMEM_LIMIT),
        name="out_proj_sum_norm" if keep_sum else "out_proj_norm",
    )(a, w, res, nw)


WKV_GROUP = 8
WKV_STREAMS = 2
WKV_STAGE_GAPS = 10


def _wkv_kernel(r_ref, k_ref, v_ref, g_ref, lw_ref, ic_ref, kk_ref, ka_ref, rk_ref, lnw_ref, lnb_ref,
                o_ref, rm_sc, y0_sc, g_sc, wc_sc, y_sc, *, group):
    seq = r_ref.shape[0]
    streams = r_ref.shape[1] // LANES
    ngroups = seq // (CHUNK * group)
    two = 2 * CHUNK
    inst = [(u, st) for u in range(group) for st in range(streams)]
    ninst = len(inst)

    def lanes(st):
        return slice(st * LANES, (st + 1) * LANES)

    lane1 = lax.broadcasted_iota(jnp.int32, (1, LANES), 1)
    m0 = (lane1 < HEAD).astype(F32)
    m1 = 1.0 - m0
    row2 = lax.broadcasted_iota(jnp.int32, (two, two), 0)
    col2 = lax.broadcasted_iota(jnp.int32, (two, two), 1)
    eye2 = (row2 == col2).astype(F32)
    same_head = (row2 < HEAD) == (col2 < HEAD)
    ones_bd = same_head.astype(BF16)
    row1 = lax.broadcasted_iota(jnp.int32, (CHUNK, LANES), 0)
    col1 = lax.broadcasted_iota(jnp.int32, (CHUNK, LANES), 1) % CHUNK
    strict = row1 > col1
    incl = row1 >= col1
    eye1 = (row1 == col1).astype(F32)
    tril_c = incl[:, :CHUNK].astype(BF16)
    zeros_b = jnp.zeros((CHUNK, LANES), BF16)

    kk_p = [kk_ref[:, lanes(st)] for st in range(streams)]
    ka_p = [ka_ref[:, lanes(st)] for st in range(streams)]

    def segsum(z, parts):
        return _mm_exact_rhs(z, ones_bd, parts)

    m0b = m0.astype(BF16)
    m1b = m1.astype(BF16)

    def bdiag(z):
        zb = z.astype(BF16)
        return jnp.concatenate([zb * m0b, zb * m1b], axis=0)

    def chunk_rows(c):
        return pl.ds(pl.multiple_of(c * CHUNK, CHUNK), CHUNK)

    def phase_a(gi):
        us = range(ninst)
        rows = [chunk_rows(gi * group + u) for u, _ in inst]
        cols = [lanes(st) for _, st in inst]
        k = [k_ref[rows[i], cols[i]] for i in us]
        lw = [lw_ref[rows[i], cols[i]] for i in us]
        kkv = [k[i] * kk_p[inst[i][1]] for i in us]
        n2 = [segsum(kkv[u] * kkv[u], 1) for u in us]
        lin = [_mm_exact_lhs(tril_c, lw[u], 2) for u in us]
        yield
        a_n, r_n, v_b, bk_n, wcb, a_all = [], [], [], [], [], []
        for u in us:
            ic = ic_ref[rows[u], cols[u]]
            kkn = kkv[u] * lax.rsqrt(jnp.maximum(n2[u], 1e-24))
            kh = k[u] * (1.0 + (ic - 1.0) * ka_p[inst[u][1]])
            e_in = jnp.exp(lin[u])
            e_ex = jnp.where(row1 == 0, 1.0, pltpu.roll(e_in, 1, axis=0))
            e_neg = jnp.exp(-lin[u])
            b_f = kkn * ic * e_neg
            k_f = kh * e_neg
            a_f = -kkn * e_ex
            a_n.append(a_f)
            r_n.append(r_ref[rows[u], cols[u]] * e_in)
            v_b.append(bdiag(v_ref[rows[u], cols[u]]))
            a_all.append(_dot_nt(jnp.concatenate([a_f, r_n[u]], axis=0).astype(BF16),
                                 jnp.concatenate([bdiag(b_f), bdiag(k_f)], axis=0)))
            bk_n.append(jnp.concatenate([b_f, k_f], axis=0).astype(BF16))
            wcol = jnp.sum(eye2 * e_in[CHUNK - 1:CHUNK, :], axis=1, keepdims=True)
            wcb.append(jnp.broadcast_to(wcol, (two, LANES)))
        yield
        t_inv, pw, a_rb, av = [], [], [], []
        for u in us:
            a_ab = jnp.where(strict, a_all[u][:CHUNK, :LANES], 0.0)
            a_ak = jnp.where(strict, a_all[u][:CHUNK, LANES:], 0.0)
            a_rk = jnp.where(incl, a_all[u][CHUNK:, LANES:], 0.0)
            a_rb.append(jnp.where(incl, a_all[u][CHUNK:, :LANES], 0.0).astype(BF16))
            av.append(_dot(jnp.concatenate([a_ak, a_rk], axis=0).astype(BF16), v_b[u]))
            t_inv.append(eye1 + a_ab)
            pw.append(a_ab)
        yield
        pw = [_dot(pw[u].astype(BF16), bdiag(pw[u])) for u in us]
        yield
        for _ in range(4):
            both = [_dot(jnp.concatenate([t_inv[u], pw[u]], axis=0).astype(BF16), bdiag(pw[u])) for u in us]
            t_inv = [t_inv[u] + both[u][:CHUNK] for u in us]
            pw = [both[u][CHUNK:] for u in us]
            yield
        t_inv = [t_inv[u] + _dot(t_inv[u].astype(BF16), bdiag(pw[u])) for u in us]
        yield
        pq = [_dot(t_inv[u].astype(BF16),
                   jnp.concatenate([bdiag(a_n[u]), bdiag(av[u][:CHUNK])], axis=1)) for u in us]
        yield
        rq = [_dot(a_rb[u], jnp.concatenate([bdiag(pq[u][:, :LANES]), bdiag(pq[u][:, LANES:])], axis=1))
              for u in us]
        mg = [_dot_tn(bk_n[u], jnp.concatenate([pq[u].astype(BF16),
                                                jnp.concatenate([zeros_b, v_b[u][:CHUNK] + v_b[u][CHUNK:]], axis=1)],
                                               axis=0)) for u in us]
        yield
        for u in us:
            m_bd = jnp.where(same_head, mg[u][:, :LANES], 0.0) * wcb[u]
            rm_sc[u] = jnp.concatenate([r_n[u] + rq[u][:, :LANES], m_bd], axis=0).astype(BF16)
            g_sc[u] = jnp.where(same_head, mg[u][:, LANES:], 0.0) * wcb[u]
            y0_sc[u] = rq[u][:, LANES:] + av[u][CHUNK:]
            wc_sc[u] = wcb[u]

    def state_step(gi, i, hs):
        u, st = inst[i]
        res = _dot(rm_sc[i], hs[st].astype(BF16))
        y_sc[chunk_rows(gi * group + u), lanes(st)] = res[:CHUNK] + y0_sc[i]
        return tuple(wc_sc[i] * hs[st] + res[CHUNK:] + g_sc[i] if t == st else hs[t] for t in range(streams))

    inv_n = 1.0 / HEAD
    grows = group * CHUNK

    def finish(gi):
        rows = pl.ds(pl.multiple_of(gi * grows, grows), grows)
        for st in range(streams):
            sl = lanes(st)
            y = y_sc[rows, sl]
            mean = segsum(y, 1) * inv_n
            yc = y - mean
            var = segsum(yc * yc, 1) * inv_n
            gn = yc * lax.rsqrt(var + GN_EPS) * lnw_ref[:, sl] + lnb_ref[:, sl]
            kh = k_ref[rows, sl] * (1.0 + (ic_ref[rows, sl] - 1.0) * ka_p[st])
            bonus = segsum(r_ref[rows, sl] * kh * rk_ref[:, sl], 1) * v_ref[rows, sl]
            g = g_ref[rows, sl]
            o_ref[rows, sl] = ((gn + bonus) * (g / (1.0 + jnp.exp2(g * (-LOG2E))))).astype(o_ref.dtype)

    def run(a_gi, b_gi, c_gi, hs):
        done = 0
        if a_gi is None:
            for i in range(ninst):
                hs = state_step(b_gi, i, hs)
        else:
            per_slot = -(-ninst // WKV_STAGE_GAPS)
            nyield = 0
            for i, _ in enumerate(phase_a(a_gi)):
                nyield += 1
                for _ in range(per_slot if (b_gi is not None and i >= 1) else 0):
                    if done < ninst:
                        hs = state_step(b_gi, done, hs)
                        done += 1
            assert nyield == WKV_STAGE_GAPS + 1 and (b_gi is None or done == ninst)
        if c_gi is not None:
            finish(c_gi)
        return hs

    run(0, None, None, None)
    hs = run(1, 0, None, tuple(jnp.zeros((two, LANES), F32) for _ in range(streams)))
    if ngroups > 2:
        hs = lax.fori_loop(1, ngroups - 1, lambda gi, hs: run(gi + 1, gi, gi - 1, hs), hs)
    run(None, ngroups - 1, ngroups - 2, hs)
    finish(ngroups - 1)


def _wkv(proj, lw, ic, kk, ka, rk, lnw, lnb, *, batch, seq):
    m, n4 = proj.shape
    w = n4 // 4
    group = WKV_GROUP
    streams = WKV_STREAMS if (w // LANES) % WKV_STREAMS == 0 else 1
    wide = streams * LANES
    nblk = w // wide
    two = 2 * CHUNK
    ninst = group * streams
    assert seq % (CHUNK * group) == 0 and seq // (CHUNK * group) >= 2

    def col_spec(p):
        return pl.BlockSpec((seq, wide), lambda b, hp: (b, p * nblk + hp))

    par_spec = pl.BlockSpec((1, wide), lambda b, hp: (0, hp))
    act_spec = pl.BlockSpec((seq, wide), lambda b, hp: (b, hp))
    return pl.pallas_call(
        functools.partial(_wkv_kernel, group=group),
        grid=(batch, nblk),
        in_specs=[col_spec(0), col_spec(1), col_spec(2), col_spec(3), act_spec, act_spec,
                  par_spec, par_spec, par_spec, par_spec, par_spec],
        out_specs=act_spec,
        out_shape=jax.ShapeDtypeStruct((m, w), BF16),
        scratch_shapes=[pltpu.VMEM((ninst, CHUNK + two, LANES), BF16),
                        pltpu.VMEM((ninst, CHUNK, LANES), F32),
                        pltpu.VMEM((ninst, two, LANES), F32),
                        pltpu.VMEM((ninst, two, LANES), F32),
                        pltpu.VMEM((seq, wide), F32)],
        compiler_params=_cparams(("parallel", "parallel")),
        name="wkv7",
    )(proj, proj, proj, proj, lw, ic, kk, ka, rk, lnw, lnb)


ATTN_HEADS_PER_STEP = 2


def _attn_kernel(q_ref, k_ref, v_ref, g_ref, lam_ref, sw_ref, o_ref, *scratch, tq, n_heads, lambda_init):
    per_step = q_ref.shape[1] // LANES
    for hh in range(per_step):
        sl = pl.ds(hh * LANES, LANES)
        _attn_head(pl.program_id(1) * per_step + hh, q_ref.at[:, sl], k_ref.at[:, sl], v_ref.at[:, sl],
                   g_ref.at[:, sl], lam_ref, sw_ref, o_ref.at[:, sl], *scratch,
                   tq=tq, n_heads=n_heads, lambda_init=lambda_init)


def _attn_head(h, q_ref, k_ref, v_ref, g_ref, lam_ref, sw_ref, o_ref,
               z_a, z_b, p_a, p_b, acc_sc, ka1_sc, ka2_sc, q1_sc, q2_sc, *, tq, n_heads, lambda_init):
    seq = q_ref.shape[0]
    nq = seq // tq
    tk = tq
    half = BF16_EXACT_INT
    assert tk <= 2 * half
    ng = 2 * tq // LANES
    lane = lax.broadcasted_iota(jnp.int32, (1, LANES), 1)
    slope = jnp.exp2(jnp.full((1, LANES), -8.0 / n_heads, F32) * (h + 1).astype(F32)) * LOG2E
    s_hi = slope.astype(BF16).astype(F32)
    s_lo = slope - s_hi
    s_pat = jnp.where(lane % 2 == 0, s_hi, s_lo)

    row_b = lax.broadcasted_iota(jnp.int32, (tk, LANES), 0)
    lane_b = lax.broadcasted_iota(jnp.int32, (tk, LANES), 1)
    kidx = jnp.where(lane_b % 4 < 2, row_b % half, (row_b // half) * half).astype(F32)
    aug1 = jnp.where((lane_b >= HEAD) & (lane_b < HEAD + 4), kidx, 0.0).astype(BF16)
    aug2 = jnp.where(lane_b < 4, kidx, 0.0).astype(BF16)
    lo_b = (lane_b < HEAD).astype(F32).astype(BF16)
    hi_b = (lane_b >= HEAD).astype(F32).astype(BF16)
    for j in range(seq // tk):
        ks = k_ref[j * tk:(j + 1) * tk, :]
        ka1_sc[j * tk:(j + 1) * tk, :] = ks * lo_b + aug1
        ka2_sc[j * tk:(j + 1) * tk, :] = ks * hi_b + aug2

    tri = (lax.broadcasted_iota(jnp.int32, (LANES, LANES), 0) <= lax.broadcasted_iota(jnp.int32, (LANES, LANES), 1))
    lam = lam_ref[...]
    lam_full = (jnp.exp(jnp.sum(lam[0:1] * lam[1:2], axis=1, keepdims=True))
                - jnp.exp(jnp.sum(lam[2:3] * lam[3:4], axis=1, keepdims=True)) + lambda_init)
    sw_scaled = sw_ref[...] * (1.0 - lambda_init)

    def load_q(qi):
        qf = q_ref[qi * tq:(qi + 1) * tq, :].astype(F32) * (LOG2E / math.sqrt(HEAD))
        q1_sc[...] = jnp.where(lane < HEAD, qf, jnp.where(lane < HEAD + 4, s_pat, 0.0)).astype(BF16)
        q2_sc[...] = jnp.where(lane >= HEAD, qf, jnp.where(lane < 4, s_pat, 0.0)).astype(BF16)

    def scores_to(j, z_ref, diag):
        hq = ng // 4
        for ka_sc, q_sc, g0 in ((ka1_sc, q1_sc, 0), (ka2_sc, q2_sc, ng // 2)):
            if not diag:
                z = _dot_nt(ka_sc[j * tk:(j + 1) * tk, :], q_sc[...])
                for g in range(ng // 2):
                    z_ref[g0 + g] = z[:, g * LANES:(g + 1) * LANES]
            else:
                top = _dot_nt(ka_sc[j * tk:j * tk + tk // 2, :], q_sc[...])
                bot = _dot_nt(ka_sc[j * tk + tk // 2:(j + 1) * tk, :], q_sc[tq // 2:, :])
                for g in range(ng // 2):
                    z_ref[g0 + g, 0:tk // 2, :] = top[:, g * LANES:(g + 1) * LANES]
                for g in range(hq):
                    z_ref[g0 + hq + g, tk // 2:tk, :] = bot[:, g * LANES:(g + 1) * LANES]

    def accumulate(j, p_ref, alpha, first, diag):
        hq = ng // 4
        if not diag:
            p_all = jnp.concatenate([p_ref[g] for g in range(ng)], axis=1)
            pv = _dot_tn(v_ref[j * tk:(j + 1) * tk, :], p_all)
            parts = [pv[:, g * LANES:(g + 1) * LANES] for g in range(ng)]
        else:
            late = [g for g in range(ng) if g % (ng // 2) >= hq]
            p_top = jnp.concatenate([p_ref[g, 0:tk // 2, :] for g in range(ng)], axis=1)
            p_bot = jnp.concatenate([p_ref[g, tk // 2:tk, :] for g in late], axis=1)
            pv_top = _dot_tn(v_ref[j * tk:j * tk + tk // 2, :], p_top)
            pv_bot = _dot_tn(v_ref[j * tk + tk // 2:(j + 1) * tk, :], p_bot)
            parts = [pv_top[:, g * LANES:(g + 1) * LANES] for g in range(ng)]
            for n, g in enumerate(late):
                parts[g] = parts[g] + pv_bot[:, n * LANES:(n + 1) * LANES]
        for g in range(ng):
            sl = slice(g * LANES, (g + 1) * LANES)
            acc_sc[g] = parts[g] if first else alpha[:, sl] * acc_sc[g] + parts[g]

    def softmax(qi, j, z_ref, p_ref, m, l):
        c_blk = slope * float((j - qi) * tq)
        ms, ls, alphas = [], [], []
        for g in range(ng):
            sl = slice(g * LANES, (g + 1) * LANES)
            nrows = tk
            if j == qi:
                nrows = (g % (ng // 2)) * LANES + LANES
                fill_to = tk // 2 if nrows <= tk // 2 else tk
                if nrows < fill_to:
                    p_ref[g, nrows:fill_to, :] = jnp.zeros((fill_to - nrows, LANES), BF16)
            diag_tile = jnp.where(tri, z_ref[g, nrows - LANES:nrows, :], NEG_BIG) if j == qi else None
            nplain = nrows - LANES if j == qi else nrows
            mx = diag_tile.max(axis=0, keepdims=True) if diag_tile is not None else None
            if nplain:
                mp = jnp.max(z_ref[g, 0:nplain, :], axis=0, keepdims=True)
                mx = mp if mx is None else jnp.maximum(mx, mp)
            mg = m[:, sl]
            mn = jnp.maximum(mg, mx + c_blk)
            alpha = jnp.exp2(mg - mn)
            shift = mn - c_blk
            lsum = None
            for r0 in range(0, nrows, LANES):
                zt = diag_tile if (j == qi and r0 == nplain) else z_ref[g, r0:r0 + LANES, :]
                pt = jnp.exp2(zt - shift)
                part = jnp.sum(pt, axis=0, keepdims=True)
                lsum = part if lsum is None else lsum + part
                p_ref[g, r0:r0 + LANES, :] = pt.astype(BF16)
            ls.append(alpha * l[:, sl] + lsum)
            ms.append(mn)
            alphas.append(alpha)
        return jnp.concatenate(ms, axis=1), jnp.concatenate(ls, axis=1), jnp.concatenate(alphas, axis=1)

    def finalize(qi, l):
        on = jnp.concatenate([acc_sc[g] for g in range(ng)], axis=1) / l
        ot = on[:, :tq] - lam_full * on[:, tq:]
        ot = ot * lax.rsqrt(jnp.mean(ot * ot, axis=0, keepdims=True) + SUBLN_EPS)
        g = g_ref[qi * tq:(qi + 1) * tq, :].astype(F32)
        o_ref[qi * tq:(qi + 1) * tq, :] = (ot.T * sw_scaled * (g / (1.0 + jnp.exp2(g * (-LOG2E))))).astype(o_ref.dtype)

    items = [(qi, j) for qi in range(nq) for j in range(qi + 1)]
    zs, ps = (z_a, z_b), (p_a, p_b)
    load_q(0)
    scores_to(0, z_a, True)
    m = l = alpha = None
    for n, (qi, j) in enumerate(items):
        if n > 0:
            pqi, pj = items[n - 1]
            accumulate(pj, ps[(n - 1) % 2], alpha, pj == 0, pj == pqi)
            if pqi != qi:
                finalize(pqi, l)
        if j == 0:
            m = jnp.full((1, 2 * tq), NEG_BIG, F32)
            l = jnp.zeros((1, 2 * tq), F32)
        if n + 1 < len(items):
            nqi, nj = items[n + 1]
            if nqi != qi:
                load_q(nqi)
            scores_to(nj, zs[(n + 1) % 2], nj == nqi)
        m, l, alpha = softmax(qi, j, zs[n % 2], ps[n % 2], m, l)
    pqi, pj = items[-1]
    accumulate(pj, ps[(len(items) - 1) % 2], alpha, pj == 0, pj == pqi)
    finalize(pqi, l)


def _diff_attn(qkvg, lam, sw, *, batch, seq, tq, lambda_init):
    m, n4 = qkvg.shape
    w = n4 // 4
    nh = w // LANES
    ngl = 2 * tq // LANES
    per_step = ATTN_HEADS_PER_STEP if nh % ATTN_HEADS_PER_STEP == 0 else 1
    wide = per_step * LANES
    nblk = nh // per_step

    def col_spec(p):
        return pl.BlockSpec((seq, wide), lambda b, h: (b, p * nblk + h))

    return pl.pallas_call(
        functools.partial(_attn_kernel, tq=tq, n_heads=nh, lambda_init=lambda_init),
        grid=(batch, nblk),
        in_specs=[col_spec(0), col_spec(1), col_spec(2), col_spec(3),
                  pl.BlockSpec(lam.shape, lambda b, h: (0, 0)),
                  pl.BlockSpec((1, LANES), lambda b, h: (0, 0))],
        out_specs=pl.BlockSpec((seq, wide), lambda b, h: (b, h)),
        out_shape=jax.ShapeDtypeStruct((m, w), BF16),
        scratch_shapes=[pltpu.VMEM((ngl, tq, LANES), F32), pltpu.VMEM((ngl, tq, LANES), F32),
                        pltpu.VMEM((ngl, tq, LANES), BF16), pltpu.VMEM((ngl, tq, LANES), BF16),
                        pltpu.VMEM((ngl, LANES, LANES), F32),
                        pltpu.VMEM((seq, LANES), BF16), pltpu.VMEM((seq, LANES), BF16),
                        pltpu.VMEM((tq, LANES), BF16), pltpu.VMEM((tq, LANES), BF16)],
        compiler_params=_cparams(("parallel", "parallel")),
        name="diff_attn",
    )(qkvg, qkvg, qkvg, qkvg, lam, sw)


def kernel(x, norm_w, rwkv_mu, rwkv_w_in, rwkv_w0, rwkv_w1, rwkv_w2, rwkv_a0, rwkv_a1, rwkv_a2, rwkv_k_k, rwkv_k_a, rwkv_r_k, rwkv_ln_w, rwkv_ln_b, rwkv_w_out, diff_w_in, diff_lambda, diff_subln_w, diff_w_out, final_norm_w):
    batch, seq, d = x.shape
    m = batch * seq
    assert d % LANES == 0 and seq % CHUNK == 0
    assert norm_w.shape[0] == 2 and rwkv_mu.shape[0] == 1 and diff_w_in.shape[0] == 1
    tm_big = min(MM_TILE, m)
    tn_big = min(MM_TILE, d)
    tm_prep = min(ROW_TILE, seq)
    tm_last = min(ROW_TILE, m)
    tq = min(ATTN_BLOCK, seq)

    x2 = x.reshape(m, d)
    row = lambda a: a.reshape(1, -1)
    lora = rwkv_w1.shape[-1]
    pad_c = lambda a: jnp.pad(a, ((0, 0), (0, LORA_PAD - lora)))
    pad_r = lambda a: jnp.pad(a, ((0, LORA_PAD - lora), (0, 0))).astype(BF16)

    mu = rwkv_mu[0]
    mu_rkvg = jnp.stack([mu[0], mu[2], mu[3], mu[5]])
    lora_in = jnp.concatenate([pad_c(rwkv_w1[0]), pad_c(rwkv_a1[0])], axis=1)
    lora_shift = jnp.concatenate([pad_c(mu[1][:, None] * rwkv_w1[0]), pad_c(mu[4][:, None] * rwkv_a1[0])], axis=1)
    wl1 = jnp.concatenate([lora_in, lora_shift], axis=0).astype(BF16)
    xs, lw, ic = _rwkv_prep(x2, row(norm_w[0]), mu_rkvg, row(rwkv_w0[0]), wl1, pad_r(rwkv_w2[0]),
                            row(rwkv_a0[0]), pad_r(rwkv_a2[0]), batch=batch, seq=seq, tm=tm_prep)
    proj = _matmul_wcast(xs, rwkv_w_in[0], tm=tm_big, tn=tn_big, out_dtype=F32, groups=4)
    yg = _wkv(proj, lw, ic, row(rwkv_k_k[0]), row(rwkv_k_a[0]), row(rwkv_r_k[0]), row(rwkv_ln_w[0]),
              row(rwkv_ln_b[0]), batch=batch, seq=seq)
    x1, h1 = _out_proj(yg, rwkv_w_out[0], x2, row(norm_w[1]), tm=tm_last, keep_sum=True,
                       norm_dtype=BF16)

    lambda_init = 0.8 - 0.6 * math.exp(-0.3 * 1)
    qkvg = _matmul_wcast(h1, diff_w_in[0], tm=min(MM_ROWS_BF16_OUT, m), tn=tn_big, out_dtype=BF16,
                         vmem_limit=MM_VMEM_LIMIT_BF16_OUT)
    att = _diff_attn(qkvg, diff_lambda[0], row(diff_subln_w[0]), batch=batch, seq=seq, tq=tq,
                     lambda_init=lambda_init)
    (out,) = _out_proj(att, diff_w_out[0], x1, row(final_norm_w), tm=tm_last, keep_sum=False,
                       norm_dtype=F32)
    return out.reshape(batch, seq, d)
```
